```python
import math
import jax, jax.numpy as jnp
from jax import lax
import numpy as np

D_MODEL = 1024
BATCH = 16
SEQ = 256
DEPTH = 1
DEC_BATCH = 2
DEC_SEQ = 2048
PAST_LEN = 256

GRID_W = 64
D_MIX = D_MODEL
D_POOL = D_MIX // 2
POOL_GROUPS = 4
POOL_CH = D_POOL // POOL_GROUPS
POOL_WINDOWS = (2, 4, 8, 16)
D_GLA_V = D_MIX - D_POOL
GLA_HEADS = 4
GLA_DV = D_GLA_V // GLA_HEADS
D_GLA_K = D_GLA_V // 2
GLA_DK = D_GLA_K // GLA_HEADS
GATE_RANK = 16
GATE_NORM = 16.0
GLA_CHUNK = 64
N_DIR = 2
D_IN = D_POOL + 2 * D_GLA_K + D_GLA_V + N_DIR * GATE_RANK + D_GLA_V
PEER_HEADS = 8
N_KEYS = 128
N_EXPERTS = N_KEYS * N_KEYS
PEER_DQ = 256
PEER_DQ_HALF = PEER_DQ // 2
PEER_TOPK = 16
PEER_BLOCK = 128
ALPHA = (2.0 * DEPTH) ** 0.25
BETA = (8.0 * DEPTH) ** -0.25
EPS = 1e-5

kernel_name = "hybrid_pool_gla_peer_diffusion_step"


def layer_norm(x, g, b):
    xf = x.astype(jnp.float32)
    mu = jnp.mean(xf, axis=-1, keepdims=True)
    var = jnp.mean(jnp.square(xf - mu), axis=-1, keepdims=True)
    return ((xf - mu) * lax.rsqrt(var + EPS) * g + b).astype(x.dtype)


def grid_pos_embed(T, dtype):
    rows = T // GRID_W
    r, col = jnp.meshgrid(jnp.arange(rows), jnp.arange(GRID_W), indexing="ij")

    def sincos(pos, dim):
        omega = 1.0 / (10000.0 ** (jnp.arange(dim // 2, dtype=jnp.float32) / (dim // 2)))
        ang = pos.reshape(-1).astype(jnp.float32)[:, None] * omega[None, :]
        return jnp.concatenate([jnp.sin(ang), jnp.cos(ang)], axis=-1)

    pe = jnp.concatenate([sincos(r, D_MODEL // 2), sincos(col, D_MODEL // 2)], axis=-1)
    return pe.astype(dtype)


def multiscale_pool(xp, pool_w, pool_scale):
    B, T, _ = xp.shape
    xg = xp.reshape(B, T, POOL_GROUPS, POOL_CH).astype(jnp.float32)
    cs = jnp.concatenate([jnp.zeros((B, 1, POOL_GROUPS, POOL_CH), jnp.float32),
                          jnp.cumsum(xg, axis=1)], axis=1)
    t = jnp.arange(T)[:, None]
    win = jnp.array(POOL_WINDOWS)[None, :]
    lo = jnp.clip(t - win // 2, 0, T)
    hi = jnp.clip(t - win // 2 + win, 0, T)
    gidx = jnp.arange(POOL_GROUPS)[None, :]
    window_sum = cs[:, hi, gidx] - cs[:, lo, gidx]
    mean = window_sum / (hi - lo).astype(jnp.float32)[None, :, :, None]
    d = mean - xg
    y = jnp.einsum("btgc,gcd->btgd", d, pool_w) * pool_scale.reshape(POOL_GROUPS, POOL_CH)
    return y.reshape(B, T, D_POOL).astype(xp.dtype)


def gla_scan(q, k, v, g, s0):
    B, H, T, DK = q.shape
    DV = v.shape[-1]
    n = T // GLA_CHUNK

    def chunks(a):
        return jnp.moveaxis(a.reshape(B, H, n, GLA_CHUNK, a.shape[-1]), 2, 0)

    qc = chunks(q.astype(jnp.float32))
    kc = chunks(k.astype(jnp.float32))
    vc = chunks(v.astype(jnp.float32))
    bc = jnp.cumsum(chunks(g.astype(jnp.float32)), axis=-2)
    lower = jnp.tril(jnp.ones((GLA_CHUNK, GLA_CHUNK), dtype=bool))[:, :, None]

    def step(S, inp):
        qi, ki, vi, bi = inp
        diff = bi[..., :, None, :] - bi[..., None, :, :]
        decay = jnp.exp(jnp.where(lower, diff, -jnp.inf))
        scores = jnp.einsum("bhtd,bhsd,bhtsd->bhts", qi, ki, decay)
        o = (jnp.einsum("bhts,bhsv->bhtv", scores, vi)
             + jnp.einsum("bhtd,bhdv->bhtv", qi * jnp.exp(bi), S))
        b_end = bi[..., -1:, :]
        S = (jnp.exp(b_end[..., 0, :])[..., None] * S
             + jnp.einsum("bhsd,bhsv->bhdv", ki * jnp.exp(b_end - bi), vi))
        return S, o

    S_fin, oc = lax.scan(step, s0.astype(jnp.float32), (qc, kc, vc, bc))
    o = jnp.moveaxis(oc, 0, 2).reshape(B, H, T, DV)
    return o, S_fin.astype(s0.dtype)


def gla_bidirectional(q, k, v, g_f, g_b, s0_f, s0_b):
    o_f, s_f = gla_scan(q, k, v, g_f, s0_f)
    flip = lambda a: jnp.flip(a, axis=2)
    o_b, s_b = gla_scan(flip(q), flip(k), flip(v), flip(g_b), s0_b)
    return o_f + flip(o_b), s_f, s_b


def peer(h, wq, keys, U, V):
    B, T, D = h.shape
    xt = h.reshape(-1, PEER_BLOCK, D)

    def block(xb):
        q = (xb @ wq).reshape(PEER_BLOCK, PEER_HEADS, 2, PEER_DQ_HALF)
        s = jnp.einsum("nhpd,hpkd->nhpk", q, keys)
        sv, si = lax.top_k(s, PEER_TOPK)
        cand = (sv[:, :, 0, :, None] + sv[:, :, 1, None, :]).reshape(PEER_BLOCK, PEER_HEADS, PEER_TOPK * PEER_TOPK)
        fv, fi = lax.top_k(cand, PEER_TOPK)
        i1 = jnp.take_along_axis(si[:, :, 0], fi // PEER_TOPK, axis=-1)
        i2 = jnp.take_along_axis(si[:, :, 1], fi % PEER_TOPK, axis=-1)
        e = i1 * N_KEYS + i2
        gate = jax.nn.softmax(fv.astype(jnp.float32), axis=-1)
        u = jnp.take(U, e, axis=0)
        vv = jnp.take(V, e, axis=0)
        a = jax.nn.gelu(jnp.einsum("nd,nhkd->nhk", xb, u))
        out = jnp.einsum("nhk,nhkd->nd", gate * a, vv)
        return out.astype(xb.dtype)

    return lax.map(block, xt).reshape(B, T, D)


def trunk_layer(x, cond, s0_f, s0_b, w_ada, b_ada, w_in, pool_w, pool_scale, gk_w, gk_b,
                gla_norm_g, w_o, ln1_g, ln1_b, peer_wq, peer_keys, peer_u, peer_v, ln2_g, ln2_b):
    B, T, _ = x.shape
    mod = (jax.nn.silu(cond) @ w_ada + b_ada)[:, None, :]
    sh1, sc1, g1, sh2, sc2, g2 = jnp.split(mod, 6, axis=-1)

    h = x * (1.0 + sc1) + sh1
    p = h @ w_in
    cuts = np.cumsum([D_POOL, D_GLA_K, D_GLA_K, D_GLA_V, N_DIR * GATE_RANK])
    xp, q, k, v, glr, og = jnp.split(p, cuts, axis=-1)

    pool_out = multiscale_pool(xp, pool_w, pool_scale)

    heads = lambda a, d: a.reshape(B, T, GLA_HEADS, d).transpose(0, 2, 1, 3)
    q = heads(q, GLA_DK) * (GLA_DK ** -0.5)
    k = heads(k, GLA_DK)
    v = heads(v, GLA_DV)
    glr = glr.reshape(B, T, N_DIR, GATE_RANK)
    gk = jax.nn.log_sigmoid(
        (jnp.einsum("btnr,nrk->nbtk", glr, gk_w) + gk_b[:, None, None, :]).astype(jnp.float32)) / GATE_NORM
    gk = gk.reshape(N_DIR, B, T, GLA_HEADS, GLA_DK).transpose(0, 1, 3, 2, 4)
    o, s_f, s_b = gla_bidirectional(q, k, v, gk[0], gk[1], s0_f, s0_b)
    o = o * lax.rsqrt(jnp.mean(jnp.square(o), axis=-1, keepdims=True) + EPS)
    o = o * gla_norm_g.reshape(GLA_HEADS, 1, GLA_DV)
    gla_out = (o.transpose(0, 2, 1, 3).reshape(B, T, D_GLA_V) * jax.nn.silu(og.astype(jnp.float32))).astype(x.dtype)

    mix = jnp.concatenate([pool_out, gla_out], axis=-1) @ w_o
    x = layer_norm(ALPHA * x + g1 * mix, ln1_g, ln1_b)

    h2 = x * (1.0 + sc2) + sh2
    ffn = peer(h2, peer_wq, peer_keys, peer_u, peer_v)
    x = layer_norm(ALPHA * x + g2 * ffn, ln2_g, ln2_b)
    return x, s_f, s_b


def setup_inputs(seed: int = 0) -> dict:
    key = jax.random.key(seed)
    ks = jax.random.split(key, 24)
    nrm = lambda k, shape, s: jax.random.normal(k, shape, jnp.float32) * s
    return {
        "x_prompt": nrm(ks[0], (BATCH, SEQ, D_MODEL), 1.0),
        "x_sample": nrm(ks[1], (DEC_BATCH, DEC_SEQ, D_MODEL), 1.0),
        "c": nrm(ks[2], (DEC_BATCH, D_MODEL), 1.0),
        "state_fwd": nrm(ks[3], (DEC_BATCH, DEPTH, GLA_HEADS, GLA_DK, GLA_DV), 1.0),
        "state_bwd": nrm(ks[4], (DEC_BATCH, DEPTH, GLA_HEADS, GLA_DK, GLA_DV), 1.0),
        "c_ctx": nrm(ks[5], (D_MODEL,), 1.0),
        "w_ada": nrm(ks[6], (DEPTH, D_MODEL, 6 * D_MODEL), 0.5 * D_MODEL ** -0.5),
        "b_ada": nrm(ks[7], (DEPTH, 6 * D_MODEL), 0.02),
        "w_in": nrm(ks[8], (DEPTH, D_MODEL, D_IN), D_MODEL ** -0.5),
        "pool_w": nrm(ks[9], (DEPTH, POOL_GROUPS, POOL_CH, POOL_CH), POOL_CH ** -0.5),
        "pool_scale": 1.0 + nrm(ks[10], (DEPTH, D_POOL), 0.02),
        "gk_w": nrm(ks[11], (DEPTH, N_DIR, GATE_RANK, D_GLA_K), GATE_RANK ** -0.5),
        "gk_b": nrm(ks[12], (DEPTH, N_DIR, D_GLA_K), 0.1),
        "gla_norm_g": 1.0 + nrm(ks[13], (DEPTH, D_GLA_V), 0.02),
        "w_o": nrm(ks[14], (DEPTH, D_MIX, D_MODEL), BETA * D_MIX ** -0.5),
        "ln1_g": 1.0 + nrm(ks[15], (DEPTH, D_MODEL), 0.02),
        "ln1_b": nrm(ks[16], (DEPTH, D_MODEL), 0.02),
        "peer_wq": nrm(ks[17], (DEPTH, D_MODEL, PEER_HEADS * PEER_DQ), D_MODEL ** -0.5),
        "peer_keys": nrm(ks[18], (DEPTH, PEER_HEADS, 2, N_KEYS, PEER_DQ_HALF), PEER_DQ_HALF ** -0.5),
        "peer_u": nrm(ks[19], (DEPTH, N_EXPERTS, D_MODEL), D_MODEL ** -0.5),
        "peer_v": nrm(ks[20], (DEPTH, N_EXPERTS, D_MODEL), BETA),
        "ln2_g": 1.0 + nrm(ks[21], (DEPTH, D_MODEL), 0.02),
        "ln2_b": nrm(ks[22], (DEPTH, D_MODEL), 0.02),
    }


def reference(x_prompt, x_sample, c, state_fwd, state_bwd, c_ctx, w_ada, b_ada, w_in, pool_w,
              pool_scale, gk_w, gk_b, gla_norm_g, w_o, ln1_g, ln1_b, peer_wq, peer_keys, peer_u,
              peer_v, ln2_g, ln2_b):
    B_ctx = x_prompt.shape[0]
    cond_ctx = jnp.broadcast_to(c_ctx[None, :], (B_ctx, D_MODEL))
    zero_state = jnp.zeros((B_ctx, GLA_HEADS, GLA_DK, GLA_DV), x_prompt.dtype)
    y_prompt = x_prompt
    new_f, new_b = [], []
    for i in range(DEPTH):
        y_prompt, s_f, s_b = trunk_layer(
            y_prompt, cond_ctx, zero_state, zero_state, w_ada[i], b_ada[i], w_in[i], pool_w[i],
            pool_scale[i], gk_w[i], gk_b[i], gla_norm_g[i], w_o[i], ln1_g[i], ln1_b[i], peer_wq[i],
            peer_keys[i], peer_u[i], peer_v[i], ln2_g[i], ln2_b[i])
        new_f.append(s_f)
        new_b.append(s_b)
    new_state_fwd = jnp.stack(new_f, axis=1)
    new_state_bwd = jnp.stack(new_b, axis=1)

    T_lat = x_sample.shape[1]
    y_sample = x_sample + grid_pos_embed(T_lat, x_sample.dtype)[None]
    for i in range(DEPTH):
        y_sample, _, _ = trunk_layer(
            y_sample, c, state_fwd[:, i], state_bwd[:, i], w_ada[i], b_ada[i], w_in[i], pool_w[i],
            pool_scale[i], gk_w[i], gk_b[i], gla_norm_g[i], w_o[i], ln1_g[i], ln1_b[i], peer_wq[i],
            peer_keys[i], peer_u[i], peer_v[i], ln2_g[i], ln2_b[i])

    return (y_prompt, y_sample, new_state_fwd, new_state_bwd)
```

```python
import functools
import math

import numpy as np
import jax
import jax.numpy as jnp
from jax import lax
from jax.experimental import pallas as pl
from jax.experimental.pallas import tpu as pltpu

F32 = jnp.float32
BF16 = jnp.bfloat16

D_MODEL = 1024
N_CTX_SEQ, CTX_LEN = 16, 256
N_LAT_SEQ, LAT_LEN = 2, 2048
GRID_W = 64
N_TOK = N_CTX_SEQ * CTX_LEN + N_LAT_SEQ * LAT_LEN
TILE = 256
N_TILES = N_TOK // TILE
CTX_TILES = N_CTX_SEQ * CTX_LEN // TILE
LAT_TILES = LAT_LEN // TILE

D_POOL = 512
POOL_GROUPS = 4
POOL_CH = 128
POOL_WINDOWS = (2, 4, 8, 16)
POOL_HALO = 8
HEADS = 4
DK = 64
DV = 128
D_K = HEADS * DK
D_V = HEADS * DV
GATE_RANK = 16
GATE_NORM = 16.0
N_LEVELS = 9
P_COLS = 2048 + 128

PEER_HEADS = 8
N_KEYS = 128
PEER_TOPK = 16
N_EXPERTS = N_KEYS * N_KEYS
DENSE_TM = 512
DENSE_CH = 512
ALPHA = 2.0 ** 0.25
EPS = 1e-5
VMEM_LIMIT = 56 * 1024 * 1024


def _cparams(sem):
    return pltpu.CompilerParams(dimension_semantics=sem, vmem_limit_bytes=VMEM_LIMIT)


def _silu(x):
    return x * jax.nn.sigmoid(x)


def _split3(x):
    x1 = x.astype(BF16)
    r1 = x - x1.astype(F32)
    x2 = r1.astype(BF16)
    r2 = r1 - x2.astype(F32)
    x3 = r2.astype(BF16)
    return x1, x2, x3


def _dot_exact(m01, x):
    c = x.shape[1]
    xs = jnp.concatenate(_split3(x), axis=1)
    y = jnp.dot(m01, xs, preferred_element_type=F32)
    return y[:, :c] + y[:, c:2 * c] + y[:, 2 * c:]


def _layer_norm(y, g, b):
    mu = jnp.mean(y, axis=-1, keepdims=True)
    yc = y - mu
    var = jnp.mean(yc * yc, axis=-1, keepdims=True)
    return yc * lax.rsqrt(var + EPS) * g + b


def _cond_row(tile):
    return (tile >= CTX_TILES).astype(jnp.int32) + (tile >= CTX_TILES + LAT_TILES).astype(jnp.int32)


def _mod_kernel(cond_ref, w_ref, b_ref, out_ref):
    s = _silu(cond_ref[...]).astype(BF16)
    out_ref[...] = jnp.dot(s, w_ref[...].astype(BF16), preferred_element_type=F32) + b_ref[...]


def _mod_call(cond8, w_ada, b_ada):
    n = w_ada.shape[1]
    bn = 1536
    return pl.pallas_call(
        _mod_kernel,
        grid=(n // bn,),
        in_specs=[pl.BlockSpec((8, D_MODEL), lambda j: (0, 0)),
                  pl.BlockSpec((D_MODEL, bn), lambda j: (0, j)),
                  pl.BlockSpec((1, bn), lambda j: (0, j))],
        out_specs=pl.BlockSpec((8, bn), lambda j: (0, j)),
        out_shape=jax.ShapeDtypeStruct((8, n), F32),
        compiler_params=_cparams(("arbitrary",)),
        name="mod",
    )(cond8, w_ada, b_ada)


def _load_x(i, xp_ref, xs_ref, pos_ref):
    return jnp.where(i < CTX_TILES, xp_ref[...], xs_ref[...] + pos_ref[...])


def _mod_rows(mod_ref, r, k):
    return mod_ref[pl.ds(r, 1), k * D_MODEL:(k + 1) * D_MODEL]


def _proj_kernel(xp_ref, xs_ref, pos_ref, mod_ref, w_ref, gkw_ref, gkb_ref,
                 oxp, oq, ok, ov, ogk, oog):
    i = pl.program_id(0)
    r = _cond_row(i)
    x = _load_x(i, xp_ref, xs_ref, pos_ref)
    h = x * (1.0 + _mod_rows(mod_ref, r, 1)) + _mod_rows(mod_ref, r, 0)
    p = jnp.dot(h.astype(BF16), w_ref[...], preferred_element_type=F32)
    oxp[...] = p[:, 0:512]
    oq[...] = p[:, 512:768] * (DK ** -0.5)
    ok[...] = p[:, 768:1024]
    ov[...] = p[:, 1024:1536]
    oog[...] = _silu(p[:, 1536:2048])
    glr = p[:, 2048:2176]
    g_hi = glr.astype(BF16)
    g_lo = (glr - g_hi.astype(F32)).astype(BF16)
    w = gkw_ref[...]
    pre = (jnp.dot(g_hi, w, preferred_element_type=F32)
           + jnp.dot(g_lo, w, preferred_element_type=F32) + gkb_ref[...])
    ogk[...] = (jnp.minimum(pre, 0.0) - jnp.log(1.0 + jnp.exp(-jnp.abs(pre)))) * (1.0 / GATE_NORM)


def _x_specs():
    return [pl.BlockSpec((TILE, D_MODEL), lambda i: (jnp.minimum(i, CTX_TILES - 1), 0)),
            pl.BlockSpec((TILE, D_MODEL), lambda i: (jnp.maximum(i - CTX_TILES, 0), 0)),
            pl.BlockSpec((TILE, D_MODEL), lambda i: (jnp.maximum(i - CTX_TILES, 0) % LAT_TILES, 0))]


def _proj_call(xp2, xs2, pos, mod, w_in_r, gkw_bd, gkb):
    tok = lambda c: pl.BlockSpec((TILE, c), lambda i: (i, 0))
    full = lambda a: pl.BlockSpec(a.shape, lambda i: (0,) * a.ndim)
    outs = [512, 256, 256, 512, 512, 512]
    return pl.pallas_call(
        _proj_kernel,
        grid=(N_TILES,),
        in_specs=_x_specs() + [full(mod), full(w_in_r), full(gkw_bd), full(gkb)],
        out_specs=[tok(c) for c in outs],
        out_shape=[jax.ShapeDtypeStruct((N_TOK, c), F32) for c in outs],
        compiler_params=_cparams(("arbitrary",)),
        name="proj",
    )(xp2, xs2, pos, mod, w_in_r, gkw_bd, gkb)


def _gla_constants(rev):
    t = np.arange(TILE)
    tri = (t[None, :] >= t[:, None]) if rev else (t[None, :] <= t[:, None])
    msel = np.zeros((N_LEVELS, TILE, TILE), np.float32)
    for l in range(N_LEVELS - 1):
        b = 1 << l
        ref = (t // (2 * b)) * (2 * b) + (b if rev else b - 1)
        msel[l, t, ref] = 1.0
    msel[N_LEVELS - 1, t, t] = 1.0
    x = t[:, None] ^ t[None, :]
    lvl = np.where(x > 0, np.floor(np.log2(np.maximum(x, 1))), N_LEVELS - 1).astype(np.int32)
    causal = (t[:, None] <= t[None, :]) if rev else (t[:, None] >= t[None, :])
    lvl = np.where(causal, lvl, -1).astype(np.int32)
    lmap = np.tile(lvl, (1, HEADS))
    return (jnp.asarray(tri, BF16), jnp.asarray(msel.reshape(N_LEVELS * TILE, TILE), BF16),
            jnp.asarray(lmap, jnp.int32))


def _gla_tile(q, k, v, g, tri, msel, lmap, st_ref, rev):
    b = _dot_exact(tri, g)
    b_end = b[0:1, :] if rev else b[TILE - 1:TILE, :]
    st = st_ref[...]
    cdim = (((1,), (1,)), ((), ()))
    qe = (q * jnp.exp(b)).astype(BF16)
    o = lax.dot_general(qe, st.astype(BF16), cdim, preferred_element_type=F32)

    refs = _dot_exact(msel, b)
    k_head = lax.broadcasted_iota(jnp.int32, (TILE, D_K), 1) // DK
    p = jnp.zeros((TILE, HEADS * TILE), F32)
    for l in range(N_LEVELS):
        d = b - refs[l * TILE:(l + 1) * TILE]
        qd = (q * jnp.exp(jnp.minimum(d, 0.0))).astype(BF16)
        kd = (k * jnp.exp(jnp.minimum(-d, 0.0))).astype(BF16)
        kbd = jnp.concatenate([jnp.where(k_head == h, kd, jnp.zeros_like(kd)) for h in range(HEADS)], axis=0)
        z = lax.dot_general(qd, kbd, cdim, preferred_element_type=F32)
        p = jnp.where(lmap == l, z, p)
    vb = v.astype(BF16)
    v_head = lax.broadcasted_iota(jnp.int32, (TILE, D_V), 1) // DV
    vbd = jnp.concatenate([jnp.where(v_head == h, vb, jnp.zeros_like(vb)) for h in range(HEADS)], axis=0)
    o = o + jnp.dot(p.astype(BF16), vbd, preferred_element_type=F32)

    kdec = (k * jnp.exp(b_end - b)).astype(BF16)
    kv = lax.dot_general(vb, kdec, (((0,), (0,)), ((), ())), preferred_element_type=F32)
    rh = lax.broadcasted_iota(jnp.int32, (D_V, D_K), 0) // DV
    ch = lax.broadcasted_iota(jnp.int32, (D_V, D_K), 1) // DK
    st_ref[...] = st * jnp.exp(b_end) + jnp.where(rh == ch, kv, 0.0)
    return o


def _gla_fwd_kernel(q_ref, k_ref, v_ref, g_ref, st0_ref, tri_ref, msel_ref, lmap_ref,
                    of_ref, stout_ref, st_scr):
    i = pl.program_id(0)
    seq_start = (i <= CTX_TILES) | (i == CTX_TILES + LAT_TILES)

    @pl.when(seq_start)
    def _():
        st_scr[...] = st0_ref[0]

    of_ref[...] = _gla_tile(q_ref[...], k_ref[...], v_ref[...], g_ref[...], tri_ref[...], msel_ref[...],
                            lmap_ref[...], st_scr, rev=False)

    @pl.when(i < CTX_TILES)
    def _():
        stout_ref[0] = st_scr[...]


def _gla_fwd_call(q, k, v, gk, st0, consts):
    tri, msel, lmap = consts
    tok = lambda c: pl.BlockSpec((TILE, c), lambda i: (i, 0))
    full = lambda a: pl.BlockSpec(a.shape, lambda i: (0,) * a.ndim)
    return pl.pallas_call(
        _gla_fwd_kernel,
        grid=(N_TILES,),
        in_specs=[tok(D_K), tok(D_K), tok(D_V), pl.BlockSpec((TILE, D_K), lambda i: (i, 0)),
                  pl.BlockSpec((1, D_V, D_K), lambda i: (_cond_row(i), 0, 0)),
                  full(tri), full(msel), full(lmap)],
        out_specs=[tok(D_V),
                   pl.BlockSpec((1, D_V, D_K), lambda i: (jnp.minimum(i, CTX_TILES - 1), 0, 0))],
        out_shape=[jax.ShapeDtypeStruct((N_TOK, D_V), F32),
                   jax.ShapeDtypeStruct((N_CTX_SEQ, D_V, D_K), F32)],
        scratch_shapes=[pltpu.VMEM((D_V, D_K), F32)],
        compiler_params=_cparams(("arbitrary",)),
        name="gla_fwd",
    )(q, k, v, gk, st0, tri, msel, lmap)


def _pool_band():
    t = np.arange(TILE)[:, None]
    e = np.arange(TILE + 2 * POOL_HALO)[None, :]
    bands = []
    for w in POOL_WINDOWS:
        lo = t + POOL_HALO - w // 2
        bands.append(((e >= lo) & (e < lo + w)).astype(np.float32))
    return jnp.asarray(np.stack(bands), BF16)


def _gla_bwd_kernel(q_ref, k_ref, v_ref, g_ref, st0_ref, tri_ref, msel_ref, lmap_ref,
                    of_ref, xpool_ref, prev_ref, next_ref, ogs_ref, xp_ref, xs_ref, pos_ref, mod_ref,
                    band_ref, poolw_ref, pscale_ref, gnorm_ref, wo_ref, ln_g_ref, ln_b_ref,
                    x1_ref, stout_ref, st_scr):
    j = N_TILES - 1 - pl.program_id(0)
    lat_idx = jnp.maximum(j - CTX_TILES, 0) % LAT_TILES
    is_ctx = j < CTX_TILES
    seq_first = is_ctx | (lat_idx == 0)
    seq_last = is_ctx | (lat_idx == LAT_TILES - 1)

    @pl.when(seq_last)
    def _():
        st_scr[...] = st0_ref[0]

    o_b = _gla_tile(q_ref[...], k_ref[...], v_ref[...], g_ref[...], tri_ref[...], msel_ref[...],
                    lmap_ref[...], st_scr, rev=True)

    @pl.when(is_ctx)
    def _():
        stout_ref[0] = st_scr[...]

    xpool = xpool_ref[...]
    prev = jnp.where(seq_first, 0.0, prev_ref[...])
    nxt = jnp.where(seq_last, 0.0, next_ref[...])
    ext = jnp.concatenate([prev, xpool, nxt], axis=0)
    tpos = jnp.where(is_ctx, 0, lat_idx) * TILE + lax.broadcasted_iota(jnp.int32, (TILE, POOL_CH), 0)
    seq_len = jnp.where(is_ctx, CTX_LEN, LAT_LEN)
    pool_parts = []
    for gi, w in enumerate(POOL_WINDOWS):
        sl = slice(gi * POOL_CH, (gi + 1) * POOL_CH)
        wsum = _dot_exact(band_ref[gi], ext[:, sl])
        lo = jnp.maximum(tpos - w // 2, 0)
        hi = jnp.minimum(tpos - w // 2 + w, seq_len)
        dmean = wsum / (hi - lo).astype(F32) - xpool[:, sl]
        y = jnp.dot(dmean.astype(BF16), poolw_ref[gi], preferred_element_type=F32)
        pool_parts.append(y * pscale_ref[:, sl])

    o = of_ref[...] + o_b
    ogs = ogs_ref[...]
    for h in range(HEADS):
        sl = slice(h * DV, (h + 1) * DV)
        oh = o[:, sl]
        oh = oh * lax.rsqrt(jnp.mean(oh * oh, axis=-1, keepdims=True) + EPS)
        pool_parts.append(oh * gnorm_ref[:, sl] * ogs[:, sl])
    mix_in = jnp.concatenate(pool_parts, axis=1).astype(BF16)
    mix = jnp.dot(mix_in, wo_ref[...], preferred_element_type=F32)

    r = _cond_row(j)
    x = _load_x(j, xp_ref, xs_ref, pos_ref)
    y = ALPHA * x + _mod_rows(mod_ref, r, 2) * mix
    x1_ref[...] = _layer_norm(y, ln_g_ref[...], ln_b_ref[...])


def _gla_bwd_call(q, k, v, gk, st0, consts, o_f, xpool, ogs, xp2, xs2, pos, mod,
                  band, poolw, pscale, gnorm, wo, ln_g, ln_b):
    tri, msel, lmap = consts
    rv = lambda i: N_TILES - 1 - i
    tok = lambda c: pl.BlockSpec((TILE, c), lambda i: (rv(i), 0))
    full = lambda a: pl.BlockSpec(a.shape, lambda i: (0,) * a.ndim)
    halo_blocks = TILE // POOL_HALO
    n_halo = N_TOK // POOL_HALO
    xspecs = [pl.BlockSpec((TILE, D_MODEL), lambda i: (jnp.minimum(rv(i), CTX_TILES - 1), 0)),
              pl.BlockSpec((TILE, D_MODEL), lambda i: (jnp.maximum(rv(i) - CTX_TILES, 0), 0)),
              pl.BlockSpec((TILE, D_MODEL), lambda i: (jnp.maximum(rv(i) - CTX_TILES, 0) % LAT_TILES, 0))]
    return pl.pallas_call(
        _gla_bwd_kernel,
        grid=(N_TILES,),
        in_specs=[tok(D_K), tok(D_K), tok(D_V), pl.BlockSpec((TILE, D_K), lambda i: (rv(i), 1)),
                  pl.BlockSpec((1, D_V, D_K), lambda i: (_cond_row(rv(i)), 0, 0)),
                  full(tri), full(msel), full(lmap),
                  tok(D_V), tok(D_POOL),
                  pl.BlockSpec((POOL_HALO, D_POOL), lambda i: (jnp.maximum(rv(i) * halo_blocks - 1, 0), 0)),
                  pl.BlockSpec((POOL_HALO, D_POOL),
                               lambda i: (jnp.minimum((rv(i) + 1) * halo_blocks, n_halo - 1), 0)),
                  tok(D_V)] + xspecs + [full(mod), full(band), full(poolw), full(pscale), full(gnorm),
                                        full(wo), full(ln_g), full(ln_b)],
        out_specs=[tok(D_MODEL),
                   pl.BlockSpec((1, D_V, D_K), lambda i: (jnp.minimum(rv(i), CTX_TILES - 1), 0, 0))],
        out_shape=[jax.ShapeDtypeStruct((N_TOK, D_MODEL), F32),
                   jax.ShapeDtypeStruct((N_CTX_SEQ, D_V, D_K), F32)],
        scratch_shapes=[pltpu.VMEM((D_V, D_K), F32)],
        compiler_params=_cparams(("arbitrary",)),
        name="gla_bwd_mix",
    )(q, k, v, gk, st0, tri, msel, lmap, o_f, xpool, xpool, xpool, ogs, xp2, xs2, pos, mod,
      band, poolw, pscale, gnorm, wo, ln_g, ln_b)


def _top_values(x, n):
    vals = []
    for _ in range(n):
        m = jnp.max(x, axis=0, keepdims=True)
        vals.append(m)
        x = jnp.where(x == m, -jnp.inf, x)
    return vals


def _route_kernel(x1_ref, mod_ref, wqt_ref, keys_ref, h2t_ref, s1_ref, s2_ref, g1_ref, e2_ref, thr_ref):
    i = pl.program_id(0)
    r = _cond_row(i)
    h2 = x1_ref[...] * (1.0 + _mod_rows(mod_ref, r, 4)) + _mod_rows(mod_ref, r, 3)
    h2t = h2.T.astype(BF16)
    h2t_ref[...] = h2t
    qt = jnp.dot(wqt_ref[...], h2t, preferred_element_type=F32)
    for h in range(PEER_HEADS):
        s = []
        for p in range(2):
            row = (2 * h + p) * N_KEYS
            qhp = qt[row:row + N_KEYS, :].astype(BF16)
            s.append(jnp.dot(keys_ref[2 * h + p], qhp, preferred_element_type=F32))
        sv1 = _top_values(s[0], PEER_TOPK)
        sv2 = _top_values(s[1], PEER_TOPK)
        a2 = jnp.concatenate(sv2, axis=0)
        cand = jnp.concatenate([a2 + sv1[a] for a in range(PEER_TOPK)], axis=0)
        fv = _top_values(cand, PEER_TOPK)
        denom = jnp.zeros_like(fv[0])
        for f in fv:
            denom = denom + jnp.exp(f - fv[0])
        s1_ref[h] = s[0]
        s2_ref[h] = s[1]
        g1_ref[h] = jnp.exp(s[0] - sv1[0]) / denom
        e2_ref[h] = jnp.exp(s[1] - sv2[0])
        thr_ref[h] = jnp.broadcast_to(fv[PEER_TOPK - 1], (8, TILE))


def _route_call(x1, mod, wqt, keys):
    full = lambda a: pl.BlockSpec(a.shape, lambda i: (0,) * a.ndim)
    hk = lambda rows: pl.BlockSpec((PEER_HEADS, rows, TILE), lambda i: (0, 0, i))
    return pl.pallas_call(
        _route_kernel,
        grid=(N_TILES,),
        in_specs=[pl.BlockSpec((TILE, D_MODEL), lambda i: (i, 0)), full(mod), full(wqt), full(keys)],
        out_specs=[pl.BlockSpec((D_MODEL, TILE), lambda i: (0, i)),
                   hk(N_KEYS), hk(N_KEYS), hk(N_KEYS), hk(N_KEYS), hk(8)],
        out_shape=[jax.ShapeDtypeStruct((D_MODEL, N_TOK), BF16)]
                  + [jax.ShapeDtypeStruct((PEER_HEADS, N_KEYS, N_TOK), F32)] * 4
                  + [jax.ShapeDtypeStruct((PEER_HEADS, 8, N_TOK), F32)],
        compiler_params=_cparams(("arbitrary",)),
        name="peer_route",
    )(x1, mod, wqt, keys)


def _gelu_tanh(x):
    return 0.5 * x * (1.0 + jnp.tanh(math.sqrt(2.0 / math.pi) * (x + 0.044715 * (x * x * x))))


def _dense_kernel(h2t_ref, s1_ref, s2_ref, g1_ref, e2_ref, thr_ref, u_ref, vt_ref, x1_ref, mod_ref,
                  ln_g_ref, ln_b_ref, y_ref, acc_ref):
    tt = pl.program_id(0)
    c = pl.program_id(1)
    n_sub = DENSE_CH // N_KEYS

    @pl.when(c == 0)
    def _():
        acc_ref[...] = jnp.zeros_like(acc_ref)

    a = jnp.dot(u_ref[...], h2t_ref[...], preferred_element_type=F32)
    w_parts = []
    for ci in range(n_sub):
        i1 = c * n_sub + ci
        act = _gelu_tanh(a[ci * N_KEYS:(ci + 1) * N_KEYS, :])
        wsum = jnp.zeros((N_KEYS, DENSE_TM), F32)
        for h in range(PEER_HEADS):
            s1row = s1_ref[h, pl.ds(i1, 1), :]
            g1row = g1_ref[h, pl.ds(i1, 1), :]
            sel = (s2_ref[h] + s1row) >= thr_ref[h, 0:1, :]
            wsum = wsum + jnp.where(sel, e2_ref[h], 0.0) * g1row
        w_parts.append((act * wsum).astype(BF16))
    w = jnp.concatenate(w_parts, axis=0)
    acc_ref[...] += jnp.dot(vt_ref[0], w, preferred_element_type=F32)

    @pl.when(c == pl.num_programs(1) - 1)
    def _():
        tiles_per = DENSE_TM // TILE
        r = _cond_row(tt * tiles_per)
        ffn = acc_ref[...].T
        y = ALPHA * x1_ref[...] + _mod_rows(mod_ref, r, 5) * ffn
        y_ref[...] = _layer_norm(y, ln_g_ref[...], ln_b_ref[...])


def _dense_call(h2t, s1, s2, g1, e2, thr, u_bf, vt_bf, x1, mod, ln_g, ln_b):
    full = lambda a: pl.BlockSpec(a.shape, lambda t, c: (0,) * a.ndim)
    hk = lambda rows: pl.BlockSpec((PEER_HEADS, rows, DENSE_TM), lambda t, c: (0, 0, t))
    return pl.pallas_call(
        _dense_kernel,
        grid=(N_TOK // DENSE_TM, N_EXPERTS // DENSE_CH),
        in_specs=[pl.BlockSpec((D_MODEL, DENSE_TM), lambda t, c: (0, t)),
                  hk(N_KEYS), hk(N_KEYS), hk(N_KEYS), hk(N_KEYS), hk(8),
                  pl.BlockSpec((DENSE_CH, D_MODEL), lambda t, c: (c, 0)),
                  pl.BlockSpec((1, D_MODEL, DENSE_CH), lambda t, c: (c, 0, 0)),
                  pl.BlockSpec((DENSE_TM, D_MODEL), lambda t, c: (t, 0)),
                  full(mod), full(ln_g), full(ln_b)],
        out_specs=pl.BlockSpec((DENSE_TM, D_MODEL), lambda t, c: (t, 0)),
        out_shape=jax.ShapeDtypeStruct((N_TOK, D_MODEL), F32),
        scratch_shapes=[pltpu.VMEM((D_MODEL, DENSE_TM), F32)],
        compiler_params=_cparams(("arbitrary", "arbitrary")),
        name="peer_dense",
    )(h2t, s1, s2, g1, e2, thr, u_bf, vt_bf, x1, mod, ln_g, ln_b)


def _grid_pos_embed():
    rows = LAT_LEN // GRID_W
    r, col = jnp.meshgrid(jnp.arange(rows), jnp.arange(GRID_W), indexing="ij")

    def sincos(pos, dim):
        omega = 1.0 / (10000.0 ** (jnp.arange(dim // 2, dtype=F32) / (dim // 2)))
        ang = pos.reshape(-1).astype(F32)[:, None] * omega[None, :]
        return jnp.concatenate([jnp.sin(ang), jnp.cos(ang)], axis=-1)

    return jnp.concatenate([sincos(r, D_MODEL // 2), sincos(col, D_MODEL // 2)], axis=-1)


def _state_blockdiag_t(state):
    out = jnp.zeros((3, D_V, D_K), F32)
    for h in range(HEADS):
        out = out.at[1:, h * DV:(h + 1) * DV, h * DK:(h + 1) * DK].set(jnp.swapaxes(state[:, h], 1, 2))
    return out


def _state_from_blockdiag_t(st):
    blocks = [jnp.swapaxes(st[:, h * DV:(h + 1) * DV, h * DK:(h + 1) * DK], 1, 2) for h in range(HEADS)]
    return jnp.stack(blocks, axis=1)[:, None]


def kernel(x_prompt, x_sample, c, state_fwd, state_bwd, c_ctx, w_ada, b_ada, w_in, pool_w, pool_scale,
           gk_w, gk_b, gla_norm_g, w_o, ln1_g, ln1_b, peer_wq, peer_keys, peer_u, peer_v, ln2_g, ln2_b):
    xp2 = x_prompt.reshape(N_CTX_SEQ * CTX_LEN, D_MODEL)
    xs2 = x_sample.reshape(N_LAT_SEQ * LAT_LEN, D_MODEL)
    pos = _grid_pos_embed()

    cond8 = jnp.zeros((8, D_MODEL), F32).at[0].set(c_ctx).at[1:3].set(c)
    mod = _mod_call(cond8, w_ada[0], b_ada[0][None, :])

    w = w_in[0]
    w_in_r = jnp.concatenate([w[:, :1536], w[:, 1568:], w[:, 1536:1568],
                              jnp.zeros((D_MODEL, P_COLS - 2080), F32)], axis=1).astype(BF16)
    gkw_bd = jnp.zeros((128, 2 * D_K), F32)
    gkw_bd = gkw_bd.at[0:GATE_RANK, 0:D_K].set(gk_w[0, 0]).at[GATE_RANK:2 * GATE_RANK, D_K:].set(gk_w[0, 1])
    gkb = gk_b[0].reshape(1, 2 * D_K)
    xpool, q, k, v, gk, ogs = _proj_call(xp2, xs2, pos, mod, w_in_r, gkw_bd.astype(BF16), gkb)

    o_f, st_f = _gla_fwd_call(q, k, v, gk, _state_blockdiag_t(state_fwd[:, 0]), _gla_constants(False))
    x1, st_b = _gla_bwd_call(
        q, k, v, gk, _state_blockdiag_t(state_bwd[:, 0]), _gla_constants(True), o_f, xpool, ogs,
        xp2, xs2, pos, mod, _pool_band(), pool_w[0].astype(BF16), pool_scale[0][None, :],
        gla_norm_g[0][None, :], w_o[0].astype(BF16), ln1_g[0][None, :], ln1_b[0][None, :])

    wqt = peer_wq[0].T.astype(BF16)
    keys = peer_keys[0].reshape(2 * PEER_HEADS, N_KEYS, N_KEYS).astype(BF16)
    h2t, s1, s2, g1, e2, thr = _route_call(x1, mod, wqt, keys)

    u_bf = peer_u[0].astype(BF16)
    n_steps = N_EXPERTS // DENSE_CH
    vt_bf = jnp.swapaxes(peer_v[0].astype(BF16).reshape(n_steps, DENSE_CH, D_MODEL), 1, 2)
    y = _dense_call(h2t, s1, s2, g1, e2, thr, u_bf, vt_bf, x1, mod, ln2_g[0][None, :], ln2_b[0][None, :])

    n_ctx = N_CTX_SEQ * CTX_LEN
    y_prompt = y[:n_ctx].reshape(N_CTX_SEQ, CTX_LEN, D_MODEL)
    y_sample = y[n_ctx:].reshape(N_LAT_SEQ, LAT_LEN, D_MODEL)
    return (y_prompt, y_sample, _state_from_blockdiag_t(st_f), _state_from_blockdiag_t(st_b))
```

```python
import functools
import math

import numpy as np
import jax
import jax.numpy as jnp
from jax import lax
from jax.experimental import pallas as pl
from jax.experimental.pallas import tpu as pltpu

F32 = jnp.float32
BF16 = jnp.bfloat16

D_MODEL = 1024
N_CTX_SEQ, CTX_LEN = 16, 256
N_LAT_SEQ, LAT_LEN = 2, 2048
GRID_W = 64
N_TOK = N_CTX_SEQ * CTX_LEN + N_LAT_SEQ * LAT_LEN
TILE = 256
N_TILES = N_TOK // TILE
CTX_TILES = N_CTX_SEQ * CTX_LEN // TILE
LAT_TILES = LAT_LEN // TILE

D_POOL = 512
POOL_GROUPS = 4
POOL_CH = 128
POOL_WINDOWS = (2, 4, 8, 16)
POOL_HALO = 8
HEADS = 4
DK = 64
DV = 128
D_K = HEADS * DK
D_V = HEADS * DV
GATE_RANK = 16
GATE_NORM = 16.0
N_LEVELS = 9
P_COLS = 2048 + 128

PEER_HEADS = 8
N_KEYS = 128
PEER_TOPK = 16
N_EXPERTS = N_KEYS * N_KEYS
DENSE_TM = 512
DENSE_CH = 1024
ALPHA = 2.0 ** 0.25
EPS = 1e-5
VMEM_LIMIT = 56 * 1024 * 1024


def _cparams(sem):
    return pltpu.CompilerParams(dimension_semantics=sem, vmem_limit_bytes=VMEM_LIMIT)


def _silu(x):
    return x * jax.nn.sigmoid(x)


def _split3(x):
    x1 = x.astype(BF16)
    r1 = x - x1.astype(F32)
    x2 = r1.astype(BF16)
    r2 = r1 - x2.astype(F32)
    x3 = r2.astype(BF16)
    return x1, x2, x3


def _dot_exact(m01, x):
    c = x.shape[1]
    xs = jnp.concatenate(_split3(x), axis=1)
    y = jnp.dot(m01, xs, preferred_element_type=F32)
    return y[:, :c] + y[:, c:2 * c] + y[:, 2 * c:]


def _layer_norm(y, g, b):
    mu = jnp.mean(y, axis=-1, keepdims=True)
    yc = y - mu
    var = jnp.mean(yc * yc, axis=-1, keepdims=True)
    return yc * lax.rsqrt(var + EPS) * g + b


def _cond_row(tile):
    return (tile >= CTX_TILES).astype(jnp.int32) + (tile >= CTX_TILES + LAT_TILES).astype(jnp.int32)


def _mod_kernel(cond_ref, w_ref, b_ref, out_ref):
    s = _silu(cond_ref[...]).astype(BF16)
    out_ref[...] = jnp.dot(s, w_ref[...].astype(BF16), preferred_element_type=F32) + b_ref[...]


def _mod_call(cond8, w_ada, b_ada):
    n = w_ada.shape[1]
    bn = 1536
    return pl.pallas_call(
        _mod_kernel,
        grid=(n // bn,),
        in_specs=[pl.BlockSpec((8, D_MODEL), lambda j: (0, 0)),
                  pl.BlockSpec((D_MODEL, bn), lambda j: (0, j)),
                  pl.BlockSpec((1, bn), lambda j: (0, j))],
        out_specs=pl.BlockSpec((8, bn), lambda j: (0, j)),
        out_shape=jax.ShapeDtypeStruct((8, n), F32),
        compiler_params=_cparams(("arbitrary",)),
        name="mod",
    )(cond8, w_ada, b_ada)


def _load_x(i, xp_ref, xs_ref, pos_ref):
    return jnp.where(i < CTX_TILES, xp_ref[...], xs_ref[...] + pos_ref[...])


def _mod_rows(mod_ref, r, k):
    return mod_ref[pl.ds(r, 1), k * D_MODEL:(k + 1) * D_MODEL]


def _proj_kernel(xp_ref, xs_ref, pos_ref, mod_ref, w_ref, gkw_ref, gkb_ref,
                 oxp, oq, ok, ov, ogk, oog):
    i = pl.program_id(0)
    r = _cond_row(i)
    x = _load_x(i, xp_ref, xs_ref, pos_ref)
    h = x * (1.0 + _mod_rows(mod_ref, r, 1)) + _mod_rows(mod_ref, r, 0)
    p = jnp.dot(h.astype(BF16), w_ref[...], preferred_element_type=F32)
    oxp[...] = p[:, 0:512]
    oq[...] = p[:, 512:768] * (DK ** -0.5)
    ok[...] = p[:, 768:1024]
    ov[...] = p[:, 1024:1536]
    oog[...] = _silu(p[:, 1536:2048])
    glr = p[:, 2048:2176]
    g_hi = glr.astype(BF16)
    g_lo = (glr - g_hi.astype(F32)).astype(BF16)
    w = gkw_ref[...]
    pre = (jnp.dot(g_hi, w, preferred_element_type=F32)
           + jnp.dot(g_lo, w, preferred_element_type=F32) + gkb_ref[...])
    ogk[...] = (jnp.minimum(pre, 0.0) - jnp.log(1.0 + jnp.exp(-jnp.abs(pre)))) * (1.0 / GATE_NORM)


def _x_specs():
    return [pl.BlockSpec((TILE, D_MODEL), lambda i: (jnp.minimum(i, CTX_TILES - 1), 0)),
            pl.BlockSpec((TILE, D_MODEL), lambda i: (jnp.maximum(i - CTX_TILES, 0), 0)),
            pl.BlockSpec((TILE, D_MODEL), lambda i: (jnp.maximum(i - CTX_TILES, 0) % LAT_TILES, 0))]


def _proj_call(xp2, xs2, pos, mod, w_in_r, gkw_bd, gkb):
    tok = lambda c: pl.BlockSpec((TILE, c), lambda i: (i, 0))
    full = lambda a: pl.BlockSpec(a.shape, lambda i: (0,) * a.ndim)
    outs = [512, 256, 256, 512, 512, 512]
    return pl.pallas_call(
        _proj_kernel,
        grid=(N_TILES,),
        in_specs=_x_specs() + [full(mod), full(w_in_r), full(gkw_bd), full(gkb)],
        out_specs=[tok(c) for c in outs],
        out_shape=[jax.ShapeDtypeStruct((N_TOK, c), F32) for c in outs],
        compiler_params=_cparams(("arbitrary",)),
        name="proj",
    )(xp2, xs2, pos, mod, w_in_r, gkw_bd, gkb)


def _gla_constants(rev):
    t = np.arange(TILE)
    tri = (t[None, :] >= t[:, None]) if rev else (t[None, :] <= t[:, None])
    msel = np.zeros((N_LEVELS, TILE, TILE), np.float32)
    for l in range(N_LEVELS - 1):
        b = 1 << l
        ref = (t // (2 * b)) * (2 * b) + (b if rev else b - 1)
        msel[l, t, ref] = 1.0
    msel[N_LEVELS - 1, t, t] = 1.0
    x = t[:, None] ^ t[None, :]
    lvl = np.where(x > 0, np.floor(np.log2(np.maximum(x, 1))), N_LEVELS - 1).astype(np.int32)
    causal = (t[:, None] <= t[None, :]) if rev else (t[:, None] >= t[None, :])
    lvl = np.where(causal, lvl, -1).astype(np.int32)
    lmap = np.tile(lvl, (1, HEADS))
    return (jnp.asarray(tri, BF16), jnp.asarray(msel.reshape(N_LEVELS * TILE, TILE), BF16),
            jnp.asarray(lmap, jnp.int32))


def _gla_tile(q, k, v, g, tri, msel, lmap, st_ref, rev):
    b = _dot_exact(tri, g)
    b_end = b[0:1, :] if rev else b[TILE - 1:TILE, :]
    st = st_ref[...]
    cdim = (((1,), (1,)), ((), ()))
    qe = (q * jnp.exp(b)).astype(BF16)
    o = lax.dot_general(qe, st.astype(BF16), cdim, preferred_element_type=F32)

    refs = _dot_exact(msel, b)
    k_head = lax.broadcasted_iota(jnp.int32, (TILE, D_K), 1) // DK
    p = jnp.zeros((TILE, HEADS * TILE), F32)
    for l in range(N_LEVELS):
        d = b - refs[l * TILE:(l + 1) * TILE]
        qd = (q * jnp.exp(jnp.minimum(d, 0.0))).astype(BF16)
        kd = (k * jnp.exp(jnp.minimum(-d, 0.0))).astype(BF16)
        kbd = jnp.concatenate([jnp.where(k_head == h, kd, jnp.zeros_like(kd)) for h in range(HEADS)], axis=0)
        z = lax.dot_general(qd, kbd, cdim, preferred_element_type=F32)
        p = jnp.where(lmap == l, z, p)
    vb = v.astype(BF16)
    v_head = lax.broadcasted_iota(jnp.int32, (TILE, D_V), 1) // DV
    vbd = jnp.concatenate([jnp.where(v_head == h, vb, jnp.zeros_like(vb)) for h in range(HEADS)], axis=0)
    o = o + jnp.dot(p.astype(BF16), vbd, preferred_element_type=F32)

    kdec = (k * jnp.exp(b_end - b)).astype(BF16)
    kv = lax.dot_general(vb, kdec, (((0,), (0,)), ((), ())), preferred_element_type=F32)
    rh = lax.broadcasted_iota(jnp.int32, (D_V, D_K), 0) // DV
    ch = lax.broadcasted_iota(jnp.int32, (D_V, D_K), 1) // DK
    st_ref[...] = st * jnp.exp(b_end) + jnp.where(rh == ch, kv, 0.0)
    return o


def _gla_fwd_kernel(q_ref, k_ref, v_ref, g_ref, st0_ref, tri_ref, msel_ref, lmap_ref,
                    of_ref, stout_ref, st_scr):
    i = pl.program_id(0)
    seq_start = (i <= CTX_TILES) | (i == CTX_TILES + LAT_TILES)

    @pl.when(seq_start)
    def _():
        st_scr[...] = st0_ref[0]

    of_ref[...] = _gla_tile(q_ref[...], k_ref[...], v_ref[...], g_ref[...], tri_ref[...], msel_ref[...],
                            lmap_ref[...], st_scr, rev=False)

    @pl.when(i < CTX_TILES)
    def _():
        stout_ref[0] = st_scr[...]


def _gla_fwd_call(q, k, v, gk, st0, consts):
    tri, msel, lmap = consts
    tok = lambda c: pl.BlockSpec((TILE, c), lambda i: (i, 0))
    full = lambda a: pl.BlockSpec(a.shape, lambda i: (0,) * a.ndim)
    return pl.pallas_call(
        _gla_fwd_kernel,
        grid=(N_TILES,),
        in_specs=[tok(D_K), tok(D_K), tok(D_V), pl.BlockSpec((TILE, D_K), lambda i: (i, 0)),
                  pl.BlockSpec((1, D_V, D_K), lambda i: (_cond_row(i), 0, 0)),
                  full(tri), full(msel), full(lmap)],
        out_specs=[tok(D_V),
                   pl.BlockSpec((1, D_V, D_K), lambda i: (jnp.minimum(i, CTX_TILES - 1), 0, 0))],
        out_shape=[jax.ShapeDtypeStruct((N_TOK, D_V), F32),
                   jax.ShapeDtypeStruct((N_CTX_SEQ, D_V, D_K), F32)],
        scratch_shapes=[pltpu.VMEM((D_V, D_K), F32)],
        compiler_params=_cparams(("arbitrary",)),
        name="gla_fwd",
    )(q, k, v, gk, st0, tri, msel, lmap)


def _pool_band():
    t = np.arange(TILE)[:, None]
    e = np.arange(TILE + 2 * POOL_HALO)[None, :]
    bands = []
    for w in POOL_WINDOWS:
        lo = t + POOL_HALO - w // 2
        bands.append(((e >= lo) & (e < lo + w)).astype(np.float32))
    return jnp.asarray(np.stack(bands), BF16)


def _gla_bwd_kernel(q_ref, k_ref, v_ref, g_ref, st0_ref, tri_ref, msel_ref, lmap_ref,
                    of_ref, xpool_ref, prev_ref, next_ref, ogs_ref, xp_ref, xs_ref, pos_ref, mod_ref,
                    band_ref, poolw_ref, pscale_ref, gnorm_ref, wo_ref, ln_g_ref, ln_b_ref,
                    x1_ref, stout_ref, st_scr):
    j = N_TILES - 1 - pl.program_id(0)
    lat_idx = jnp.maximum(j - CTX_TILES, 0) % LAT_TILES
    is_ctx = j < CTX_TILES
    seq_first = is_ctx | (lat_idx == 0)
    seq_last = is_ctx | (lat_idx == LAT_TILES - 1)

    @pl.when(seq_last)
    def _():
        st_scr[...] = st0_ref[0]

    o_b = _gla_tile(q_ref[...], k_ref[...], v_ref[...], g_ref[...], tri_ref[...], msel_ref[...],
                    lmap_ref[...], st_scr, rev=True)

    @pl.when(is_ctx)
    def _():
        stout_ref[0] = st_scr[...]

    xpool = xpool_ref[...]
    prev = jnp.where(seq_first, 0.0, prev_ref[...])
    nxt = jnp.where(seq_last, 0.0, next_ref[...])
    ext = jnp.concatenate([prev, xpool, nxt], axis=0)
    tpos = jnp.where(is_ctx, 0, lat_idx) * TILE + lax.broadcasted_iota(jnp.int32, (TILE, POOL_CH), 0)
    seq_len = jnp.where(is_ctx, CTX_LEN, LAT_LEN)
    pool_parts = []
    for gi, w in enumerate(POOL_WINDOWS):
        sl = slice(gi * POOL_CH, (gi + 1) * POOL_CH)
        wsum = _dot_exact(band_ref[gi], ext[:, sl])
        lo = jnp.maximum(tpos - w // 2, 0)
        hi = jnp.minimum(tpos - w // 2 + w, seq_len)
        dmean = wsum / (hi - lo).astype(F32) - xpool[:, sl]
        y = jnp.dot(dmean.astype(BF16), poolw_ref[gi], preferred_element_type=F32)
        pool_parts.append(y * pscale_ref[:, sl])

    o = of_ref[...] + o_b
    ogs = ogs_ref[...]
    for h in range(HEADS):
        sl = slice(h * DV, (h + 1) * DV)
        oh = o[:, sl]
        oh = oh * lax.rsqrt(jnp.mean(oh * oh, axis=-1, keepdims=True) + EPS)
        pool_parts.append(oh * gnorm_ref[:, sl] * ogs[:, sl])
    mix_in = jnp.concatenate(pool_parts, axis=1).astype(BF16)
    mix = jnp.dot(mix_in, wo_ref[...], preferred_element_type=F32)

    r = _cond_row(j)
    x = _load_x(j, xp_ref, xs_ref, pos_ref)
    y = ALPHA * x + _mod_rows(mod_ref, r, 2) * mix
    x1_ref[...] = _layer_norm(y, ln_g_ref[...], ln_b_ref[...])


def _gla_bwd_call(q, k, v, gk, st0, consts, o_f, xpool, ogs, xp2, xs2, pos, mod,
                  band, poolw, pscale, gnorm, wo, ln_g, ln_b):
    tri, msel, lmap = consts
    rv = lambda i: N_TILES - 1 - i
    tok = lambda c: pl.BlockSpec((TILE, c), lambda i: (rv(i), 0))
    full = lambda a: pl.BlockSpec(a.shape, lambda i: (0,) * a.ndim)
    halo_blocks = TILE // POOL_HALO
    n_halo = N_TOK // POOL_HALO
    xspecs = [pl.BlockSpec((TILE, D_MODEL), lambda i: (jnp.minimum(rv(i), CTX_TILES - 1), 0)),
              pl.BlockSpec((TILE, D_MODEL), lambda i: (jnp.maximum(rv(i) - CTX_TILES, 0), 0)),
              pl.BlockSpec((TILE, D_MODEL), lambda i: (jnp.maximum(rv(i) - CTX_TILES, 0) % LAT_TILES, 0))]
    return pl.pallas_call(
        _gla_bwd_kernel,
        grid=(N_TILES,),
        in_specs=[tok(D_K), tok(D_K), tok(D_V), pl.BlockSpec((TILE, D_K), lambda i: (rv(i), 1)),
                  pl.BlockSpec((1, D_V, D_K), lambda i: (_cond_row(rv(i)), 0, 0)),
                  full(tri), full(msel), full(lmap),
                  tok(D_V), tok(D_POOL),
                  pl.BlockSpec((POOL_HALO, D_POOL), lambda i: (jnp.maximum(rv(i) * halo_blocks - 1, 0), 0)),
                  pl.BlockSpec((POOL_HALO, D_POOL),
                               lambda i: (jnp.minimum((rv(i) + 1) * halo_blocks, n_halo - 1), 0)),
                  tok(D_V)] + xspecs + [full(mod), full(band), full(poolw), full(pscale), full(gnorm),
                                        full(wo), full(ln_g), full(ln_b)],
        out_specs=[tok(D_MODEL),
                   pl.BlockSpec((1, D_V, D_K), lambda i: (jnp.minimum(rv(i), CTX_TILES - 1), 0, 0))],
        out_shape=[jax.ShapeDtypeStruct((N_TOK, D_MODEL), F32),
                   jax.ShapeDtypeStruct((N_CTX_SEQ, D_V, D_K), F32)],
        scratch_shapes=[pltpu.VMEM((D_V, D_K), F32)],
        compiler_params=_cparams(("arbitrary",)),
        name="gla_bwd_mix",
    )(q, k, v, gk, st0, tri, msel, lmap, o_f, xpool, xpool, xpool, ogs, xp2, xs2, pos, mod,
      band, poolw, pscale, gnorm, wo, ln_g, ln_b)


def _top_values(x, n, with_rank=False):
    vals = []
    rank = jnp.full(x.shape, float(n), F32)
    for j in range(n):
        m = jnp.max(x, axis=0, keepdims=True)
        vals.append(m)
        hit = x == m
        if with_rank:
            rank = jnp.where(hit, float(j), rank)
        x = jnp.where(hit, -jnp.inf, x)
    return (vals, rank) if with_rank else vals


def _route_kernel(x1_ref, mod_ref, wqt_ref, keys_ref, h2t_ref, cnt_ref, g1_ref, r2_ref, e2_ref):
    i = pl.program_id(0)
    r = _cond_row(i)
    h2 = x1_ref[...] * (1.0 + _mod_rows(mod_ref, r, 4)) + _mod_rows(mod_ref, r, 3)
    h2t = h2.T.astype(BF16)
    h2t_ref[...] = h2t
    qt = jnp.dot(wqt_ref[...], h2t, preferred_element_type=F32)
    for h in range(PEER_HEADS):
        s = []
        for p in range(2):
            row = (2 * h + p) * N_KEYS
            qhp = qt[row:row + N_KEYS, :].astype(BF16)
            s.append(jnp.dot(keys_ref[2 * h + p], qhp, preferred_element_type=F32))
        sv1, rank1 = _top_values(s[0], PEER_TOPK, with_rank=True)
        sv2, rank2 = _top_values(s[1], PEER_TOPK, with_rank=True)
        a1 = jnp.concatenate(sv1, axis=0)
        a2 = jnp.concatenate(sv2, axis=0)
        row8 = lax.broadcasted_iota(jnp.int32, (8, TILE), 0)
        cand = [a2 + sv1[0]]
        for a in range(1, 8):
            cand.append(jnp.where(row8 < PEER_TOPK // (a + 1), a2[0:8] + sv1[a], -jnp.inf))
        cand.append(a1[8:16] + sv2[0])
        fv = _top_values(jnp.concatenate(cand, axis=0), PEER_TOPK)
        denom = jnp.zeros_like(fv[0])
        for f in fv:
            denom = denom + jnp.exp(f - fv[0])
        thr = fv[PEER_TOPK - 1]
        cnt = jnp.zeros_like(rank1)
        for a in range(PEER_TOPK):
            blk = cand[a] if a < 8 else cand[8][a - 8:a - 7]
            n_a = jnp.sum((blk >= thr).astype(F32), axis=0, keepdims=True)
            cnt = jnp.where(rank1 == float(a), n_a, cnt)
        cnt_ref[h] = cnt
        g1_ref[h] = jnp.exp(s[0] - sv1[0]) / denom
        r2_ref[h] = rank2.astype(BF16)
        e2_ref[h] = jnp.exp(s[1] - sv2[0]).astype(BF16)


def _route_call(x1, mod, wqt, keys):
    full = lambda a: pl.BlockSpec(a.shape, lambda i: (0,) * a.ndim)
    hk = pl.BlockSpec((PEER_HEADS, N_KEYS, TILE), lambda i: (0, 0, i))
    per_key = lambda dt: jax.ShapeDtypeStruct((PEER_HEADS, N_KEYS, N_TOK), dt)
    return pl.pallas_call(
        _route_kernel,
        grid=(N_TILES,),
        in_specs=[pl.BlockSpec((TILE, D_MODEL), lambda i: (i, 0)), full(mod), full(wqt), full(keys)],
        out_specs=[pl.BlockSpec((D_MODEL, TILE), lambda i: (0, i)), hk, hk, hk, hk],
        out_shape=[jax.ShapeDtypeStruct((D_MODEL, N_TOK), BF16),
                   per_key(F32), per_key(F32), per_key(BF16), per_key(BF16)],
        compiler_params=_cparams(("arbitrary",)),
        name="peer_route",
    )(x1, mod, wqt, keys)


def _gelu_tanh(x):
    c1 = math.sqrt(2.0 / math.pi)
    c2 = c1 * 0.044715
    return x * (0.5 + 0.5 * jnp.tanh(x * (c1 + c2 * (x * x))))


def _dense_kernel(h2t_ref, cnt_ref, g1_ref, r2_ref, e2_ref, u_ref, vt_ref, x1_ref, mod_ref,
                  ln_g_ref, ln_b_ref, y_ref, acc_ref):
    tt = pl.program_id(0)
    c = pl.program_id(1)
    n_sub = DENSE_CH // N_KEYS

    @pl.when(c == 0)
    def _():
        acc_ref[...] = jnp.zeros_like(acc_ref)

    a = jnp.dot(u_ref[...], h2t_ref[...], preferred_element_type=F32)
    w_parts = []
    for ci in range(n_sub):
        i1 = c * n_sub + ci
        act = _gelu_tanh(a[ci * N_KEYS:(ci + 1) * N_KEYS, :].astype(BF16))
        wsum = jnp.zeros((N_KEYS, DENSE_TM), BF16)
        for h in range(PEER_HEADS):
            cnt_row = cnt_ref[h, pl.ds(i1, 1), :].astype(BF16)
            g1_row = g1_ref[h, pl.ds(i1, 1), :].astype(BF16)
            sel = r2_ref[h] < cnt_row
            wsum = wsum + jnp.where(sel, e2_ref[h], jnp.zeros((), BF16)) * g1_row
        w_parts.append(act * wsum)
    w = jnp.concatenate(w_parts, axis=0)
    acc_ref[...] += jnp.dot(vt_ref[0], w, preferred_element_type=F32)

    @pl.when(c == pl.num_programs(1) - 1)
    def _():
        tiles_per = DENSE_TM // TILE
        r = _cond_row(tt * tiles_per)
        ffn = acc_ref[...].T
        y = ALPHA * x1_ref[...] + _mod_rows(mod_ref, r, 5) * ffn
        y_ref[...] = _layer_norm(y, ln_g_ref[...], ln_b_ref[...])


def _dense_call(h2t, cnt, g1, r2, e2, u_bf, vt_bf, x1, mod, ln_g, ln_b):
    full = lambda a: pl.BlockSpec(a.shape, lambda t, c: (0,) * a.ndim)
    hk = pl.BlockSpec((PEER_HEADS, N_KEYS, DENSE_TM), lambda t, c: (0, 0, t))
    return pl.pallas_call(
        _dense_kernel,
        grid=(N_TOK // DENSE_TM, N_EXPERTS // DENSE_CH),
        in_specs=[pl.BlockSpec((D_MODEL, DENSE_TM), lambda t, c: (0, t)),
                  hk, hk, hk, hk,
                  pl.BlockSpec((DENSE_CH, D_MODEL), lambda t, c: (c, 0)),
                  pl.BlockSpec((1, D_MODEL, DENSE_CH), lambda t, c: (c, 0, 0)),
                  pl.BlockSpec((DENSE_TM, D_MODEL), lambda t, c: (t, 0)),
                  full(mod), full(ln_g), full(ln_b)],
        out_specs=pl.BlockSpec((DENSE_TM, D_MODEL), lambda t, c: (t, 0)),
        out_shape=jax.ShapeDtypeStruct((N_TOK, D_MODEL), F32),
        scratch_shapes=[pltpu.VMEM((D_MODEL, DENSE_TM), F32)],
        compiler_params=_cparams(("arbitrary", "arbitrary")),
        name="peer_dense",
    )(h2t, cnt, g1, r2, e2, u_bf, vt_bf, x1, mod, ln_g, ln_b)


def _grid_pos_embed():
    rows = LAT_LEN // GRID_W
    r, col = jnp.meshgrid(jnp.arange(rows), jnp.arange(GRID_W), indexing="ij")

    def sincos(pos, dim):
        omega = 1.0 / (10000.0 ** (jnp.arange(dim // 2, dtype=F32) / (dim // 2)))
        ang = pos.reshape(-1).astype(F32)[:, None] * omega[None, :]
        return jnp.concatenate([jnp.sin(ang), jnp.cos(ang)], axis=-1)

    return jnp.concatenate([sincos(r, D_MODEL // 2), sincos(col, D_MODEL // 2)], axis=-1)


def _state_blockdiag_t(state):
    out = jnp.zeros((3, D_V, D_K), F32)
    for h in range(HEADS):
        out = out.at[1:, h * DV:(h + 1) * DV, h * DK:(h + 1) * DK].set(jnp.swapaxes(state[:, h], 1, 2))
    return out


def _state_from_blockdiag_t(st):
    blocks = [jnp.swapaxes(st[:, h * DV:(h + 1) * DV, h * DK:(h + 1) * DK], 1, 2) for h in range(HEADS)]
    return jnp.stack(blocks, axis=1)[:, None]


def kernel(x_prompt, x_sample, c, state_fwd, state_bwd, c_ctx, w_ada, b_ada, w_in, pool_w, pool_scale,
           gk_w, gk_b, gla_norm_g, w_o, ln1_g, ln1_b, peer_wq, peer_keys, peer_u, peer_v, ln2_g, ln2_b):
    xp2 = x_prompt.reshape(N_CTX_SEQ * CTX_LEN, D_MODEL)
    xs2 = x_sample.reshape(N_LAT_SEQ * LAT_LEN, D_MODEL)
    pos = _grid_pos_embed()

    cond8 = jnp.zeros((8, D_MODEL), F32).at[0].set(c_ctx).at[1:3].set(c)
    mod = _mod_call(cond8, w_ada[0], b_ada[0][None, :])

    w = w_in[0]
    w_in_r = jnp.concatenate([w[:, :1536], w[:, 1568:], w[:, 1536:1568],
                              jnp.zeros((D_MODEL, P_COLS - 2080), F32)], axis=1).astype(BF16)
    gkw_bd = jnp.zeros((128, 2 * D_K), F32)
    gkw_bd = gkw_bd.at[0:GATE_RANK, 0:D_K].set(gk_w[0, 0]).at[GATE_RANK:2 * GATE_RANK, D_K:].set(gk_w[0, 1])
    gkb = gk_b[0].reshape(1, 2 * D_K)
    xpool, q, k, v, gk, ogs = _proj_call(xp2, xs2, pos, mod, w_in_r, gkw_bd.astype(BF16), gkb)

    o_f, st_f = _gla_fwd_call(q, k, v, gk, _state_blockdiag_t(state_fwd[:, 0]), _gla_constants(False))
    x1, st_b = _gla_bwd_call(
        q, k, v, gk, _state_blockdiag_t(state_bwd[:, 0]), _gla_constants(True), o_f, xpool, ogs,
        xp2, xs2, pos, mod, _pool_band(), pool_w[0].astype(BF16), pool_scale[0][None, :],
        gla_norm_g[0][None, :], w_o[0].astype(BF16), ln1_g[0][None, :], ln1_b[0][None, :])

    wqt = peer_wq[0].T.astype(BF16)
    keys = peer_keys[0].reshape(2 * PEER_HEADS, N_KEYS, N_KEYS).astype(BF16)
    h2t, cnt, g1, r2, e2 = _route_call(x1, mod, wqt, keys)

    u_bf = peer_u[0].astype(BF16)
    n_steps = N_EXPERTS // DENSE_CH
    vt_bf = jnp.swapaxes(peer_v[0].astype(BF16).reshape(n_steps, DENSE_CH, D_MODEL), 1, 2)
    y = _dense_call(h2t, cnt, g1, r2, e2, u_bf, vt_bf, x1, mod, ln2_g[0][None, :], ln2_b[0][None, :])

    n_ctx = N_CTX_SEQ * CTX_LEN
    y_prompt = y[:n_ctx].reshape(N_CTX_SEQ, CTX_LEN, D_MODEL)
    y_sample = y[n_ctx:].reshape(N_LAT_SEQ, LAT_LEN, D_MODEL)
    return (y_prompt, y_sample, _state_from_blockdiag_t(st_f), _state_from_blockdiag_t(st_b))
```

```python
import functools
import math

import numpy as np
import jax
import jax.numpy as jnp
from jax import lax
from jax.experimental import pallas as pl
from jax.experimental.pallas import tpu as pltpu

F32 = jnp.float32
BF16 = jnp.bfloat16

D_MODEL = 1024
N_CTX_SEQ, CTX_LEN = 16, 256
N_LAT_SEQ, LAT_LEN = 2, 2048
GRID_W = 64
N_TOK = N_CTX_SEQ * CTX_LEN + N_LAT_SEQ * LAT_LEN
TILE = 256
N_TILES = N_TOK // TILE
CTX_TILES = N_CTX_SEQ * CTX_LEN // TILE
LAT_TILES = LAT_LEN // TILE

D_POOL = 512
POOL_GROUPS = 4
POOL_CH = 128
POOL_WINDOWS = (2, 4, 8, 16)
POOL_HALO = 8
HEADS = 4
DK = 64
DV = 128
D_K = HEADS * DK
D_V = HEADS * DV
GATE_RANK = 16
GATE_NORM = 16.0
CHUNK = 64
N_LEVELS = 7
P_COLS = 2048 + 128

PEER_HEADS = 8
N_KEYS = 128
PEER_TOPK = 16
N_EXPERTS = N_KEYS * N_KEYS
DENSE_TM = 512
DENSE_CH = 1024
ALPHA = 2.0 ** 0.25
EPS = 1e-5
VMEM_LIMIT = 56 * 1024 * 1024


def _cparams(sem, flags=None):
    return pltpu.CompilerParams(dimension_semantics=sem, vmem_limit_bytes=VMEM_LIMIT, flags=flags)


def _silu(x):
    return x * jax.nn.sigmoid(x)


def _split3(x):
    x1 = x.astype(BF16)
    r1 = x - x1.astype(F32)
    x2 = r1.astype(BF16)
    r2 = r1 - x2.astype(F32)
    x3 = r2.astype(BF16)
    return x1, x2, x3


def _dot_exact(m01, x):
    c = x.shape[1]
    xs = jnp.concatenate(_split3(x), axis=1)
    y = jnp.dot(m01, xs, preferred_element_type=F32)
    return y[:, :c] + y[:, c:2 * c] + y[:, 2 * c:]


def _layer_norm(y, g, b):
    mu = jnp.mean(y, axis=-1, keepdims=True)
    yc = y - mu
    var = jnp.mean(yc * yc, axis=-1, keepdims=True)
    return yc * lax.rsqrt(var + EPS) * g + b


def _cond_row(tile):
    return (tile >= CTX_TILES).astype(jnp.int32) + (tile >= CTX_TILES + LAT_TILES).astype(jnp.int32)


def _mod_kernel(cond_ref, w_ref, b_ref, out_ref):
    s = _silu(cond_ref[...]).astype(BF16)
    out_ref[...] = jnp.dot(s, w_ref[...].astype(BF16), preferred_element_type=F32) + b_ref[...]


def _mod_call(cond8, w_ada, b_ada):
    n = w_ada.shape[1]
    bn = 1536
    return pl.pallas_call(
        _mod_kernel,
        grid=(n // bn,),
        in_specs=[pl.BlockSpec((8, D_MODEL), lambda j: (0, 0)),
                  pl.BlockSpec((D_MODEL, bn), lambda j: (0, j)),
                  pl.BlockSpec((1, bn), lambda j: (0, j))],
        out_specs=pl.BlockSpec((8, bn), lambda j: (0, j)),
        out_shape=jax.ShapeDtypeStruct((8, n), F32),
        compiler_params=_cparams(("arbitrary",)),
        name="mod",
    )(cond8, w_ada, b_ada)


def _load_x(i, xp_ref, xs_ref, pos_ref):
    return jnp.where(i < CTX_TILES, xp_ref[...], xs_ref[...] + pos_ref[...])


def _mod_rows(mod_ref, r, k):
    return mod_ref[pl.ds(r, 1), k * D_MODEL:(k + 1) * D_MODEL]


def _proj_kernel(xp_ref, xs_ref, pos_ref, mod_ref, w_ref, gkw_ref, gkb_ref,
                 oxp, oq, ok, ov, ogk, oog):
    i = pl.program_id(0)
    r = _cond_row(i)
    x = _load_x(i, xp_ref, xs_ref, pos_ref)
    h = x * (1.0 + _mod_rows(mod_ref, r, 1)) + _mod_rows(mod_ref, r, 0)
    p = jnp.dot(h.astype(BF16), w_ref[...], preferred_element_type=F32)
    oxp[...] = p[:, 0:512]
    oq[...] = p[:, 512:768] * (DK ** -0.5)
    ok[...] = p[:, 768:1024]
    ov[...] = p[:, 1024:1536]
    oog[...] = _silu(p[:, 1536:2048])
    glr = p[:, 2048:2176]
    g_hi = glr.astype(BF16)
    g_lo = (glr - g_hi.astype(F32)).astype(BF16)
    w = gkw_ref[...]
    pre = (jnp.dot(g_hi, w, preferred_element_type=F32)
           + jnp.dot(g_lo, w, preferred_element_type=F32) + gkb_ref[...])
    ogk[...] = (jnp.minimum(pre, 0.0) - jnp.log(1.0 + jnp.exp(-jnp.abs(pre)))) * (1.0 / GATE_NORM)


def _x_specs():
    return [pl.BlockSpec((TILE, D_MODEL), lambda i: (jnp.minimum(i, CTX_TILES - 1), 0)),
            pl.BlockSpec((TILE, D_MODEL), lambda i: (jnp.maximum(i - CTX_TILES, 0), 0)),
            pl.BlockSpec((TILE, D_MODEL), lambda i: (jnp.maximum(i - CTX_TILES, 0) % LAT_TILES, 0))]


def _proj_call(xp2, xs2, pos, mod, w_in_r, gkw_bd, gkb):
    tok = lambda c: pl.BlockSpec((TILE, c), lambda i: (i, 0))
    full = lambda a: pl.BlockSpec(a.shape, lambda i: (0,) * a.ndim)
    outs = [512, 256, 256, 512, 512, 512]
    return pl.pallas_call(
        _proj_kernel,
        grid=(N_TILES,),
        in_specs=_x_specs() + [full(mod), full(w_in_r), full(gkw_bd), full(gkb)],
        out_specs=[tok(c) for c in outs],
        out_shape=[jax.ShapeDtypeStruct((N_TOK, c), F32) for c in outs],
        compiler_params=_cparams(("arbitrary",)),
        name="proj",
    )(xp2, xs2, pos, mod, w_in_r, gkw_bd, gkb)


def _gla_constants(rev):
    t = np.arange(TILE)
    same_chunk = (t[:, None] // CHUNK) == (t[None, :] // CHUNK)
    tri = same_chunk & ((t[None, :] >= t[:, None]) if rev else (t[None, :] <= t[:, None]))
    u = np.arange(CHUNK)
    x = u[:, None] ^ u[None, :]
    lvl = np.where(x > 0, np.floor(np.log2(np.maximum(x, 1))), N_LEVELS - 1).astype(np.int32)
    causal = (u[:, None] <= u[None, :]) if rev else (u[:, None] >= u[None, :])
    lvl = np.where(causal, lvl, -1).astype(np.int32)
    return jnp.asarray(tri, BF16), jnp.asarray(np.tile(lvl, (1, HEADS)), jnp.int32)


def _level_ref(b_scr, level, rev):
    half = 1 << level
    sub = lax.broadcasted_iota(jnp.int32, (8, D_K), 0)
    row = lambda r: jnp.broadcast_to(b_scr[r:r + 1, :], (8, D_K))
    pieces = []
    for m in range(TILE // 8):
        refs = [((8 * m + u) // (2 * half)) * (2 * half) + (half if rev else half - 1) for u in range(8)]
        piece = row(refs[0])
        for u in range(1, 8):
            if refs[u] != refs[u - 1]:
                piece = jnp.where(sub >= u, row(refs[u]), piece)
        pieces.append(piece)
    return jnp.concatenate(pieces, axis=0)


def _head_blocks(x, width):
    head = lax.broadcasted_iota(jnp.int32, x.shape, 1) // width
    return jnp.concatenate([jnp.where(head == h, x, jnp.zeros_like(x)) for h in range(HEADS)], axis=0)


def _gla_tile(q, k, v, g, tri, lmap, st_ref, b_scr, rev):
    b = _dot_exact(tri, g)
    b_scr[...] = b
    qds, kds = [], []
    for l in range(N_LEVELS - 1):
        d = b - _level_ref(b_scr, l, rev)
        qds.append((q * jnp.exp(jnp.minimum(d, 0.0))).astype(BF16))
        kds.append((k * jnp.exp(jnp.minimum(-d, 0.0))).astype(BF16))
    qds.append(q.astype(BF16))
    kds.append(k.astype(BF16))
    qe = (q * jnp.exp(b)).astype(BF16)
    vb = v.astype(BF16)

    cdim = (((1,), (1,)), ((), ()))
    st_head = lax.broadcasted_iota(jnp.int32, (DV, D_K), 1) // DK
    st = st_ref[...]
    outs = [None] * (TILE // CHUNK)
    for c in (reversed(range(TILE // CHUNK)) if rev else range(TILE // CHUNK)):
        rows = slice(c * CHUNK, (c + 1) * CHUNK)
        o = lax.dot_general(qe[rows], _head_blocks(st.astype(BF16), DK), cdim,
                            preferred_element_type=F32)
        p = jnp.zeros((CHUNK, HEADS * CHUNK), F32)
        for l in range(N_LEVELS):
            z = lax.dot_general(qds[l][rows], _head_blocks(kds[l][rows], DK), cdim,
                                preferred_element_type=F32)
            p = jnp.where(lmap == l, z, p)
        outs[c] = o + jnp.dot(p.astype(BF16), _head_blocks(vb[rows], DV), preferred_element_type=F32)

        b_c = b[rows]
        b_end = b_c[0:1, :] if rev else b_c[CHUNK - 1:CHUNK, :]
        kdec = (k[rows] * jnp.exp(b_end - b_c)).astype(BF16)
        kv = lax.dot_general(vb[rows], kdec, (((0,), (0,)), ((), ())), preferred_element_type=F32)
        kvd = kv[0:DV]
        for h in range(1, HEADS):
            kvd = jnp.where(st_head == h, kv[h * DV:(h + 1) * DV], kvd)
        st = st * jnp.exp(b_end) + kvd
    st_ref[...] = st
    return jnp.concatenate(outs, axis=0)


def _gla_fwd_kernel(q_ref, k_ref, v_ref, g_ref, st0_ref, tri_ref, lmap_ref,
                    of_ref, stout_ref, st_scr, b_scr):
    i = pl.program_id(0)
    seq_start = (i <= CTX_TILES) | (i == CTX_TILES + LAT_TILES)

    @pl.when(seq_start)
    def _():
        st_scr[...] = st0_ref[0]

    of_ref[...] = _gla_tile(q_ref[...], k_ref[...], v_ref[...], g_ref[...], tri_ref[...],
                            lmap_ref[...], st_scr, b_scr, rev=False)

    @pl.when(i < CTX_TILES)
    def _():
        stout_ref[0] = st_scr[...]


def _gla_fwd_call(q, k, v, gk, st0, consts):
    tri, lmap = consts
    tok = lambda c: pl.BlockSpec((TILE, c), lambda i: (i, 0))
    full = lambda a: pl.BlockSpec(a.shape, lambda i: (0,) * a.ndim)
    return pl.pallas_call(
        _gla_fwd_kernel,
        grid=(N_TILES,),
        in_specs=[tok(D_K), tok(D_K), tok(D_V), pl.BlockSpec((TILE, D_K), lambda i: (i, 0)),
                  pl.BlockSpec((1, DV, D_K), lambda i: (_cond_row(i), 0, 0)),
                  full(tri), full(lmap)],
        out_specs=[tok(D_V),
                   pl.BlockSpec((1, DV, D_K), lambda i: (jnp.minimum(i, CTX_TILES - 1), 0, 0))],
        out_shape=[jax.ShapeDtypeStruct((N_TOK, D_V), F32),
                   jax.ShapeDtypeStruct((N_CTX_SEQ, DV, D_K), F32)],
        scratch_shapes=[pltpu.VMEM((DV, D_K), F32), pltpu.VMEM((TILE, D_K), F32)],
        compiler_params=_cparams(("arbitrary",)),
        name="gla_fwd",
    )(q, k, v, gk, st0, tri, lmap)


def _pool_band():
    t = np.arange(TILE)[:, None]
    e = np.arange(TILE + 2 * POOL_HALO)[None, :]
    bands = []
    for w in POOL_WINDOWS:
        lo = t + POOL_HALO - w // 2
        bands.append(((e >= lo) & (e < lo + w)).astype(np.float32))
    return jnp.asarray(np.stack(bands), BF16)


def _gla_bwd_kernel(q_ref, k_ref, v_ref, g_ref, st0_ref, tri_ref, lmap_ref,
                    of_ref, xpool_ref, prev_ref, next_ref, ogs_ref, xp_ref, xs_ref, pos_ref, mod_ref,
                    band_ref, poolw_ref, pscale_ref, gnorm_ref, wo_ref, ln_g_ref, ln_b_ref,
                    x1_ref, stout_ref, st_scr, b_scr):
    j = N_TILES - 1 - pl.program_id(0)
    lat_idx = jnp.maximum(j - CTX_TILES, 0) % LAT_TILES
    is_ctx = j < CTX_TILES
    seq_first = is_ctx | (lat_idx == 0)
    seq_last = is_ctx | (lat_idx == LAT_TILES - 1)

    @pl.when(seq_last)
    def _():
        st_scr[...] = st0_ref[0]

    o_b = _gla_tile(q_ref[...], k_ref[...], v_ref[...], g_ref[...], tri_ref[...],
                    lmap_ref[...], st_scr, b_scr, rev=True)

    @pl.when(is_ctx)
    def _():
        stout_ref[0] = st_scr[...]

    xpool = xpool_ref[...]
    prev = jnp.where(seq_first, 0.0, prev_ref[...])
    nxt = jnp.where(seq_last, 0.0, next_ref[...])
    ext = jnp.concatenate([prev, xpool, nxt], axis=0)
    tpos = jnp.where(is_ctx, 0, lat_idx) * TILE + lax.broadcasted_iota(jnp.int32, (TILE, POOL_CH), 0)
    seq_len = jnp.where(is_ctx, CTX_LEN, LAT_LEN)
    pool_parts = []
    for gi, w in enumerate(POOL_WINDOWS):
        sl = slice(gi * POOL_CH, (gi + 1) * POOL_CH)
        wsum = _dot_exact(band_ref[gi], ext[:, sl])
        lo = jnp.maximum(tpos - w // 2, 0)
        hi = jnp.minimum(tpos - w // 2 + w, seq_len)
        dmean = wsum / (hi - lo).astype(F32) - xpool[:, sl]
        y = jnp.dot(dmean.astype(BF16), poolw_ref[gi], preferred_element_type=F32)
        pool_parts.append(y * pscale_ref[:, sl])

    o = of_ref[...] + o_b
    ogs = ogs_ref[...]
    for h in range(HEADS):
        sl = slice(h * DV, (h + 1) * DV)
        oh = o[:, sl]
        oh = oh * lax.rsqrt(jnp.mean(oh * oh, axis=-1, keepdims=True) + EPS)
        pool_parts.append(oh * gnorm_ref[:, sl] * ogs[:, sl])
    mix_in = jnp.concatenate(pool_parts, axis=1).astype(BF16)
    mix = jnp.dot(mix_in, wo_ref[...], preferred_element_type=F32)

    r = _cond_row(j)
    x = _load_x(j, xp_ref, xs_ref, pos_ref)
    y = ALPHA * x + _mod_rows(mod_ref, r, 2) * mix
    x1_ref[...] = _layer_norm(y, ln_g_ref[...], ln_b_ref[...])


def _gla_bwd_call(q, k, v, gk, st0, consts, o_f, xpool, ogs, xp2, xs2, pos, mod,
                  band, poolw, pscale, gnorm, wo, ln_g, ln_b):
    tri, lmap = consts
    rv = lambda i: N_TILES - 1 - i
    tok = lambda c: pl.BlockSpec((TILE, c), lambda i: (rv(i), 0))
    full = lambda a: pl.BlockSpec(a.shape, lambda i: (0,) * a.ndim)
    halo_blocks = TILE // POOL_HALO
    n_halo = N_TOK // POOL_HALO
    xspecs = [pl.BlockSpec((TILE, D_MODEL), lambda i: (jnp.minimum(rv(i), CTX_TILES - 1), 0)),
              pl.BlockSpec((TILE, D_MODEL), lambda i: (jnp.maximum(rv(i) - CTX_TILES, 0), 0)),
              pl.BlockSpec((TILE, D_MODEL), lambda i: (jnp.maximum(rv(i) - CTX_TILES, 0) % LAT_TILES, 0))]
    return pl.pallas_call(
        _gla_bwd_kernel,
        grid=(N_TILES,),
        in_specs=[tok(D_K), tok(D_K), tok(D_V), pl.BlockSpec((TILE, D_K), lambda i: (rv(i), 1)),
                  pl.BlockSpec((1, DV, D_K), lambda i: (_cond_row(rv(i)), 0, 0)),
                  full(tri), full(lmap),
                  tok(D_V), tok(D_POOL),
                  pl.BlockSpec((POOL_HALO, D_POOL), lambda i: (jnp.maximum(rv(i) * halo_blocks - 1, 0), 0)),
                  pl.BlockSpec((POOL_HALO, D_POOL),
                               lambda i: (jnp.minimum((rv(i) + 1) * halo_blocks, n_halo - 1), 0)),
                  tok(D_V)] + xspecs + [full(mod), full(band), full(poolw), full(pscale), full(gnorm),
                                        full(wo), full(ln_g), full(ln_b)],
        out_specs=[tok(D_MODEL),
                   pl.BlockSpec((1, DV, D_K), lambda i: (jnp.minimum(rv(i), CTX_TILES - 1), 0, 0))],
        out_shape=[jax.ShapeDtypeStruct((N_TOK, D_MODEL), F32),
                   jax.ShapeDtypeStruct((N_CTX_SEQ, DV, D_K), F32)],
        scratch_shapes=[pltpu.VMEM((DV, D_K), F32), pltpu.VMEM((TILE, D_K), F32)],
        compiler_params=_cparams(("arbitrary",)),
        name="gla_bwd_mix",
    )(q, k, v, gk, st0, tri, lmap, o_f, xpool, xpool, xpool, ogs, xp2, xs2, pos, mod,
      band, poolw, pscale, gnorm, wo, ln_g, ln_b)


def _top_values(x, n, with_rank=False):
    vals = []
    rank = jnp.full(x.shape, float(n), F32)
    for j in range(n):
        m = jnp.max(x, axis=0, keepdims=True)
        vals.append(m)
        hit = x == m
        if with_rank:
            rank = jnp.where(hit, float(j), rank)
        x = jnp.where(hit, -jnp.inf, x)
    return (vals, rank) if with_rank else vals


def _route_kernel(x1_ref, mod_ref, wqt_ref, keys_ref, h2t_ref, cnt_ref, g1_ref, r2_ref, e2_ref):
    i = pl.program_id(0)
    r = _cond_row(i)
    h2 = x1_ref[...] * (1.0 + _mod_rows(mod_ref, r, 4)) + _mod_rows(mod_ref, r, 3)
    h2t = h2.T.astype(BF16)
    h2t_ref[...] = h2t
    qt = jnp.dot(wqt_ref[...], h2t, preferred_element_type=F32)
    for h in range(PEER_HEADS):
        s = []
        for p in range(2):
            row = (2 * h + p) * N_KEYS
            qhp = qt[row:row + N_KEYS, :].astype(BF16)
            s.append(jnp.dot(keys_ref[2 * h + p], qhp, preferred_element_type=F32))
        sv1, rank1 = _top_values(s[0], PEER_TOPK, with_rank=True)
        sv2, rank2 = _top_values(s[1], PEER_TOPK, with_rank=True)
        a1 = jnp.concatenate(sv1, axis=0)
        a2 = jnp.concatenate(sv2, axis=0)
        row8 = lax.broadcasted_iota(jnp.int32, (8, TILE), 0)
        cand = [a2 + sv1[0]]
        for a in range(1, 8):
            cand.append(jnp.where(row8 < PEER_TOPK // (a + 1), a2[0:8] + sv1[a], -jnp.inf))
        cand.append(a1[8:16] + sv2[0])
        fv = _top_values(jnp.concatenate(cand, axis=0), PEER_TOPK)
        denom = jnp.zeros_like(fv[0])
        for f in fv:
            denom = denom + jnp.exp(f - fv[0])
        thr = fv[PEER_TOPK - 1]
        cnt = jnp.zeros_like(rank1)
        for a in range(PEER_TOPK):
            blk = cand[a] if a < 8 else cand[8][a - 8:a - 7]
            n_a = jnp.sum((blk >= thr).astype(F32), axis=0, keepdims=True)
            cnt = jnp.where(rank1 == float(a), n_a, cnt)
        cnt_ref[h] = cnt
        g1_ref[h] = jnp.exp(s[0] - sv1[0]) / denom
        r2_ref[h] = rank2.astype(BF16)
        e2_ref[h] = jnp.exp(s[1] - sv2[0]).astype(BF16)


def _route_call(x1, mod, wqt, keys):
    full = lambda a: pl.BlockSpec(a.shape, lambda i: (0,) * a.ndim)
    hk = pl.BlockSpec((PEER_HEADS, N_KEYS, TILE), lambda i: (0, 0, i))
    per_key = lambda dt: jax.ShapeDtypeStruct((PEER_HEADS, N_KEYS, N_TOK), dt)
    return pl.pallas_call(
        _route_kernel,
        grid=(N_TILES,),
        in_specs=[pl.BlockSpec((TILE, D_MODEL), lambda i: (i, 0)), full(mod), full(wqt), full(keys)],
        out_specs=[pl.BlockSpec((D_MODEL, TILE), lambda i: (0, i)), hk, hk, hk, hk],
        out_shape=[jax.ShapeDtypeStruct((D_MODEL, N_TOK), BF16),
                   per_key(F32), per_key(F32), per_key(BF16), per_key(BF16)],
        compiler_params=_cparams(("arbitrary",)),
        name="peer_route",
    )(x1, mod, wqt, keys)


def _gelu_tanh(x):
    c1 = math.sqrt(2.0 / math.pi)
    c2 = c1 * 0.044715
    return x * (0.5 + 0.5 * jnp.tanh(x * (c1 + c2 * (x * x))))


def _dense_kernel(h2t_ref, cnt_ref, g1_ref, r2_ref, e2_ref, u_ref, vt_ref, x1_ref, mod_ref,
                  ln_g_ref, ln_b_ref, y_ref, acc_ref):
    tt = pl.program_id(0)
    c = pl.program_id(1)
    n_sub = DENSE_CH // N_KEYS

    @pl.when(c == 0)
    def _():
        acc_ref[...] = jnp.zeros_like(acc_ref)

    a = jnp.dot(u_ref[...], h2t_ref[...], preferred_element_type=F32)
    i1_base = pl.multiple_of(c * n_sub, n_sub)
    cnt8 = [cnt_ref[h, pl.ds(i1_base, n_sub), :].astype(BF16) for h in range(PEER_HEADS)]
    g18 = [g1_ref[h, pl.ds(i1_base, n_sub), :].astype(BF16) for h in range(PEER_HEADS)]
    w_parts = []
    for ci in range(n_sub):
        act = _gelu_tanh(a[ci * N_KEYS:(ci + 1) * N_KEYS, :].astype(BF16))
        wsum = jnp.zeros((N_KEYS, DENSE_TM), BF16)
        for h in range(PEER_HEADS):
            sel = r2_ref[h] < cnt8[h][ci:ci + 1, :]
            wsum = wsum + jnp.where(sel, e2_ref[h], jnp.zeros((), BF16)) * g18[h][ci:ci + 1, :]
        w_parts.append(act * wsum)
    w = jnp.concatenate(w_parts, axis=0)
    acc_ref[...] += jnp.dot(vt_ref[0], w, preferred_element_type=F32)

    @pl.when(c == pl.num_programs(1) - 1)
    def _():
        r = _cond_row(tt * (DENSE_TM // TILE))
        ffn = acc_ref[...].T
        y = ALPHA * x1_ref[...] + _mod_rows(mod_ref, r, 5) * ffn
        y_ref[...] = _layer_norm(y, ln_g_ref[...], ln_b_ref[...])


def _dense_call(h2t, cnt, g1, r2, e2, u_bf, vt_bf, x1, mod, ln_g, ln_b):
    full = lambda a: pl.BlockSpec(a.shape, lambda t, c: (0,) * a.ndim)
    hk = pl.BlockSpec((PEER_HEADS, N_KEYS, DENSE_TM), lambda t, c: (0, 0, t))
    return pl.pallas_call(
        _dense_kernel,
        grid=(N_TOK // DENSE_TM, N_EXPERTS // DENSE_CH),
        in_specs=[pl.BlockSpec((D_MODEL, DENSE_TM), lambda t, c: (0, t)),
                  hk, hk, hk, hk,
                  pl.BlockSpec((DENSE_CH, D_MODEL), lambda t, c: (c, 0)),
                  pl.BlockSpec((1, D_MODEL, DENSE_CH), lambda t, c: (c, 0, 0)),
                  pl.BlockSpec((DENSE_TM, D_MODEL), lambda t, c: (t, 0)),
                  full(mod), full(ln_g), full(ln_b)],
        out_specs=pl.BlockSpec((DENSE_TM, D_MODEL), lambda t, c: (t, 0)),
        out_shape=jax.ShapeDtypeStruct((N_TOK, D_MODEL), F32),
        scratch_shapes=[pltpu.VMEM((D_MODEL, DENSE_TM), F32)],
        compiler_params=_cparams(("arbitrary", "arbitrary")),
        name="peer_dense",
    )(h2t, cnt, g1, r2, e2, u_bf, vt_bf, x1, mod, ln_g, ln_b)


def _grid_pos_embed():
    rows = LAT_LEN // GRID_W
    r, col = np.meshgrid(np.arange(rows), np.arange(GRID_W), indexing="ij")

    def sincos(pos, dim):
        omega = 1.0 / (10000.0 ** (np.arange(dim // 2, dtype=np.float64) / (dim // 2)))
        ang = pos.reshape(-1).astype(np.float64)[:, None] * omega[None, :]
        return np.concatenate([np.sin(ang), np.cos(ang)], axis=-1)

    pe = np.concatenate([sincos(r, D_MODEL // 2), sincos(col, D_MODEL // 2)], axis=-1)
    return jnp.asarray(pe, F32)


def _state_pack_t(state):
    packed = jnp.transpose(state, (0, 3, 1, 2)).reshape(N_LAT_SEQ, DV, D_K)
    return jnp.concatenate([jnp.zeros((1, DV, D_K), F32), packed], axis=0)


def _state_unpack_t(st):
    return jnp.transpose(st.reshape(N_CTX_SEQ, DV, HEADS, DK), (0, 2, 3, 1))[:, None]


def kernel(x_prompt, x_sample, c, state_fwd, state_bwd, c_ctx, w_ada, b_ada, w_in, pool_w, pool_scale,
           gk_w, gk_b, gla_norm_g, w_o, ln1_g, ln1_b, peer_wq, peer_keys, peer_u, peer_v, ln2_g, ln2_b):
    xp2 = x_prompt.reshape(N_CTX_SEQ * CTX_LEN, D_MODEL)
    xs2 = x_sample.reshape(N_LAT_SEQ * LAT_LEN, D_MODEL)
    pos = _grid_pos_embed()

    cond8 = jnp.zeros((8, D_MODEL), F32).at[0].set(c_ctx).at[1:3].set(c)
    mod = _mod_call(cond8, w_ada[0], b_ada[0][None, :])

    w = w_in[0]
    w_in_r = jnp.concatenate([w[:, :1536], w[:, 1568:], w[:, 1536:1568],
                              jnp.zeros((D_MODEL, P_COLS - 2080), F32)], axis=1).astype(BF16)
    gkw_bd = jnp.zeros((128, 2 * D_K), F32)
    gkw_bd = gkw_bd.at[0:GATE_RANK, 0:D_K].set(gk_w[0, 0]).at[GATE_RANK:2 * GATE_RANK, D_K:].set(gk_w[0, 1])
    gkb = gk_b[0].reshape(1, 2 * D_K)
    xpool, q, k, v, gk, ogs = _proj_call(xp2, xs2, pos, mod, w_in_r, gkw_bd.astype(BF16), gkb)

    o_f, st_f = _gla_fwd_call(q, k, v, gk, _state_pack_t(state_fwd[:, 0]), _gla_constants(False))
    x1, st_b = _gla_bwd_call(
        q, k, v, gk, _state_pack_t(state_bwd[:, 0]), _gla_constants(True), o_f, xpool, ogs,
        xp2, xs2, pos, mod, _pool_band(), pool_w[0].astype(BF16), pool_scale[0][None, :],
        gla_norm_g[0][None, :], w_o[0].astype(BF16), ln1_g[0][None, :], ln1_b[0][None, :])

    wqt = peer_wq[0].T.astype(BF16)
    keys = peer_keys[0].reshape(2 * PEER_HEADS, N_KEYS, N_KEYS).astype(BF16)
    h2t, cnt, g1, r2, e2 = _route_call(x1, mod, wqt, keys)

    u_bf = peer_u[0].astype(BF16)
    n_steps = N_EXPERTS // DENSE_CH
    vt_bf = jnp.swapaxes(peer_v[0].astype(BF16).reshape(n_steps, DENSE_CH, D_MODEL), 1, 2)
    y = _dense_call(h2t, cnt, g1, r2, e2, u_bf, vt_bf, x1, mod, ln2_g[0][None, :], ln2_b[0][None, :])

    n_ctx = N_CTX_SEQ * CTX_LEN
    y_prompt = y[:n_ctx].reshape(N_CTX_SEQ, CTX_LEN, D_MODEL)
    y_sample = y[n_ctx:].reshape(N_LAT_SEQ, LAT_LEN, D_MODEL)
    return (y_prompt, y_sample, _state_unpack_t(st_f), _state_unpack_t(st_b))
```

```python
import functools
import math

import numpy as np
import jax
import jax.numpy as jnp
from jax import lax
from jax.experimental import pallas as pl
from jax.experimental.pallas import tpu as pltpu

F32 = jnp.float32
BF16 = jnp.bfloat16

D_MODEL = 1024
N_CTX_SEQ, CTX_LEN = 16, 256
N_LAT_SEQ, LAT_LEN = 2, 2048
GRID_W = 64
N_TOK = N_CTX_SEQ * CTX_LEN + N_LAT_SEQ * LAT_LEN
TILE = 256
N_TILES = N_TOK // TILE
CTX_TILES = N_CTX_SEQ * CTX_LEN // TILE
LAT_TILES = LAT_LEN // TILE

D_POOL = 512
POOL_GROUPS = 4
POOL_CH = 128
POOL_WINDOWS = (2, 4, 8, 16)
POOL_HALO = 8
HEADS = 4
DK = 64
DV = 128
D_K = HEADS * DK
D_V = HEADS * DV
GATE_RANK = 16
GATE_NORM = 16.0
CHUNK = 64
N_LEVELS = 7
P_COLS = 2048 + 128

PEER_HEADS = 8
N_KEYS = 128
PEER_TOPK = 16
N_EXPERTS = N_KEYS * N_KEYS
DENSE_TM = 512
DENSE_CH = 1024
DENSE_PIECES = 8
ALPHA = 2.0 ** 0.25
EPS = 1e-5
VMEM_LIMIT = 56 * 1024 * 1024


def _cparams(sem, flags=None):
    return pltpu.CompilerParams(dimension_semantics=sem, vmem_limit_bytes=VMEM_LIMIT, flags=flags)


def _silu(x):
    return x * jax.nn.sigmoid(x)


def _split3(x):
    x1 = x.astype(BF16)
    r1 = x - x1.astype(F32)
    x2 = r1.astype(BF16)
    r2 = r1 - x2.astype(F32)
    x3 = r2.astype(BF16)
    return x1, x2, x3


def _dot_exact(m01, x):
    c = x.shape[1]
    xs = jnp.concatenate(_split3(x), axis=1)
    y = jnp.dot(m01, xs, preferred_element_type=F32)
    return y[:, :c] + y[:, c:2 * c] + y[:, 2 * c:]


def _layer_norm(y, g, b):
    mu = jnp.mean(y, axis=-1, keepdims=True)
    yc = y - mu
    var = jnp.mean(yc * yc, axis=-1, keepdims=True)
    return yc * lax.rsqrt(var + EPS) * g + b


def _cond_row(tile):
    return (tile >= CTX_TILES).astype(jnp.int32) + (tile >= CTX_TILES + LAT_TILES).astype(jnp.int32)


def _mod_kernel(cond_ref, w_ref, b_ref, out_ref):
    s = _silu(cond_ref[...]).astype(BF16)
    out_ref[...] = jnp.dot(s, w_ref[...].astype(BF16), preferred_element_type=F32) + b_ref[...]


def _mod_call(cond8, w_ada, b_ada):
    n = w_ada.shape[1]
    bn = 1536
    return pl.pallas_call(
        _mod_kernel,
        grid=(n // bn,),
        in_specs=[pl.BlockSpec((8, D_MODEL), lambda j: (0, 0)),
                  pl.BlockSpec((D_MODEL, bn), lambda j: (0, j)),
                  pl.BlockSpec((1, bn), lambda j: (0, j))],
        out_specs=pl.BlockSpec((8, bn), lambda j: (0, j)),
        out_shape=jax.ShapeDtypeStruct((8, n), F32),
        compiler_params=_cparams(("arbitrary",)),
        name="mod",
    )(cond8, w_ada, b_ada)


def _load_x(i, xp_ref, xs_ref, pos_ref):
    return jnp.where(i < CTX_TILES, xp_ref[...], xs_ref[...] + pos_ref[...])


def _mod_rows(mod_ref, r, k):
    return mod_ref[pl.ds(r, 1), k * D_MODEL:(k + 1) * D_MODEL]


def _proj_kernel(xp_ref, xs_ref, pos_ref, mod_ref, w_ref, gkw_ref, gkb_ref,
                 oxp, oq, ok, ov, ogk, oog):
    i = pl.program_id(0)
    r = _cond_row(i)
    x = _load_x(i, xp_ref, xs_ref, pos_ref)
    h = x * (1.0 + _mod_rows(mod_ref, r, 1)) + _mod_rows(mod_ref, r, 0)
    p = jnp.dot(h.astype(BF16), w_ref[...], preferred_element_type=F32)
    oxp[...] = p[:, 0:512]
    oq[...] = p[:, 512:768] * (DK ** -0.5)
    ok[...] = p[:, 768:1024]
    ov[...] = p[:, 1024:1536]
    oog[...] = _silu(p[:, 1536:2048])
    glr = p[:, 2048:2176]
    g_hi = glr.astype(BF16)
    g_lo = (glr - g_hi.astype(F32)).astype(BF16)
    w = gkw_ref[...]
    pre = (jnp.dot(g_hi, w, preferred_element_type=F32)
           + jnp.dot(g_lo, w, preferred_element_type=F32) + gkb_ref[...])
    ogk[...] = (jnp.minimum(pre, 0.0) - jnp.log(1.0 + jnp.exp(-jnp.abs(pre)))) * (1.0 / GATE_NORM)


def _x_specs():
    return [pl.BlockSpec((TILE, D_MODEL), lambda i: (jnp.minimum(i, CTX_TILES - 1), 0)),
            pl.BlockSpec((TILE, D_MODEL), lambda i: (jnp.maximum(i - CTX_TILES, 0), 0)),
            pl.BlockSpec((TILE, D_MODEL), lambda i: (jnp.maximum(i - CTX_TILES, 0) % LAT_TILES, 0))]


def _proj_call(xp2, xs2, pos, mod, w_in_r, gkw_bd, gkb):
    tok = lambda c: pl.BlockSpec((TILE, c), lambda i: (i, 0))
    full = lambda a: pl.BlockSpec(a.shape, lambda i: (0,) * a.ndim)
    outs = [512, 256, 256, 512, 512, 512]
    return pl.pallas_call(
        _proj_kernel,
        grid=(N_TILES,),
        in_specs=_x_specs() + [full(mod), full(w_in_r), full(gkw_bd), full(gkb)],
        out_specs=[tok(c) for c in outs],
        out_shape=[jax.ShapeDtypeStruct((N_TOK, c), F32) for c in outs],
        compiler_params=_cparams(("arbitrary",)),
        name="proj",
    )(xp2, xs2, pos, mod, w_in_r, gkw_bd, gkb)


def _gla_constants(rev):
    t = np.arange(TILE)
    same_chunk = (t[:, None] // CHUNK) == (t[None, :] // CHUNK)
    tri = same_chunk & ((t[None, :] >= t[:, None]) if rev else (t[None, :] <= t[:, None]))
    u = np.arange(CHUNK)
    x = u[:, None] ^ u[None, :]
    lvl = np.where(x > 0, np.floor(np.log2(np.maximum(x, 1))), N_LEVELS - 1).astype(np.int32)
    causal = (u[:, None] <= u[None, :]) if rev else (u[:, None] >= u[None, :])
    lvl = np.where(causal, lvl, -1).astype(np.int32)
    return jnp.asarray(tri, BF16), jnp.asarray(np.tile(lvl, (1, HEADS)), jnp.int32)


def _level_ref(b_scr, level, rev):
    half = 1 << level
    sub = lax.broadcasted_iota(jnp.int32, (8, D_K), 0)
    row = lambda r: jnp.broadcast_to(b_scr[r:r + 1, :], (8, D_K))
    pieces = []
    for m in range(TILE // 8):
        refs = [((8 * m + u) // (2 * half)) * (2 * half) + (half if rev else half - 1) for u in range(8)]
        piece = row(refs[0])
        for u in range(1, 8):
            if refs[u] != refs[u - 1]:
                piece = jnp.where(sub >= u, row(refs[u]), piece)
        pieces.append(piece)
    return jnp.concatenate(pieces, axis=0)


def _head_blocks(x, width):
    head = lax.broadcasted_iota(jnp.int32, x.shape, 1) // width
    return jnp.concatenate([jnp.where(head == h, x, jnp.zeros_like(x)) for h in range(HEADS)], axis=0)


def _gla_tile(q, k, v, g, tri, lmap, st_ref, b_scr, rev):
    b = _dot_exact(tri, g)
    b_scr[...] = b
    qds, kds = [], []
    for l in range(N_LEVELS - 1):
        d = b - _level_ref(b_scr, l, rev)
        qds.append((q * jnp.exp(jnp.minimum(d, 0.0))).astype(BF16))
        kds.append((k * jnp.exp(jnp.minimum(-d, 0.0))).astype(BF16))
    qds.append(q.astype(BF16))
    kds.append(k.astype(BF16))
    qe = (q * jnp.exp(b)).astype(BF16)
    vb = v.astype(BF16)

    cdim = (((1,), (1,)), ((), ()))
    st_head = lax.broadcasted_iota(jnp.int32, (DV, D_K), 1) // DK
    st = st_ref[...]
    outs = [None] * (TILE // CHUNK)
    for c in (reversed(range(TILE // CHUNK)) if rev else range(TILE // CHUNK)):
        rows = slice(c * CHUNK, (c + 1) * CHUNK)
        o = lax.dot_general(qe[rows], _head_blocks(st.astype(BF16), DK), cdim,
                            preferred_element_type=F32)
        p = jnp.zeros((CHUNK, HEADS * CHUNK), F32)
        for l in range(N_LEVELS):
            z = lax.dot_general(qds[l][rows], _head_blocks(kds[l][rows], DK), cdim,
                                preferred_element_type=F32)
            p = jnp.where(lmap == l, z, p)
        outs[c] = o + jnp.dot(p.astype(BF16), _head_blocks(vb[rows], DV), preferred_element_type=F32)

        b_c = b[rows]
        b_end = b_c[0:1, :] if rev else b_c[CHUNK - 1:CHUNK, :]
        kdec = (k[rows] * jnp.exp(b_end - b_c)).astype(BF16)
        kv = lax.dot_general(vb[rows], kdec, (((0,), (0,)), ((), ())), preferred_element_type=F32)
        kvd = kv[0:DV]
        for h in range(1, HEADS):
            kvd = jnp.where(st_head == h, kv[h * DV:(h + 1) * DV], kvd)
        st = st * jnp.exp(b_end) + kvd
    st_ref[...] = st
    return jnp.concatenate(outs, axis=0)


def _gla_fwd_kernel(q_ref, k_ref, v_ref, g_ref, st0_ref, tri_ref, lmap_ref,
                    of_ref, stout_ref, st_scr, b_scr):
    i = pl.program_id(0)
    seq_start = (i <= CTX_TILES) | (i == CTX_TILES + LAT_TILES)

    @pl.when(seq_start)
    def _():
        st_scr[...] = st0_ref[0]

    of_ref[...] = _gla_tile(q_ref[...], k_ref[...], v_ref[...], g_ref[...], tri_ref[...],
                            lmap_ref[...], st_scr, b_scr, rev=False)

    @pl.when(i < CTX_TILES)
    def _():
        stout_ref[0] = st_scr[...]


def _gla_fwd_call(q, k, v, gk, st0, consts):
    tri, lmap = consts
    tok = lambda c: pl.BlockSpec((TILE, c), lambda i: (i, 0))
    full = lambda a: pl.BlockSpec(a.shape, lambda i: (0,) * a.ndim)
    return pl.pallas_call(
        _gla_fwd_kernel,
        grid=(N_TILES,),
        in_specs=[tok(D_K), tok(D_K), tok(D_V), pl.BlockSpec((TILE, D_K), lambda i: (i, 0)),
                  pl.BlockSpec((1, DV, D_K), lambda i: (_cond_row(i), 0, 0)),
                  full(tri), full(lmap)],
        out_specs=[tok(D_V),
                   pl.BlockSpec((1, DV, D_K), lambda i: (jnp.minimum(i, CTX_TILES - 1), 0, 0))],
        out_shape=[jax.ShapeDtypeStruct((N_TOK, D_V), F32),
                   jax.ShapeDtypeStruct((N_CTX_SEQ, DV, D_K), F32)],
        scratch_shapes=[pltpu.VMEM((DV, D_K), F32), pltpu.VMEM((TILE, D_K), F32)],
        compiler_params=_cparams(("arbitrary",)),
        name="gla_fwd",
    )(q, k, v, gk, st0, tri, lmap)


def _pool_band():
    t = np.arange(TILE)[:, None]
    e = np.arange(TILE + 2 * POOL_HALO)[None, :]
    bands = []
    for w in POOL_WINDOWS:
        lo = t + POOL_HALO - w // 2
        bands.append(((e >= lo) & (e < lo + w)).astype(np.float32))
    return jnp.asarray(np.stack(bands), BF16)


def _gla_bwd_kernel(q_ref, k_ref, v_ref, g_ref, st0_ref, tri_ref, lmap_ref,
                    of_ref, xpool_ref, prev_ref, next_ref, ogs_ref, xp_ref, xs_ref, pos_ref, mod_ref,
                    band_ref, poolw_ref, pscale_ref, gnorm_ref, wo_ref, ln_g_ref, ln_b_ref,
                    x1_ref, stout_ref, st_scr, b_scr):
    j = N_TILES - 1 - pl.program_id(0)
    lat_idx = jnp.maximum(j - CTX_TILES, 0) % LAT_TILES
    is_ctx = j < CTX_TILES
    seq_first = is_ctx | (lat_idx == 0)
    seq_last = is_ctx | (lat_idx == LAT_TILES - 1)

    @pl.when(seq_last)
    def _():
        st_scr[...] = st0_ref[0]

    o_b = _gla_tile(q_ref[...], k_ref[...], v_ref[...], g_ref[...], tri_ref[...],
                    lmap_ref[...], st_scr, b_scr, rev=True)

    @pl.when(is_ctx)
    def _():
        stout_ref[0] = st_scr[...]

    xpool = xpool_ref[...]
    prev = jnp.where(seq_first, 0.0, prev_ref[...])
    nxt = jnp.where(seq_last, 0.0, next_ref[...])
    ext = jnp.concatenate([prev, xpool, nxt], axis=0)
    tpos = jnp.where(is_ctx, 0, lat_idx) * TILE + lax.broadcasted_iota(jnp.int32, (TILE, POOL_CH), 0)
    seq_len = jnp.where(is_ctx, CTX_LEN, LAT_LEN)
    pool_parts = []
    for gi, w in enumerate(POOL_WINDOWS):
        sl = slice(gi * POOL_CH, (gi + 1) * POOL_CH)
        wsum = _dot_exact(band_ref[gi], ext[:, sl])
        lo = jnp.maximum(tpos - w // 2, 0)
        hi = jnp.minimum(tpos - w // 2 + w, seq_len)
        dmean = wsum / (hi - lo).astype(F32) - xpool[:, sl]
        y = jnp.dot(dmean.astype(BF16), poolw_ref[gi], preferred_element_type=F32)
        pool_parts.append(y * pscale_ref[:, sl])

    o = of_ref[...] + o_b
    ogs = ogs_ref[...]
    for h in range(HEADS):
        sl = slice(h * DV, (h + 1) * DV)
        oh = o[:, sl]
        oh = oh * lax.rsqrt(jnp.mean(oh * oh, axis=-1, keepdims=True) + EPS)
        pool_parts.append(oh * gnorm_ref[:, sl] * ogs[:, sl])
    mix_in = jnp.concatenate(pool_parts, axis=1).astype(BF16)
    mix = jnp.dot(mix_in, wo_ref[...], preferred_element_type=F32)

    r = _cond_row(j)
    x = _load_x(j, xp_ref, xs_ref, pos_ref)
    y = ALPHA * x + _mod_rows(mod_ref, r, 2) * mix
    x1_ref[...] = _layer_norm(y, ln_g_ref[...], ln_b_ref[...])


def _gla_bwd_call(q, k, v, gk, st0, consts, o_f, xpool, ogs, xp2, xs2, pos, mod,
                  band, poolw, pscale, gnorm, wo, ln_g, ln_b):
    tri, lmap = consts
    rv = lambda i: N_TILES - 1 - i
    tok = lambda c: pl.BlockSpec((TILE, c), lambda i: (rv(i), 0))
    full = lambda a: pl.BlockSpec(a.shape, lambda i: (0,) * a.ndim)
    halo_blocks = TILE // POOL_HALO
    n_halo = N_TOK // POOL_HALO
    xspecs = [pl.BlockSpec((TILE, D_MODEL), lambda i: (jnp.minimum(rv(i), CTX_TILES - 1), 0)),
              pl.BlockSpec((TILE, D_MODEL), lambda i: (jnp.maximum(rv(i) - CTX_TILES, 0), 0)),
              pl.BlockSpec((TILE, D_MODEL), lambda i: (jnp.maximum(rv(i) - CTX_TILES, 0) % LAT_TILES, 0))]
    return pl.pallas_call(
        _gla_bwd_kernel,
        grid=(N_TILES,),
        in_specs=[tok(D_K), tok(D_K), tok(D_V), pl.BlockSpec((TILE, D_K), lambda i: (rv(i), 1)),
                  pl.BlockSpec((1, DV, D_K), lambda i: (_cond_row(rv(i)), 0, 0)),
                  full(tri), full(lmap),
                  tok(D_V), tok(D_POOL),
                  pl.BlockSpec((POOL_HALO, D_POOL), lambda i: (jnp.maximum(rv(i) * halo_blocks - 1, 0), 0)),
                  pl.BlockSpec((POOL_HALO, D_POOL),
                               lambda i: (jnp.minimum((rv(i) + 1) * halo_blocks, n_halo - 1), 0)),
                  tok(D_V)] + xspecs + [full(mod), full(band), full(poolw), full(pscale), full(gnorm),
                                        full(wo), full(ln_g), full(ln_b)],
        out_specs=[tok(D_MODEL),
                   pl.BlockSpec((1, DV, D_K), lambda i: (jnp.minimum(rv(i), CTX_TILES - 1), 0, 0))],
        out_shape=[jax.ShapeDtypeStruct((N_TOK, D_MODEL), F32),
                   jax.ShapeDtypeStruct((N_CTX_SEQ, DV, D_K), F32)],
        scratch_shapes=[pltpu.VMEM((DV, D_K), F32), pltpu.VMEM((TILE, D_K), F32)],
        compiler_params=_cparams(("arbitrary",)),
        name="gla_bwd_mix",
    )(q, k, v, gk, st0, tri, lmap, o_f, xpool, xpool, xpool, ogs, xp2, xs2, pos, mod,
      band, poolw, pscale, gnorm, wo, ln_g, ln_b)


def _top_values(x, n, with_rank=False):
    vals = []
    rank = jnp.full(x.shape, float(n), F32)
    for j in range(n):
        m = jnp.max(x, axis=0, keepdims=True)
        vals.append(m)
        hit = x == m
        if with_rank:
            rank = jnp.where(hit, float(j), rank)
        x = jnp.where(hit, -jnp.inf, x)
    return (vals, rank) if with_rank else vals


def _route_kernel(x1_ref, mod_ref, wqt_ref, keys_ref, h2t_ref, cnt_ref, g1_ref, r2_ref, e2_ref):
    i = pl.program_id(0)
    r = _cond_row(i)
    h2 = x1_ref[...] * (1.0 + _mod_rows(mod_ref, r, 4)) + _mod_rows(mod_ref, r, 3)
    h2t = h2.T.astype(BF16)
    h2t_ref[...] = h2t
    qt = jnp.dot(wqt_ref[...], h2t, preferred_element_type=F32)
    for h in range(PEER_HEADS):
        s = []
        for p in range(2):
            row = (2 * h + p) * N_KEYS
            qhp = qt[row:row + N_KEYS, :].astype(BF16)
            s.append(jnp.dot(keys_ref[2 * h + p], qhp, preferred_element_type=F32))
        sv1, rank1 = _top_values(s[0], PEER_TOPK, with_rank=True)
        sv2, rank2 = _top_values(s[1], PEER_TOPK, with_rank=True)
        a1 = jnp.concatenate(sv1, axis=0)
        a2 = jnp.concatenate(sv2, axis=0)
        row8 = lax.broadcasted_iota(jnp.int32, (8, TILE), 0)
        cand = [a2 + sv1[0]]
        for a in range(1, 8):
            cand.append(jnp.where(row8 < PEER_TOPK // (a + 1), a2[0:8] + sv1[a], -jnp.inf))
        cand.append(a1[8:16] + sv2[0])
        fv = _top_values(jnp.concatenate(cand, axis=0), PEER_TOPK)
        denom = jnp.zeros_like(fv[0])
        for f in fv:
            denom = denom + jnp.exp(f - fv[0])
        thr = fv[PEER_TOPK - 1]
        cnt = jnp.zeros_like(rank1)
        for a in range(PEER_TOPK):
            blk = cand[a] if a < 8 else cand[8][a - 8:a - 7]
            n_a = jnp.sum((blk >= thr).astype(F32), axis=0, keepdims=True)
            cnt = jnp.where(rank1 == float(a), n_a, cnt)
        cnt_ref[h] = cnt
        g1_ref[h] = jnp.exp(s[0] - sv1[0]) / denom
        r2_ref[h] = rank2.astype(BF16)
        e2_ref[h] = jnp.exp(s[1] - sv2[0]).astype(BF16)


def _route_call(x1, mod, wqt, keys):
    full = lambda a: pl.BlockSpec(a.shape, lambda i: (0,) * a.ndim)
    hk = pl.BlockSpec((PEER_HEADS, N_KEYS, TILE), lambda i: (0, 0, i))
    per_key = lambda dt: jax.ShapeDtypeStruct((PEER_HEADS, N_KEYS, N_TOK), dt)
    return pl.pallas_call(
        _route_kernel,
        grid=(N_TILES,),
        in_specs=[pl.BlockSpec((TILE, D_MODEL), lambda i: (i, 0)), full(mod), full(wqt), full(keys)],
        out_specs=[pl.BlockSpec((D_MODEL, TILE), lambda i: (0, i)), hk, hk, hk, hk],
        out_shape=[jax.ShapeDtypeStruct((D_MODEL, N_TOK), BF16),
                   per_key(F32), per_key(F32), per_key(BF16), per_key(BF16)],
        compiler_params=_cparams(("arbitrary",)),
        name="peer_route",
    )(x1, mod, wqt, keys)


def _gelu_tanh(x):
    c1 = math.sqrt(2.0 / math.pi)
    c2 = c1 * 0.044715
    return x * (0.5 + 0.5 * jnp.tanh(x * (c1 + c2 * (x * x))))


def _dense_kernel(h2t_ref, cnt_ref, g1_ref, r2_ref, e2_ref, u_ref, vt_ref, x1_ref, mod_ref,
                  ln_g_ref, ln_b_ref, y_ref, acc_ref):
    tt = pl.program_id(0)
    c = pl.program_id(1)
    n_sub = DENSE_CH // N_KEYS

    @pl.when(c == 0)
    def _():
        acc_ref[...] = jnp.zeros_like(acc_ref)

    i1_base = pl.multiple_of(c * n_sub, n_sub)
    cnt8 = [cnt_ref[h, pl.ds(i1_base, n_sub), :].astype(BF16) for h in range(PEER_HEADS)]
    g18 = [g1_ref[h, pl.ds(i1_base, n_sub), :].astype(BF16) for h in range(PEER_HEADS)]
    def gate(ci):
        wsum = jnp.zeros((N_KEYS, DENSE_TM), BF16)
        for h in range(PEER_HEADS):
            sel = r2_ref[h] < cnt8[h][ci:ci + 1, :]
            wsum = wsum + jnp.where(sel, e2_ref[h], jnp.zeros((), BF16)) * g18[h][ci:ci + 1, :]
        return wsum

    sub_per_piece = n_sub // DENSE_PIECES
    gates, a_pieces = [], []
    for pc in range(DENSE_PIECES):
        rows = slice(pc * sub_per_piece * N_KEYS, (pc + 1) * sub_per_piece * N_KEYS)
        lhs = u_ref[rows, :]
        if pc > 0:
            zero = gates[-1][0:16, 0:128] * jnp.zeros((), BF16)
            top = jnp.concatenate([lhs[0:16, 0:128] + zero, lhs[0:16, 128:]], axis=1)
            lhs = jnp.concatenate([top, lhs[16:]], axis=0)
        a_pieces.append(jnp.dot(lhs, h2t_ref[...], preferred_element_type=F32))
        gates.extend(gate(ci) for ci in range(pc * sub_per_piece, (pc + 1) * sub_per_piece))
    a = jnp.concatenate(a_pieces, axis=0)
    w_parts = [_gelu_tanh(a[ci * N_KEYS:(ci + 1) * N_KEYS, :].astype(BF16)) * gates[ci] for ci in range(n_sub)]
    zero = gates[-1][0:16, 0:128] * jnp.zeros((), BF16)
    w0 = w_parts[0]
    top = jnp.concatenate([w0[0:16, 0:128] + zero, w0[0:16, 128:]], axis=1)
    w_parts[0] = jnp.concatenate([top, w0[16:]], axis=0)
    w = jnp.concatenate(w_parts, axis=0)
    acc_ref[...] += jnp.dot(vt_ref[0], w, preferred_element_type=F32)

    @pl.when(c == pl.num_programs(1) - 1)
    def _():
        r = _cond_row(tt * (DENSE_TM // TILE))
        ffn = acc_ref[...].T
        y = ALPHA * x1_ref[...] + _mod_rows(mod_ref, r, 5) * ffn
        y_ref[...] = _layer_norm(y, ln_g_ref[...], ln_b_ref[...])


def _dense_call(h2t, cnt, g1, r2, e2, u_bf, vt_bf, x1, mod, ln_g, ln_b):
    full = lambda a: pl.BlockSpec(a.shape, lambda t, c: (0,) * a.ndim)
    hk = pl.BlockSpec((PEER_HEADS, N_KEYS, DENSE_TM), lambda t, c: (0, 0, t))
    return pl.pallas_call(
        _dense_kernel,
        grid=(N_TOK // DENSE_TM, N_EXPERTS // DENSE_CH),
        in_specs=[pl.BlockSpec((D_MODEL, DENSE_TM), lambda t, c: (0, t)),
                  hk, hk, hk, hk,
                  pl.BlockSpec((DENSE_CH, D_MODEL), lambda t, c: (c, 0)),
                  pl.BlockSpec((1, D_MODEL, DENSE_CH), lambda t, c: (c, 0, 0)),
                  pl.BlockSpec((DENSE_TM, D_MODEL), lambda t, c: (t, 0)),
                  full(mod), full(ln_g), full(ln_b)],
        out_specs=pl.BlockSpec((DENSE_TM, D_MODEL), lambda t, c: (t, 0)),
        out_shape=jax.ShapeDtypeStruct((N_TOK, D_MODEL), F32),
        scratch_shapes=[pltpu.VMEM((D_MODEL, DENSE_TM), F32)],
        compiler_params=_cparams(("arbitrary", "arbitrary")),
        name="peer_dense",
    )(h2t, cnt, g1, r2, e2, u_bf, vt_bf, x1, mod, ln_g, ln_b)


def _grid_pos_embed():
    rows = LAT_LEN // GRID_W
    r, col = np.meshgrid(np.arange(rows), np.arange(GRID_W), indexing="ij")

    def sincos(pos, dim):
        omega = 1.0 / (10000.0 ** (np.arange(dim // 2, dtype=np.float64) / (dim // 2)))
        ang = pos.reshape(-1).astype(np.float64)[:, None] * omega[None, :]
        return np.concatenate([np.sin(ang), np.cos(ang)], axis=-1)

    pe = np.concatenate([sincos(r, D_MODEL // 2), sincos(col, D_MODEL // 2)], axis=-1)
    return jnp.asarray(pe, F32)


def _state_pack_t(state):
    packed = jnp.transpose(state, (0, 3, 1, 2)).reshape(N_LAT_SEQ, DV, D_K)
    return jnp.concatenate([jnp.zeros((1, DV, D_K), F32), packed], axis=0)


def _state_unpack_t(st):
    return jnp.transpose(st.reshape(N_CTX_SEQ, DV, HEADS, DK), (0, 2, 3, 1))[:, None]


def kernel(x_prompt, x_sample, c, state_fwd, state_bwd, c_ctx, w_ada, b_ada, w_in, pool_w, pool_scale,
           gk_w, gk_b, gla_norm_g, w_o, ln1_g, ln1_b, peer_wq, peer_keys, peer_u, peer_v, ln2_g, ln2_b):
    xp2 = x_prompt.reshape(N_CTX_SEQ * CTX_LEN, D_MODEL)
    xs2 = x_sample.reshape(N_LAT_SEQ * LAT_LEN, D_MODEL)
    pos = _grid_pos_embed()

    cond8 = jnp.zeros((8, D_MODEL), F32).at[0].set(c_ctx).at[1:3].set(c)
    mod = _mod_call(cond8, w_ada[0], b_ada[0][None, :])

    w = w_in[0]
    w_in_r = jnp.concatenate([w[:, :1536], w[:, 1568:], w[:, 1536:1568],
                              jnp.zeros((D_MODEL, P_COLS - 2080), F32)], axis=1).astype(BF16)
    gkw_bd = jnp.zeros((128, 2 * D_K), F32)
    gkw_bd = gkw_bd.at[0:GATE_RANK, 0:D_K].set(gk_w[0, 0]).at[GATE_RANK:2 * GATE_RANK, D_K:].set(gk_w[0, 1])
    gkb = gk_b[0].reshape(1, 2 * D_K)
    xpool, q, k, v, gk, ogs = _proj_call(xp2, xs2, pos, mod, w_in_r, gkw_bd.astype(BF16), gkb)

    o_f, st_f = _gla_fwd_call(q, k, v, gk, _state_pack_t(state_fwd[:, 0]), _gla_constants(False))
    x1, st_b = _gla_bwd_call(
        q, k, v, gk, _state_pack_t(state_bwd[:, 0]), _gla_constants(True), o_f, xpool, ogs,
        xp2, xs2, pos, mod, _pool_band(), pool_w[0].astype(BF16), pool_scale[0][None, :],
        gla_norm_g[0][None, :], w_o[0].astype(BF16), ln1_g[0][None, :], ln1_b[0][None, :])

    wqt = peer_wq[0].T.astype(BF16)
    keys = peer_keys[0].reshape(2 * PEER_HEADS, N_KEYS, N_KEYS).astype(BF16)
    h2t, cnt, g1, r2, e2 = _route_call(x1, mod, wqt, keys)

    u_bf = peer_u[0].astype(BF16)
    n_steps = N_EXPERTS // DENSE_CH
    vt_bf = jnp.swapaxes(peer_v[0].astype(BF16).reshape(n_steps, DENSE_CH, D_MODEL), 1, 2)
    y = _dense_call(h2t, cnt, g1, r2, e2, u_bf, vt_bf, x1, mod, ln2_g[0][None, :], ln2_b[0][None, :])

    n_ctx = N_CTX_SEQ * CTX_LEN
    y_prompt = y[:n_ctx].reshape(N_CTX_SEQ, CTX_LEN, D_MODEL)
    y_sample = y[n_ctx:].reshape(N_LAT_SEQ, LAT_LEN, D_MODEL)
    return (y_prompt, y_sample, _state_unpack_t(st_f), _state_unpack_t(st_b))
```

```python
import functools
import math

import numpy as np
import jax
import jax.numpy as jnp
from jax import lax
from jax.experimental import pallas as pl
from jax.experimental.pallas import tpu as pltpu

F32 = jnp.float32
BF16 = jnp.bfloat16

D_MODEL = 1024
N_CTX_SEQ, CTX_LEN = 16, 256
N_LAT_SEQ, LAT_LEN = 2, 2048
GRID_W = 64
N_TOK = N_CTX_SEQ * CTX_LEN + N_LAT_SEQ * LAT_LEN
TILE = 256
N_TILES = N_TOK // TILE
CTX_TILES = N_CTX_SEQ * CTX_LEN // TILE
LAT_TILES = LAT_LEN // TILE

D_POOL = 512
POOL_GROUPS = 4
POOL_CH = 128
POOL_WINDOWS = (2, 4, 8, 16)
POOL_HALO = 8
HEADS = 4
DK = 64
DV = 128
D_K = HEADS * DK
D_V = HEADS * DV
GATE_RANK = 16
GATE_NORM = 16.0
CHUNK = 64
N_LEVELS = 7
P_COLS = 2048 + 128

PEER_HEADS = 8
N_KEYS = 128
PEER_TOPK = 16
N_EXPERTS = N_KEYS * N_KEYS
DENSE_TM = 512
DENSE_CH = 2048
DENSE_PIECES = 8
ALPHA = 2.0 ** 0.25
EPS = 1e-5
VMEM_LIMIT = 56 * 1024 * 1024


def _cparams(sem, flags=None):
    return pltpu.CompilerParams(dimension_semantics=sem, vmem_limit_bytes=VMEM_LIMIT, flags=flags)


def _silu(x):
    return x * jax.nn.sigmoid(x)


def _split3(x):
    x1 = x.astype(BF16)
    r1 = x - x1.astype(F32)
    x2 = r1.astype(BF16)
    r2 = r1 - x2.astype(F32)
    x3 = r2.astype(BF16)
    return x1, x2, x3


def _dot_exact(m01, x):
    c = x.shape[1]
    xs = jnp.concatenate(_split3(x), axis=1)
    y = jnp.dot(m01, xs, preferred_element_type=F32)
    return y[:, :c] + y[:, c:2 * c] + y[:, 2 * c:]


def _layer_norm(y, g, b):
    mu = jnp.mean(y, axis=-1, keepdims=True)
    yc = y - mu
    var = jnp.mean(yc * yc, axis=-1, keepdims=True)
    return yc * lax.rsqrt(var + EPS) * g + b


def _cond_row(tile):
    return (tile >= CTX_TILES).astype(jnp.int32) + (tile >= CTX_TILES + LAT_TILES).astype(jnp.int32)


def _mod_kernel(cond_ref, w_ref, b_ref, out_ref):
    s = _silu(cond_ref[...]).astype(BF16)
    out_ref[...] = jnp.dot(s, w_ref[...].astype(BF16), preferred_element_type=F32) + b_ref[...]


def _mod_call(cond8, w_ada, b_ada):
    n = w_ada.shape[1]
    bn = 1536
    return pl.pallas_call(
        _mod_kernel,
        grid=(n // bn,),
        in_specs=[pl.BlockSpec((8, D_MODEL), lambda j: (0, 0)),
                  pl.BlockSpec((D_MODEL, bn), lambda j: (0, j)),
                  pl.BlockSpec((1, bn), lambda j: (0, j))],
        out_specs=pl.BlockSpec((8, bn), lambda j: (0, j)),
        out_shape=jax.ShapeDtypeStruct((8, n), F32),
        compiler_params=_cparams(("arbitrary",)),
        name="mod",
    )(cond8, w_ada, b_ada)


def _load_x(i, xp_ref, xs_ref, pos_ref):
    return jnp.where(i < CTX_TILES, xp_ref[...], xs_ref[...] + pos_ref[...])


def _mod_rows(mod_ref, r, k):
    return mod_ref[pl.ds(r, 1), k * D_MODEL:(k + 1) * D_MODEL]


def _proj_kernel(xp_ref, xs_ref, pos_ref, mod_ref, w_ref, gkw_ref, gkb_ref,
                 oxp, oq, ok, ov, ogk, oog):
    i = pl.program_id(0)
    r = _cond_row(i)
    x = _load_x(i, xp_ref, xs_ref, pos_ref)
    h = x * (1.0 + _mod_rows(mod_ref, r, 1)) + _mod_rows(mod_ref, r, 0)
    p = jnp.dot(h.astype(BF16), w_ref[...], preferred_element_type=F32)
    oxp[...] = p[:, 0:512]
    oq[...] = p[:, 512:768] * (DK ** -0.5)
    ok[...] = p[:, 768:1024]
    ov[...] = p[:, 1024:1536]
    oog[...] = _silu(p[:, 1536:2048])
    glr = p[:, 2048:2176]
    g_hi = glr.astype(BF16)
    g_lo = (glr - g_hi.astype(F32)).astype(BF16)
    w = gkw_ref[...]
    pre = (jnp.dot(g_hi, w, preferred_element_type=F32)
           + jnp.dot(g_lo, w, preferred_element_type=F32) + gkb_ref[...])
    ogk[...] = (jnp.minimum(pre, 0.0) - jnp.log(1.0 + jnp.exp(-jnp.abs(pre)))) * (1.0 / GATE_NORM)


def _x_specs():
    return [pl.BlockSpec((TILE, D_MODEL), lambda i: (jnp.minimum(i, CTX_TILES - 1), 0)),
            pl.BlockSpec((TILE, D_MODEL), lambda i: (jnp.maximum(i - CTX_TILES, 0), 0)),
            pl.BlockSpec((TILE, D_MODEL), lambda i: (jnp.maximum(i - CTX_TILES, 0) % LAT_TILES, 0))]


def _proj_call(xp2, xs2, pos, mod, w_in_r, gkw_bd, gkb):
    tok = lambda c: pl.BlockSpec((TILE, c), lambda i: (i, 0))
    full = lambda a: pl.BlockSpec(a.shape, lambda i: (0,) * a.ndim)
    outs = [512, 256, 256, 512, 512, 512]
    return pl.pallas_call(
        _proj_kernel,
        grid=(N_TILES,),
        in_specs=_x_specs() + [full(mod), full(w_in_r), full(gkw_bd), full(gkb)],
        out_specs=[tok(c) for c in outs],
        out_shape=[jax.ShapeDtypeStruct((N_TOK, c), F32) for c in outs],
        compiler_params=_cparams(("arbitrary",)),
        name="proj",
    )(xp2, xs2, pos, mod, w_in_r, gkw_bd, gkb)


def _gla_constants(rev):
    t = np.arange(TILE)
    same_chunk = (t[:, None] // CHUNK) == (t[None, :] // CHUNK)
    tri = same_chunk & ((t[None, :] >= t[:, None]) if rev else (t[None, :] <= t[:, None]))
    u = np.arange(CHUNK)
    x = u[:, None] ^ u[None, :]
    lvl = np.where(x > 0, np.floor(np.log2(np.maximum(x, 1))), N_LEVELS - 1).astype(np.int32)
    causal = (u[:, None] <= u[None, :]) if rev else (u[:, None] >= u[None, :])
    lvl = np.where(causal, lvl, -1).astype(np.int32)
    return jnp.asarray(tri, BF16), jnp.asarray(np.tile(lvl, (1, HEADS)), jnp.int32)


def _level_ref(b_scr, level, rev):
    half = 1 << level
    sub = lax.broadcasted_iota(jnp.int32, (8, D_K), 0)
    row = lambda r: jnp.broadcast_to(b_scr[r:r + 1, :], (8, D_K))
    pieces = []
    for m in range(TILE // 8):
        refs = [((8 * m + u) // (2 * half)) * (2 * half) + (half if rev else half - 1) for u in range(8)]
        piece = row(refs[0])
        for u in range(1, 8):
            if refs[u] != refs[u - 1]:
                piece = jnp.where(sub >= u, row(refs[u]), piece)
        pieces.append(piece)
    return jnp.concatenate(pieces, axis=0)


def _head_blocks(x, width):
    head = lax.broadcasted_iota(jnp.int32, x.shape, 1) // width
    return jnp.concatenate([jnp.where(head == h, x, jnp.zeros_like(x)) for h in range(HEADS)], axis=0)


def _gla_tile(q, k, v, g, tri, lmap, st_ref, b_scr, rev):
    b = _dot_exact(tri, g)
    b_scr[...] = b
    qds, kds = [], []
    for l in range(N_LEVELS - 1):
        d = b - _level_ref(b_scr, l, rev)
        qds.append((q * jnp.exp(jnp.minimum(d, 0.0))).astype(BF16))
        kds.append((k * jnp.exp(jnp.minimum(-d, 0.0))).astype(BF16))
    qds.append(q.astype(BF16))
    kds.append(k.astype(BF16))
    qe = (q * jnp.exp(b)).astype(BF16)
    vb = v.astype(BF16)

    cdim = (((1,), (1,)), ((), ()))
    st_head = lax.broadcasted_iota(jnp.int32, (DV, D_K), 1) // DK
    st = st_ref[...]
    outs = [None] * (TILE // CHUNK)
    for c in (reversed(range(TILE // CHUNK)) if rev else range(TILE // CHUNK)):
        rows = slice(c * CHUNK, (c + 1) * CHUNK)
        o = lax.dot_general(qe[rows], _head_blocks(st.astype(BF16), DK), cdim,
                            preferred_element_type=F32)
        p = jnp.zeros((CHUNK, HEADS * CHUNK), F32)
        for l in range(N_LEVELS):
            z = lax.dot_general(qds[l][rows], _head_blocks(kds[l][rows], DK), cdim,
                                preferred_element_type=F32)
            p = jnp.where(lmap == l, z, p)
        outs[c] = o + jnp.dot(p.astype(BF16), _head_blocks(vb[rows], DV), preferred_element_type=F32)

        b_c = b[rows]
        b_end = b_c[0:1, :] if rev else b_c[CHUNK - 1:CHUNK, :]
        kdec = (k[rows] * jnp.exp(b_end - b_c)).astype(BF16)
        kv = lax.dot_general(vb[rows], kdec, (((0,), (0,)), ((), ())), preferred_element_type=F32)
        kvd = kv[0:DV]
        for h in range(1, HEADS):
            kvd = jnp.where(st_head == h, kv[h * DV:(h + 1) * DV], kvd)
        st = st * jnp.exp(b_end) + kvd
    st_ref[...] = st
    return jnp.concatenate(outs, axis=0)


def _gla_fwd_kernel(q_ref, k_ref, v_ref, g_ref, st0_ref, tri_ref, lmap_ref,
                    of_ref, stout_ref, st_scr, b_scr):
    i = pl.program_id(0)
    seq_start = (i <= CTX_TILES) | (i == CTX_TILES + LAT_TILES)

    @pl.when(seq_start)
    def _():
        st_scr[...] = st0_ref[0]

    of_ref[...] = _gla_tile(q_ref[...], k_ref[...], v_ref[...], g_ref[...], tri_ref[...],
                            lmap_ref[...], st_scr, b_scr, rev=False)

    @pl.when(i < CTX_TILES)
    def _():
        stout_ref[0] = st_scr[...]


def _gla_fwd_call(q, k, v, gk, st0, consts):
    tri, lmap = consts
    tok = lambda c: pl.BlockSpec((TILE, c), lambda i: (i, 0))
    full = lambda a: pl.BlockSpec(a.shape, lambda i: (0,) * a.ndim)
    return pl.pallas_call(
        _gla_fwd_kernel,
        grid=(N_TILES,),
        in_specs=[tok(D_K), tok(D_K), tok(D_V), pl.BlockSpec((TILE, D_K), lambda i: (i, 0)),
                  pl.BlockSpec((1, DV, D_K), lambda i: (_cond_row(i), 0, 0)),
                  full(tri), full(lmap)],
        out_specs=[tok(D_V),
                   pl.BlockSpec((1, DV, D_K), lambda i: (jnp.minimum(i, CTX_TILES - 1), 0, 0))],
        out_shape=[jax.ShapeDtypeStruct((N_TOK, D_V), F32),
                   jax.ShapeDtypeStruct((N_CTX_SEQ, DV, D_K), F32)],
        scratch_shapes=[pltpu.VMEM((DV, D_K), F32), pltpu.VMEM((TILE, D_K), F32)],
        compiler_params=_cparams(("arbitrary",)),
        name="gla_fwd",
    )(q, k, v, gk, st0, tri, lmap)


def _pool_band():
    t = np.arange(TILE)[:, None]
    e = np.arange(TILE + 2 * POOL_HALO)[None, :]
    bands = []
    for w in POOL_WINDOWS:
        lo = t + POOL_HALO - w // 2
        bands.append(((e >= lo) & (e < lo + w)).astype(np.float32))
    return jnp.asarray(np.stack(bands), BF16)


def _gla_bwd_kernel(q_ref, k_ref, v_ref, g_ref, st0_ref, tri_ref, lmap_ref,
                    of_ref, xpool_ref, prev_ref, next_ref, ogs_ref, xp_ref, xs_ref, pos_ref, mod_ref,
                    band_ref, poolw_ref, pscale_ref, gnorm_ref, wo_ref, ln_g_ref, ln_b_ref,
                    x1_ref, stout_ref, st_scr, b_scr):
    j = N_TILES - 1 - pl.program_id(0)
    lat_idx = jnp.maximum(j - CTX_TILES, 0) % LAT_TILES
    is_ctx = j < CTX_TILES
    seq_first = is_ctx | (lat_idx == 0)
    seq_last = is_ctx | (lat_idx == LAT_TILES - 1)

    @pl.when(seq_last)
    def _():
        st_scr[...] = st0_ref[0]

    o_b = _gla_tile(q_ref[...], k_ref[...], v_ref[...], g_ref[...], tri_ref[...],
                    lmap_ref[...], st_scr, b_scr, rev=True)

    @pl.when(is_ctx)
    def _():
        stout_ref[0] = st_scr[...]

    xpool = xpool_ref[...]
    prev = jnp.where(seq_first, 0.0, prev_ref[...])
    nxt = jnp.where(seq_last, 0.0, next_ref[...])
    ext = jnp.concatenate([prev, xpool, nxt], axis=0)
    tpos = jnp.where(is_ctx, 0, lat_idx) * TILE + lax.broadcasted_iota(jnp.int32, (TILE, POOL_CH), 0)
    seq_len = jnp.where(is_ctx, CTX_LEN, LAT_LEN)
    pool_parts = []
    for gi, w in enumerate(POOL_WINDOWS):
        sl = slice(gi * POOL_CH, (gi + 1) * POOL_CH)
        wsum = _dot_exact(band_ref[gi], ext[:, sl])
        lo = jnp.maximum(tpos - w // 2, 0)
        hi = jnp.minimum(tpos - w // 2 + w, seq_len)
        dmean = wsum / (hi - lo).astype(F32) - xpool[:, sl]
        y = jnp.dot(dmean.astype(BF16), poolw_ref[gi], preferred_element_type=F32)
        pool_parts.append(y * pscale_ref[:, sl])

    o = of_ref[...] + o_b
    ogs = ogs_ref[...]
    for h in range(HEADS):
        sl = slice(h * DV, (h + 1) * DV)
        oh = o[:, sl]
        oh = oh * lax.rsqrt(jnp.mean(oh * oh, axis=-1, keepdims=True) + EPS)
        pool_parts.append(oh * gnorm_ref[:, sl] * ogs[:, sl])
    mix_in = jnp.concatenate(pool_parts, axis=1).astype(BF16)
    mix = jnp.dot(mix_in, wo_ref[...], preferred_element_type=F32)

    r = _cond_row(j)
    x = _load_x(j, xp_ref, xs_ref, pos_ref)
    y = ALPHA * x + _mod_rows(mod_ref, r, 2) * mix
    x1_ref[...] = _layer_norm(y, ln_g_ref[...], ln_b_ref[...])


def _gla_bwd_call(q, k, v, gk, st0, consts, o_f, xpool, ogs, xp2, xs2, pos, mod,
                  band, poolw, pscale, gnorm, wo, ln_g, ln_b):
    tri, lmap = consts
    rv = lambda i: N_TILES - 1 - i
    tok = lambda c: pl.BlockSpec((TILE, c), lambda i: (rv(i), 0))
    full = lambda a: pl.BlockSpec(a.shape, lambda i: (0,) * a.ndim)
    halo_blocks = TILE // POOL_HALO
    n_halo = N_TOK // POOL_HALO
    xspecs = [pl.BlockSpec((TILE, D_MODEL), lambda i: (jnp.minimum(rv(i), CTX_TILES - 1), 0)),
              pl.BlockSpec((TILE, D_MODEL), lambda i: (jnp.maximum(rv(i) - CTX_TILES, 0), 0)),
              pl.BlockSpec((TILE, D_MODEL), lambda i: (jnp.maximum(rv(i) - CTX_TILES, 0) % LAT_TILES, 0))]
    return pl.pallas_call(
        _gla_bwd_kernel,
        grid=(N_TILES,),
        in_specs=[tok(D_K), tok(D_K), tok(D_V), pl.BlockSpec((TILE, D_K), lambda i: (rv(i), 1)),
                  pl.BlockSpec((1, DV, D_K), lambda i: (_cond_row(rv(i)), 0, 0)),
                  full(tri), full(lmap),
                  tok(D_V), tok(D_POOL),
                  pl.BlockSpec((POOL_HALO, D_POOL), lambda i: (jnp.maximum(rv(i) * halo_blocks - 1, 0), 0)),
                  pl.BlockSpec((POOL_HALO, D_POOL),
                               lambda i: (jnp.minimum((rv(i) + 1) * halo_blocks, n_halo - 1), 0)),
                  tok(D_V)] + xspecs + [full(mod), full(band), full(poolw), full(pscale), full(gnorm),
                                        full(wo), full(ln_g), full(ln_b)],
        out_specs=[tok(D_MODEL),
                   pl.BlockSpec((1, DV, D_K), lambda i: (jnp.minimum(rv(i), CTX_TILES - 1), 0, 0))],
        out_shape=[jax.ShapeDtypeStruct((N_TOK, D_MODEL), F32),
                   jax.ShapeDtypeStruct((N_CTX_SEQ, DV, D_K), F32)],
        scratch_shapes=[pltpu.VMEM((DV, D_K), F32), pltpu.VMEM((TILE, D_K), F32)],
        compiler_params=_cparams(("arbitrary",)),
        name="gla_bwd_mix",
    )(q, k, v, gk, st0, tri, lmap, o_f, xpool, xpool, xpool, ogs, xp2, xs2, pos, mod,
      band, poolw, pscale, gnorm, wo, ln_g, ln_b)


def _top_values(x, n, with_rank=False):
    vals = []
    rank = jnp.full(x.shape, float(n), F32)
    for j in range(n):
        m = jnp.max(x, axis=0, keepdims=True)
        vals.append(m)
        hit = x == m
        if with_rank:
            rank = jnp.where(hit, float(j), rank)
        x = jnp.where(hit, -jnp.inf, x)
    return (vals, rank) if with_rank else vals


def _route_kernel(x1_ref, mod_ref, wqt_ref, keys_ref, h2t_ref, cnt_ref, g1_ref, r2_ref, e2_ref):
    i = pl.program_id(0)
    r = _cond_row(i)
    h2 = x1_ref[...] * (1.0 + _mod_rows(mod_ref, r, 4)) + _mod_rows(mod_ref, r, 3)
    h2t = h2.T.astype(BF16)
    h2t_ref[...] = h2t
    qt = jnp.dot(wqt_ref[...], h2t, preferred_element_type=F32)
    for h in range(PEER_HEADS):
        s = []
        for p in range(2):
            row = (2 * h + p) * N_KEYS
            qhp = qt[row:row + N_KEYS, :].astype(BF16)
            s.append(jnp.dot(keys_ref[2 * h + p], qhp, preferred_element_type=F32))
        sv1, rank1 = _top_values(s[0], PEER_TOPK, with_rank=True)
        sv2, rank2 = _top_values(s[1], PEER_TOPK, with_rank=True)
        a1 = jnp.concatenate(sv1, axis=0)
        a2 = jnp.concatenate(sv2, axis=0)
        row8 = lax.broadcasted_iota(jnp.int32, (8, TILE), 0)
        cand = [a2 + sv1[0]]
        for a in range(1, 8):
            cand.append(jnp.where(row8 < PEER_TOPK // (a + 1), a2[0:8] + sv1[a], -jnp.inf))
        cand.append(a1[8:16] + sv2[0])
        fv = _top_values(jnp.concatenate(cand, axis=0), PEER_TOPK)
        denom = jnp.zeros_like(fv[0])
        for f in fv:
            denom = denom + jnp.exp(f - fv[0])
        thr = fv[PEER_TOPK - 1]
        cnt = jnp.zeros_like(rank1)
        for a in range(PEER_TOPK):
            blk = cand[a] if a < 8 else cand[8][a - 8:a - 7]
            n_a = jnp.sum((blk >= thr).astype(F32), axis=0, keepdims=True)
            cnt = jnp.where(rank1 == float(a), n_a, cnt)
        cnt_ref[h] = cnt
        g1_ref[h] = jnp.exp(s[0] - sv1[0]) / denom
        r2_ref[h] = rank2.astype(BF16)
        e2_ref[h] = jnp.exp(s[1] - sv2[0]).astype(BF16)


def _route_call(x1, mod, wqt, keys):
    full = lambda a: pl.BlockSpec(a.shape, lambda i: (0,) * a.ndim)
    hk = pl.BlockSpec((PEER_HEADS, N_KEYS, TILE), lambda i: (0, 0, i))
    per_key = lambda dt: jax.ShapeDtypeStruct((PEER_HEADS, N_KEYS, N_TOK), dt)
    return pl.pallas_call(
        _route_kernel,
        grid=(N_TILES,),
        in_specs=[pl.BlockSpec((TILE, D_MODEL), lambda i: (i, 0)), full(mod), full(wqt), full(keys)],
        out_specs=[pl.BlockSpec((D_MODEL, TILE), lambda i: (0, i)), hk, hk, hk, hk],
        out_shape=[jax.ShapeDtypeStruct((D_MODEL, N_TOK), BF16),
                   per_key(F32), per_key(F32), per_key(BF16), per_key(BF16)],
        compiler_params=_cparams(("arbitrary",)),
        name="peer_route",
    )(x1, mod, wqt, keys)


def _gelu_tanh(x):
    c1 = math.sqrt(2.0 / math.pi)
    c2 = c1 * 0.044715
    return x * (0.5 + 0.5 * jnp.tanh(x * (c1 + c2 * (x * x))))


def _dense_kernel(h2t_ref, cnt_ref, g1_ref, r2_ref, e2_ref, u_ref, vt_ref, x1_ref, mod_ref,
                  ln_g_ref, ln_b_ref, y_ref, acc_ref):
    tt = pl.program_id(0)
    c = pl.program_id(1)
    n_sub = DENSE_CH // N_KEYS

    @pl.when(c == 0)
    def _():
        acc_ref[...] = jnp.zeros_like(acc_ref)

    i1_base = pl.multiple_of(c * n_sub, n_sub)
    cnt8 = [cnt_ref[h, pl.ds(i1_base, n_sub), :].astype(BF16) for h in range(PEER_HEADS)]
    g18 = [g1_ref[h, pl.ds(i1_base, n_sub), :].astype(BF16) for h in range(PEER_HEADS)]
    def gate(ci):
        wsum = jnp.zeros((N_KEYS, DENSE_TM), BF16)
        for h in range(PEER_HEADS):
            sel = r2_ref[h] < cnt8[h][ci:ci + 1, :]
            wsum = wsum + jnp.where(sel, e2_ref[h], jnp.zeros((), BF16)) * g18[h][ci:ci + 1, :]
        return wsum

    sub_per_piece = n_sub // DENSE_PIECES
    gates, a_pieces = [], []
    for pc in range(DENSE_PIECES):
        rows = slice(pc * sub_per_piece * N_KEYS, (pc + 1) * sub_per_piece * N_KEYS)
        lhs = u_ref[rows, :]
        if pc > 0:
            zero = gates[-1][0:16, 0:128] * jnp.zeros((), BF16)
            top = jnp.concatenate([lhs[0:16, 0:128] + zero, lhs[0:16, 128:]], axis=1)
            lhs = jnp.concatenate([top, lhs[16:]], axis=0)
        a_pieces.append(jnp.dot(lhs, h2t_ref[...], preferred_element_type=F32))
        gates.extend(gate(ci) for ci in range(pc * sub_per_piece, (pc + 1) * sub_per_piece))
    a = jnp.concatenate(a_pieces, axis=0)
    w_parts = [_gelu_tanh(a[ci * N_KEYS:(ci + 1) * N_KEYS, :].astype(BF16)) * gates[ci] for ci in range(n_sub)]
    zero = gates[-1][0:16, 0:128] * jnp.zeros((), BF16)
    w0 = w_parts[0]
    top = jnp.concatenate([w0[0:16, 0:128] + zero, w0[0:16, 128:]], axis=1)
    w_parts[0] = jnp.concatenate([top, w0[16:]], axis=0)
    w = jnp.concatenate(w_parts, axis=0)
    acc_ref[...] += jnp.dot(vt_ref[0], w, preferred_element_type=F32)

    @pl.when(c == pl.num_programs(1) - 1)
    def _():
        r = _cond_row(tt * (DENSE_TM // TILE))
        ffn = acc_ref[...].T
        y = ALPHA * x1_ref[...] + _mod_rows(mod_ref, r, 5) * ffn
        y_ref[...] = _layer_norm(y, ln_g_ref[...], ln_b_ref[...])


def _dense_call(h2t, cnt, g1, r2, e2, u_bf, vt_bf, x1, mod, ln_g, ln_b):
    full = lambda a: pl.BlockSpec(a.shape, lambda t, c: (0,) * a.ndim)
    hk = pl.BlockSpec((PEER_HEADS, N_KEYS, DENSE_TM), lambda t, c: (0, 0, t))
    return pl.pallas_call(
        _dense_kernel,
        grid=(N_TOK // DENSE_TM, N_EXPERTS // DENSE_CH),
        in_specs=[pl.BlockSpec((D_MODEL, DENSE_TM), lambda t, c: (0, t)),
                  hk, hk, hk, hk,
                  pl.BlockSpec((DENSE_CH, D_MODEL), lambda t, c: (c, 0)),
                  pl.BlockSpec((1, D_MODEL, DENSE_CH), lambda t, c: (c, 0, 0)),
                  pl.BlockSpec((DENSE_TM, D_MODEL), lambda t, c: (t, 0)),
                  full(mod), full(ln_g), full(ln_b)],
        out_specs=pl.BlockSpec((DENSE_TM, D_MODEL), lambda t, c: (t, 0)),
        out_shape=jax.ShapeDtypeStruct((N_TOK, D_MODEL), F32),
        scratch_shapes=[pltpu.VMEM((D_MODEL, DENSE_TM), F32)],
        compiler_params=_cparams(("arbitrary", "arbitrary")),
        name="peer_dense",
    )(h2t, cnt, g1, r2, e2, u_bf, vt_bf, x1, mod, ln_g, ln_b)


def _grid_pos_embed():
    rows = LAT_LEN // GRID_W
    r, col = np.meshgrid(np.arange(rows), np.arange(GRID_W), indexing="ij")

    def sincos(pos, dim):
        omega = 1.0 / (10000.0 ** (np.arange(dim // 2, dtype=np.float64) / (dim // 2)))
        ang = pos.reshape(-1).astype(np.float64)[:, None] * omega[None, :]
        return np.concatenate([np.sin(ang), np.cos(ang)], axis=-1)

    pe = np.concatenate([sincos(r, D_MODEL // 2), sincos(col, D_MODEL // 2)], axis=-1)
    return jnp.asarray(pe, F32)


def _state_pack_t(state):
    packed = jnp.transpose(state, (0, 3, 1, 2)).reshape(N_LAT_SEQ, DV, D_K)
    return jnp.concatenate([jnp.zeros((1, DV, D_K), F32), packed], axis=0)


def _state_unpack_t(st):
    return jnp.transpose(st.reshape(N_CTX_SEQ, DV, HEADS, DK), (0, 2, 3, 1))[:, None]


def kernel(x_prompt, x_sample, c, state_fwd, state_bwd, c_ctx, w_ada, b_ada, w_in, pool_w, pool_scale,
           gk_w, gk_b, gla_norm_g, w_o, ln1_g, ln1_b, peer_wq, peer_keys, peer_u, peer_v, ln2_g, ln2_b):
    xp2 = x_prompt.reshape(N_CTX_SEQ * CTX_LEN, D_MODEL)
    xs2 = x_sample.reshape(N_LAT_SEQ * LAT_LEN, D_MODEL)
    pos = _grid_pos_embed()

    cond8 = jnp.zeros((8, D_MODEL), F32).at[0].set(c_ctx).at[1:3].set(c)
    mod = _mod_call(cond8, w_ada[0], b_ada[0][None, :])

    w = w_in[0]
    w_in_r = jnp.concatenate([w[:, :1536], w[:, 1568:], w[:, 1536:1568],
                              jnp.zeros((D_MODEL, P_COLS - 2080), F32)], axis=1).astype(BF16)
    gkw_bd = jnp.zeros((128, 2 * D_K), F32)
    gkw_bd = gkw_bd.at[0:GATE_RANK, 0:D_K].set(gk_w[0, 0]).at[GATE_RANK:2 * GATE_RANK, D_K:].set(gk_w[0, 1])
    gkb = gk_b[0].reshape(1, 2 * D_K)
    xpool, q, k, v, gk, ogs = _proj_call(xp2, xs2, pos, mod, w_in_r, gkw_bd.astype(BF16), gkb)

    o_f, st_f = _gla_fwd_call(q, k, v, gk, _state_pack_t(state_fwd[:, 0]), _gla_constants(False))
    x1, st_b = _gla_bwd_call(
        q, k, v, gk, _state_pack_t(state_bwd[:, 0]), _gla_constants(True), o_f, xpool, ogs,
        xp2, xs2, pos, mod, _pool_band(), pool_w[0].astype(BF16), pool_scale[0][None, :],
        gla_norm_g[0][None, :], w_o[0].astype(BF16), ln1_g[0][None, :], ln1_b[0][None, :])

    wqt = peer_wq[0].T.astype(BF16)
    keys = peer_keys[0].reshape(2 * PEER_HEADS, N_KEYS, N_KEYS).astype(BF16)
    h2t, cnt, g1, r2, e2 = _route_call(x1, mod, wqt, keys)

    u_bf = peer_u[0].astype(BF16)
    n_steps = N_EXPERTS // DENSE_CH
    vt_bf = jnp.swapaxes(peer_v[0].astype(BF16).reshape(n_steps, DENSE_CH, D_MODEL), 1, 2)
    y = _dense_call(h2t, cnt, g1, r2, e2, u_bf, vt_bf, x1, mod, ln2_g[0][None, :], ln2_b[0][None, :])

    n_ctx = N_CTX_SEQ * CTX_LEN
    y_prompt = y[:n_ctx].reshape(N_CTX_SEQ, CTX_LEN, D_MODEL)
    y_sample = y[n_ctx:].reshape(N_LAT_SEQ, LAT_LEN, D_MODEL)
    return (y_prompt, y_sample, _state_unpack_t(st_f), _state_unpack_t(st_b))
```

```python
import functools
import math

import numpy as np
import jax
import jax.numpy as jnp
from jax import lax
from jax.experimental import pallas as pl
from jax.experimental.pallas import tpu as pltpu

F32 = jnp.float32
BF16 = jnp.bfloat16

D_MODEL = 1024
N_CTX_SEQ, CTX_LEN = 16, 256
N_LAT_SEQ, LAT_LEN = 2, 2048
GRID_W = 64
N_TOK = N_CTX_SEQ * CTX_LEN + N_LAT_SEQ * LAT_LEN
TILE = 256
N_TILES = N_TOK // TILE
CTX_TILES = N_CTX_SEQ * CTX_LEN // TILE
LAT_TILES = LAT_LEN // TILE

D_POOL = 512
POOL_GROUPS = 4
POOL_CH = 128
POOL_WINDOWS = (2, 4, 8, 16)
POOL_HALO = 8
HEADS = 4
DK = 64
DV = 128
D_K = HEADS * DK
D_V = HEADS * DV
GATE_RANK = 16
GATE_NORM = 16.0
CHUNK = 64
N_LEVELS = 7
P_COLS = 2048 + 128

PEER_HEADS = 8
N_KEYS = 128
PEER_TOPK = 16
N_EXPERTS = N_KEYS * N_KEYS
WQ_SLABS = PEER_HEADS * 2 * N_KEYS // 128
DENSE_TM = 512
DENSE_CH = 2048
DENSE_CTX_TILES = N_CTX_SEQ * CTX_LEN // DENSE_TM
DENSE_PIECES = 8
ALPHA = 2.0 ** 0.25
EPS = 1e-5
VMEM_LIMIT = 56 * 1024 * 1024


def _cparams(sem, flags=None):
    return pltpu.CompilerParams(dimension_semantics=sem, vmem_limit_bytes=VMEM_LIMIT, flags=flags)


def _silu(x):
    return x * jax.nn.sigmoid(x)


def _split3(x):
    x1 = x.astype(BF16)
    r1 = x - x1.astype(F32)
    x2 = r1.astype(BF16)
    r2 = r1 - x2.astype(F32)
    x3 = r2.astype(BF16)
    return x1, x2, x3


def _dot_exact(m01, x):
    c = x.shape[1]
    xs = jnp.concatenate(_split3(x), axis=1)
    y = jnp.dot(m01, xs, preferred_element_type=F32)
    return y[:, :c] + y[:, c:2 * c] + y[:, 2 * c:]


def _layer_norm(y, g, b):
    mu = jnp.mean(y, axis=-1, keepdims=True)
    yc = y - mu
    var = jnp.mean(yc * yc, axis=-1, keepdims=True)
    return yc * lax.rsqrt(var + EPS) * g + b


def _cond_row(tile):
    return (tile >= CTX_TILES).astype(jnp.int32) + (tile >= CTX_TILES + LAT_TILES).astype(jnp.int32)


def _mod_kernel(cond_ref, w_ref, b_ref, out_ref):
    s = _silu(cond_ref[...]).astype(BF16)
    out_ref[...] = jnp.dot(s, w_ref[...].astype(BF16), preferred_element_type=F32) + b_ref[...]


def _mod_call(cond8, w_ada, b_ada):
    n = w_ada.shape[1]
    bn = 1536
    return pl.pallas_call(
        _mod_kernel,
        grid=(n // bn,),
        in_specs=[pl.BlockSpec((8, D_MODEL), lambda j: (0, 0)),
                  pl.BlockSpec((D_MODEL, bn), lambda j: (0, j)),
                  pl.BlockSpec((1, bn), lambda j: (0, j))],
        out_specs=pl.BlockSpec((8, bn), lambda j: (0, j)),
        out_shape=jax.ShapeDtypeStruct((8, n), F32),
        compiler_params=_cparams(("arbitrary",)),
        name="mod",
    )(cond8, w_ada, b_ada)


def _load_x(i, xp_ref, xs_ref, pos_ref):
    return jnp.where(i < CTX_TILES, xp_ref[...], xs_ref[...] + pos_ref[...])


def _mod_rows(mod_ref, r, k):
    return mod_ref[pl.ds(r, 1), k * D_MODEL:(k + 1) * D_MODEL]


def _proj_kernel(xp_ref, xs_ref, pos_ref, mod_ref, w_ref, gkw_ref, gkb_ref,
                 oxp, oq, ok, ov, ogk, oog):
    i = pl.program_id(0)
    r = _cond_row(i)
    x = _load_x(i, xp_ref, xs_ref, pos_ref)
    h = x * (1.0 + _mod_rows(mod_ref, r, 1)) + _mod_rows(mod_ref, r, 0)
    p = jnp.dot(h.astype(BF16), w_ref[...], preferred_element_type=F32)
    oxp[...] = p[:, 0:512]
    oq[...] = p[:, 512:768] * (DK ** -0.5)
    ok[...] = p[:, 768:1024]
    ov[...] = p[:, 1024:1536]
    oog[...] = _silu(p[:, 1536:2048])
    glr = p[:, 2048:2176]
    g_hi = glr.astype(BF16)
    g_lo = (glr - g_hi.astype(F32)).astype(BF16)
    w = gkw_ref[...]
    pre = (jnp.dot(g_hi, w, preferred_element_type=F32)
           + jnp.dot(g_lo, w, preferred_element_type=F32) + gkb_ref[...])
    ogk[...] = (jnp.minimum(pre, 0.0) - jnp.log(1.0 + jnp.exp(-jnp.abs(pre)))) * (1.0 / GATE_NORM)


def _x_specs():
    return [pl.BlockSpec((TILE, D_MODEL), lambda i: (jnp.minimum(i, CTX_TILES - 1), 0)),
            pl.BlockSpec((TILE, D_MODEL), lambda i: (jnp.maximum(i - CTX_TILES, 0), 0)),
            pl.BlockSpec((TILE, D_MODEL), lambda i: (jnp.maximum(i - CTX_TILES, 0) % LAT_TILES, 0))]


def _proj_call(xp2, xs2, pos, mod, w_in_r, gkw_bd, gkb):
    tok = lambda c: pl.BlockSpec((TILE, c), lambda i: (i, 0))
    full = lambda a: pl.BlockSpec(a.shape, lambda i: (0,) * a.ndim)
    outs = [512, 256, 256, 512, 512, 512]
    return pl.pallas_call(
        _proj_kernel,
        grid=(N_TILES,),
        in_specs=_x_specs() + [full(mod), full(w_in_r), full(gkw_bd), full(gkb)],
        out_specs=[tok(c) for c in outs],
        out_shape=[jax.ShapeDtypeStruct((N_TOK, c), F32) for c in outs],
        compiler_params=_cparams(("arbitrary",)),
        name="proj",
    )(xp2, xs2, pos, mod, w_in_r, gkw_bd, gkb)


def _gla_constants(rev):
    t = np.arange(TILE)
    same_chunk = (t[:, None] // CHUNK) == (t[None, :] // CHUNK)
    tri = same_chunk & ((t[None, :] >= t[:, None]) if rev else (t[None, :] <= t[:, None]))
    u = np.arange(CHUNK)
    x = u[:, None] ^ u[None, :]
    lvl = np.where(x > 0, np.floor(np.log2(np.maximum(x, 1))), N_LEVELS - 1).astype(np.int32)
    causal = (u[:, None] <= u[None, :]) if rev else (u[:, None] >= u[None, :])
    lvl = np.where(causal, lvl, -1).astype(np.int32)
    return jnp.asarray(tri, BF16), jnp.asarray(np.tile(lvl, (1, HEADS)), jnp.int32)


def _level_ref(b_scr, level, rev):
    half = 1 << level
    sub = lax.broadcasted_iota(jnp.int32, (8, D_K), 0)
    row = lambda r: jnp.broadcast_to(b_scr[r:r + 1, :], (8, D_K))
    pieces = []
    for m in range(TILE // 8):
        refs = [((8 * m + u) // (2 * half)) * (2 * half) + (half if rev else half - 1) for u in range(8)]
        piece = row(refs[0])
        for u in range(1, 8):
            if refs[u] != refs[u - 1]:
                piece = jnp.where(sub >= u, row(refs[u]), piece)
        pieces.append(piece)
    return jnp.concatenate(pieces, axis=0)


def _head_blocks(x, width):
    head = lax.broadcasted_iota(jnp.int32, x.shape, 1) // width
    return jnp.concatenate([jnp.where(head == h, x, jnp.zeros_like(x)) for h in range(HEADS)], axis=0)


def _gla_tile(q, k, v, g, tri, lmap, st_ref, b_scr, rev):
    b = _dot_exact(tri, g)
    b_scr[...] = b
    qds, kds = [], []
    for l in range(N_LEVELS - 1):
        d = b - _level_ref(b_scr, l, rev)
        qds.append((q * jnp.exp(jnp.minimum(d, 0.0))).astype(BF16))
        kds.append((k * jnp.exp(jnp.minimum(-d, 0.0))).astype(BF16))
    qds.append(q.astype(BF16))
    kds.append(k.astype(BF16))
    qe = (q * jnp.exp(b)).astype(BF16)
    vb = v.astype(BF16)

    cdim = (((1,), (1,)), ((), ()))
    st_head = lax.broadcasted_iota(jnp.int32, (DV, D_K), 1) // DK
    st = st_ref[...]
    outs = [None] * (TILE // CHUNK)
    for c in (reversed(range(TILE // CHUNK)) if rev else range(TILE // CHUNK)):
        rows = slice(c * CHUNK, (c + 1) * CHUNK)
        o = lax.dot_general(qe[rows], _head_blocks(st.astype(BF16), DK), cdim,
                            preferred_element_type=F32)
        p = jnp.zeros((CHUNK, HEADS * CHUNK), F32)
        for l in range(N_LEVELS):
            z = lax.dot_general(qds[l][rows], _head_blocks(kds[l][rows], DK), cdim,
                                preferred_element_type=F32)
            p = jnp.where(lmap == l, z, p)
        outs[c] = o + jnp.dot(p.astype(BF16), _head_blocks(vb[rows], DV), preferred_element_type=F32)

        b_c = b[rows]
        b_end = b_c[0:1, :] if rev else b_c[CHUNK - 1:CHUNK, :]
        kdec = (k[rows] * jnp.exp(b_end - b_c)).astype(BF16)
        kv = lax.dot_general(vb[rows], kdec, (((0,), (0,)), ((), ())), preferred_element_type=F32)
        kvd = kv[0:DV]
        for h in range(1, HEADS):
            kvd = jnp.where(st_head == h, kv[h * DV:(h + 1) * DV], kvd)
        st = st * jnp.exp(b_end) + kvd
    st_ref[...] = st
    return jnp.concatenate(outs, axis=0)


def _gla_fwd_kernel(q_ref, k_ref, v_ref, g_ref, st0_ref, tri_ref, lmap_ref, wq_ref,
                    of_ref, stout_ref, wqt_ref, st_scr, b_scr):
    i = pl.program_id(0)
    seq_start = (i <= CTX_TILES) | (i == CTX_TILES + LAT_TILES)

    @pl.when(seq_start)
    def _():
        st_scr[...] = st0_ref[0]

    @pl.when(i < WQ_SLABS)
    def _():
        wqt_ref[...] = wq_ref[...].T.astype(BF16)

    of_ref[...] = _gla_tile(q_ref[...], k_ref[...], v_ref[...], g_ref[...], tri_ref[...],
                            lmap_ref[...], st_scr, b_scr, rev=False)

    @pl.when(i < CTX_TILES)
    def _():
        stout_ref[0] = st_scr[...]


def _gla_fwd_call(q, k, v, gk, st0, consts, wq):
    tri, lmap = consts
    tok = lambda c: pl.BlockSpec((TILE, c), lambda i: (i, 0))
    full = lambda a: pl.BlockSpec(a.shape, lambda i: (0,) * a.ndim)
    slab = lambda i: jnp.minimum(i, WQ_SLABS - 1)
    return pl.pallas_call(
        _gla_fwd_kernel,
        grid=(N_TILES,),
        in_specs=[tok(D_K), tok(D_K), tok(D_V), pl.BlockSpec((TILE, D_K), lambda i: (i, 0)),
                  pl.BlockSpec((1, DV, D_K), lambda i: (_cond_row(i), 0, 0)),
                  full(tri), full(lmap),
                  pl.BlockSpec((D_MODEL, 128), lambda i: (0, slab(i)))],
        out_specs=[tok(D_V),
                   pl.BlockSpec((1, DV, D_K), lambda i: (jnp.minimum(i, CTX_TILES - 1), 0, 0)),
                   pl.BlockSpec((128, D_MODEL), lambda i: (slab(i), 0))],
        out_shape=[jax.ShapeDtypeStruct((N_TOK, D_V), F32),
                   jax.ShapeDtypeStruct((N_CTX_SEQ, DV, D_K), F32),
                   jax.ShapeDtypeStruct((WQ_SLABS * 128, D_MODEL), BF16)],
        scratch_shapes=[pltpu.VMEM((DV, D_K), F32), pltpu.VMEM((TILE, D_K), F32)],
        compiler_params=_cparams(("arbitrary",)),
        name="gla_fwd",
    )(q, k, v, gk, st0, tri, lmap, wq)


def _pool_band():
    t = np.arange(TILE)[:, None]
    e = np.arange(TILE + 2 * POOL_HALO)[None, :]
    bands = []
    for w in POOL_WINDOWS:
        lo = t + POOL_HALO - w // 2
        bands.append(((e >= lo) & (e < lo + w)).astype(np.float32))
    return jnp.asarray(np.stack(bands), BF16)


def _gla_bwd_kernel(q_ref, k_ref, v_ref, g_ref, st0_ref, tri_ref, lmap_ref,
                    of_ref, xpool_ref, prev_ref, next_ref, ogs_ref, xp_ref, xs_ref, pos_ref, mod_ref,
                    band_ref, poolw_ref, pscale_ref, gnorm_ref, wo_ref, ln_g_ref, ln_b_ref,
                    x1_ref, stout_ref, st_scr, b_scr):
    j = N_TILES - 1 - pl.program_id(0)
    lat_idx = jnp.maximum(j - CTX_TILES, 0) % LAT_TILES
    is_ctx = j < CTX_TILES
    seq_first = is_ctx | (lat_idx == 0)
    seq_last = is_ctx | (lat_idx == LAT_TILES - 1)

    @pl.when(seq_last)
    def _():
        st_scr[...] = st0_ref[0]

    o_b = _gla_tile(q_ref[...], k_ref[...], v_ref[...], g_ref[...], tri_ref[...],
                    lmap_ref[...], st_scr, b_scr, rev=True)

    @pl.when(is_ctx)
    def _():
        stout_ref[0] = st_scr[...]

    xpool = xpool_ref[...]
    prev = jnp.where(seq_first, 0.0, prev_ref[...])
    nxt = jnp.where(seq_last, 0.0, next_ref[...])
    ext = jnp.concatenate([prev, xpool, nxt], axis=0)
    tpos = jnp.where(is_ctx, 0, lat_idx) * TILE + lax.broadcasted_iota(jnp.int32, (TILE, POOL_CH), 0)
    seq_len = jnp.where(is_ctx, CTX_LEN, LAT_LEN)
    pool_parts = []
    for gi, w in enumerate(POOL_WINDOWS):
        sl = slice(gi * POOL_CH, (gi + 1) * POOL_CH)
        wsum = _dot_exact(band_ref[gi], ext[:, sl])
        lo = jnp.maximum(tpos - w // 2, 0)
        hi = jnp.minimum(tpos - w // 2 + w, seq_len)
        dmean = wsum / (hi - lo).astype(F32) - xpool[:, sl]
        y = jnp.dot(dmean.astype(BF16), poolw_ref[gi], preferred_element_type=F32)
        pool_parts.append(y * pscale_ref[:, sl])

    o = of_ref[...] + o_b
    ogs = ogs_ref[...]
    for h in range(HEADS):
        sl = slice(h * DV, (h + 1) * DV)
        oh = o[:, sl]
        oh = oh * lax.rsqrt(jnp.mean(oh * oh, axis=-1, keepdims=True) + EPS)
        pool_parts.append(oh * gnorm_ref[:, sl] * ogs[:, sl])
    mix_in = jnp.concatenate(pool_parts, axis=1).astype(BF16)
    mix = jnp.dot(mix_in, wo_ref[...], preferred_element_type=F32)

    r = _cond_row(j)
    x = _load_x(j, xp_ref, xs_ref, pos_ref)
    y = ALPHA * x + _mod_rows(mod_ref, r, 2) * mix
    x1_ref[...] = _layer_norm(y, ln_g_ref[...], ln_b_ref[...])


def _gla_bwd_call(q, k, v, gk, st0, consts, o_f, xpool, ogs, xp2, xs2, pos, mod,
                  band, poolw, pscale, gnorm, wo, ln_g, ln_b):
    tri, lmap = consts
    rv = lambda i: N_TILES - 1 - i
    tok = lambda c: pl.BlockSpec((TILE, c), lambda i: (rv(i), 0))
    full = lambda a: pl.BlockSpec(a.shape, lambda i: (0,) * a.ndim)
    halo_blocks = TILE // POOL_HALO
    n_halo = N_TOK // POOL_HALO
    xspecs = [pl.BlockSpec((TILE, D_MODEL), lambda i: (jnp.minimum(rv(i), CTX_TILES - 1), 0)),
              pl.BlockSpec((TILE, D_MODEL), lambda i: (jnp.maximum(rv(i) - CTX_TILES, 0), 0)),
              pl.BlockSpec((TILE, D_MODEL), lambda i: (jnp.maximum(rv(i) - CTX_TILES, 0) % LAT_TILES, 0))]
    return pl.pallas_call(
        _gla_bwd_kernel,
        grid=(N_TILES,),
        in_specs=[tok(D_K), tok(D_K), tok(D_V), pl.BlockSpec((TILE, D_K), lambda i: (rv(i), 1)),
                  pl.BlockSpec((1, DV, D_K), lambda i: (_cond_row(rv(i)), 0, 0)),
                  full(tri), full(lmap),
                  tok(D_V), tok(D_POOL),
                  pl.BlockSpec((POOL_HALO, D_POOL), lambda i: (jnp.maximum(rv(i) * halo_blocks - 1, 0), 0)),
                  pl.BlockSpec((POOL_HALO, D_POOL),
                               lambda i: (jnp.minimum((rv(i) + 1) * halo_blocks, n_halo - 1), 0)),
                  tok(D_V)] + xspecs + [full(mod), full(band), full(poolw), full(pscale), full(gnorm),
                                        full(wo), full(ln_g), full(ln_b)],
        out_specs=[tok(D_MODEL),
                   pl.BlockSpec((1, DV, D_K), lambda i: (jnp.minimum(rv(i), CTX_TILES - 1), 0, 0))],
        out_shape=[jax.ShapeDtypeStruct((N_TOK, D_MODEL), F32),
                   jax.ShapeDtypeStruct((N_CTX_SEQ, DV, D_K), F32)],
        scratch_shapes=[pltpu.VMEM((DV, D_K), F32), pltpu.VMEM((TILE, D_K), F32)],
        compiler_params=_cparams(("arbitrary",)),
        name="gla_bwd_mix",
    )(q, k, v, gk, st0, tri, lmap, o_f, xpool, xpool, xpool, ogs, xp2, xs2, pos, mod,
      band, poolw, pscale, gnorm, wo, ln_g, ln_b)


def _top_values(x, n, with_rank=False):
    vals = []
    rank = jnp.full(x.shape, float(n), F32)
    for j in range(n):
        m = jnp.max(x, axis=0, keepdims=True)
        vals.append(m)
        hit = x == m
        if with_rank:
            rank = jnp.where(hit, float(j), rank)
        x = jnp.where(hit, -jnp.inf, x)
    return (vals, rank) if with_rank else vals


def _route_kernel(x1_ref, mod_ref, wqt_ref, keys_ref, u_ref, v_ref,
                  h2t_ref, cnt_ref, g1_ref, r2_ref, e2_ref, ubf_ref, vtbf_ref):
    i = pl.program_id(0)
    r = _cond_row(i)
    ubf_ref[...] = u_ref[...].astype(BF16)
    vtbf_ref[0] = v_ref[...].T.astype(BF16)
    h2 = x1_ref[...] * (1.0 + _mod_rows(mod_ref, r, 4)) + _mod_rows(mod_ref, r, 3)
    h2t = h2.T.astype(BF16)
    h2t_ref[...] = h2t
    qt = jnp.dot(wqt_ref[...], h2t, preferred_element_type=F32)
    for h in range(PEER_HEADS):
        s = []
        for p in range(2):
            row = (2 * h + p) * N_KEYS
            qhp = qt[row:row + N_KEYS, :].astype(BF16)
            s.append(jnp.dot(keys_ref[2 * h + p], qhp, preferred_element_type=F32))
        sv1, rank1 = _top_values(s[0], PEER_TOPK, with_rank=True)
        sv2, rank2 = _top_values(s[1], PEER_TOPK, with_rank=True)
        a1 = jnp.concatenate(sv1, axis=0)
        a2 = jnp.concatenate(sv2, axis=0)
        row8 = lax.broadcasted_iota(jnp.int32, (8, TILE), 0)
        cand = [a2 + sv1[0]]
        for a in range(1, 8):
            cand.append(jnp.where(row8 < PEER_TOPK // (a + 1), a2[0:8] + sv1[a], -jnp.inf))
        cand.append(a1[8:16] + sv2[0])
        fv = _top_values(jnp.concatenate(cand, axis=0), PEER_TOPK)
        denom = jnp.zeros_like(fv[0])
        for f in fv:
            denom = denom + jnp.exp(f - fv[0])
        thr = fv[PEER_TOPK - 1]
        cnt = jnp.zeros_like(rank1)
        for a in range(PEER_TOPK):
            blk = cand[a] if a < 8 else cand[8][a - 8:a - 7]
            n_a = jnp.sum((blk >= thr).astype(F32), axis=0, keepdims=True)
            cnt = jnp.where(rank1 == float(a), n_a, cnt)
        cnt_ref[h] = cnt
        g1_ref[h] = jnp.exp(s[0] - sv1[0]) / denom
        r2_ref[h] = rank2.astype(BF16)
        e2_ref[h] = jnp.exp(s[1] - sv2[0]).astype(BF16)


def _route_call(x1, mod, wqt, keys, u, v):
    full = lambda a: pl.BlockSpec(a.shape, lambda i: (0,) * a.ndim)
    hk = pl.BlockSpec((PEER_HEADS, N_KEYS, TILE), lambda i: (0, 0, i))
    per_key = lambda dt: jax.ShapeDtypeStruct((PEER_HEADS, N_KEYS, N_TOK), dt)
    slab = N_EXPERTS // N_TILES
    slabs_per_block = DENSE_CH // slab
    table = pl.BlockSpec((slab, D_MODEL), lambda i: (i, 0))
    return pl.pallas_call(
        _route_kernel,
        grid=(N_TILES,),
        in_specs=[pl.BlockSpec((TILE, D_MODEL), lambda i: (i, 0)), full(mod), full(wqt), full(keys),
                  table, table],
        out_specs=[pl.BlockSpec((D_MODEL, TILE), lambda i: (0, i)), hk, hk, hk, hk, table,
                   pl.BlockSpec((1, D_MODEL, slab), lambda i: (i // slabs_per_block, 0, i % slabs_per_block))],
        out_shape=[jax.ShapeDtypeStruct((D_MODEL, N_TOK), BF16),
                   per_key(F32), per_key(F32), per_key(BF16), per_key(BF16),
                   jax.ShapeDtypeStruct((N_EXPERTS, D_MODEL), BF16),
                   jax.ShapeDtypeStruct((N_EXPERTS // DENSE_CH, D_MODEL, DENSE_CH), BF16)],
        compiler_params=_cparams(("arbitrary",)),
        name="peer_route",
    )(x1, mod, wqt, keys, u, v)


def _gelu_tanh(x):
    c1 = math.sqrt(2.0 / math.pi)
    c2 = c1 * 0.044715
    return x * (0.5 + 0.5 * jnp.tanh(x * (c1 + c2 * (x * x))))


def _dense_kernel(h2t_ref, cnt_ref, g1_ref, r2_ref, e2_ref, u_ref, vt_ref, x1_ref, mod_ref,
                  ln_g_ref, ln_b_ref, yp_ref, ys_ref, acc_ref):
    tt = pl.program_id(0)
    c = pl.program_id(1)
    n_sub = DENSE_CH // N_KEYS

    @pl.when(c == 0)
    def _():
        acc_ref[...] = jnp.zeros_like(acc_ref)

    i1_base = pl.multiple_of(c * n_sub, n_sub)
    cnt8 = [cnt_ref[h, pl.ds(i1_base, n_sub), :].astype(BF16) for h in range(PEER_HEADS)]
    g18 = [g1_ref[h, pl.ds(i1_base, n_sub), :].astype(BF16) for h in range(PEER_HEADS)]
    def gate(ci):
        wsum = jnp.zeros((N_KEYS, DENSE_TM), BF16)
        for h in range(PEER_HEADS):
            sel = r2_ref[h] < cnt8[h][ci:ci + 1, :]
            wsum = wsum + jnp.where(sel, e2_ref[h], jnp.zeros((), BF16)) * g18[h][ci:ci + 1, :]
        return wsum

    sub_per_piece = n_sub // DENSE_PIECES
    gates, a_pieces = [], []
    for pc in range(DENSE_PIECES):
        rows = slice(pc * sub_per_piece * N_KEYS, (pc + 1) * sub_per_piece * N_KEYS)
        lhs = u_ref[rows, :]
        if pc > 0:
            zero = gates[-1][0:16, 0:128] * jnp.zeros((), BF16)
            top = jnp.concatenate([lhs[0:16, 0:128] + zero, lhs[0:16, 128:]], axis=1)
            lhs = jnp.concatenate([top, lhs[16:]], axis=0)
        a_pieces.append(jnp.dot(lhs, h2t_ref[...], preferred_element_type=F32))
        gates.extend(gate(ci) for ci in range(pc * sub_per_piece, (pc + 1) * sub_per_piece))
    a = jnp.concatenate(a_pieces, axis=0)
    w_parts = [_gelu_tanh(a[ci * N_KEYS:(ci + 1) * N_KEYS, :].astype(BF16)) * gates[ci] for ci in range(n_sub)]
    zero = gates[-1][0:16, 0:128] * jnp.zeros((), BF16)
    w0 = w_parts[0]
    top = jnp.concatenate([w0[0:16, 0:128] + zero, w0[0:16, 128:]], axis=1)
    w_parts[0] = jnp.concatenate([top, w0[16:]], axis=0)
    w = jnp.concatenate(w_parts, axis=0)
    acc_ref[...] += jnp.dot(vt_ref[0], w, preferred_element_type=F32)

    @pl.when(c == pl.num_programs(1) - 1)
    def _():
        r = _cond_row(tt * (DENSE_TM // TILE))
        ffn = acc_ref[...].T
        y = _layer_norm(ALPHA * x1_ref[...] + _mod_rows(mod_ref, r, 5) * ffn, ln_g_ref[...], ln_b_ref[...])

        @pl.when(tt < DENSE_CTX_TILES)
        def _():
            yp_ref[...] = y

        @pl.when(tt >= DENSE_CTX_TILES)
        def _():
            ys_ref[...] = y


def _dense_call(h2t, cnt, g1, r2, e2, u_bf, vt_bf, x1, mod, ln_g, ln_b):
    full = lambda a: pl.BlockSpec(a.shape, lambda t, c: (0,) * a.ndim)
    hk = pl.BlockSpec((PEER_HEADS, N_KEYS, DENSE_TM), lambda t, c: (0, 0, t))
    n_half = N_CTX_SEQ * CTX_LEN
    return pl.pallas_call(
        _dense_kernel,
        grid=(N_TOK // DENSE_TM, N_EXPERTS // DENSE_CH),
        in_specs=[pl.BlockSpec((D_MODEL, DENSE_TM), lambda t, c: (0, t)),
                  hk, hk, hk, hk,
                  pl.BlockSpec((DENSE_CH, D_MODEL), lambda t, c: (c, 0)),
                  pl.BlockSpec((1, D_MODEL, DENSE_CH), lambda t, c: (c, 0, 0)),
                  pl.BlockSpec((DENSE_TM, D_MODEL), lambda t, c: (t, 0)),
                  full(mod), full(ln_g), full(ln_b)],
        out_specs=[pl.BlockSpec((DENSE_TM, D_MODEL), lambda t, c: (jnp.minimum(t, DENSE_CTX_TILES - 1), 0)),
                   pl.BlockSpec((DENSE_TM, D_MODEL), lambda t, c: (jnp.maximum(t - DENSE_CTX_TILES, 0), 0))],
        out_shape=[jax.ShapeDtypeStruct((n_half, D_MODEL), F32),
                   jax.ShapeDtypeStruct((N_TOK - n_half, D_MODEL), F32)],
        scratch_shapes=[pltpu.VMEM((D_MODEL, DENSE_TM), F32)],
        compiler_params=_cparams(("arbitrary", "arbitrary")),
        name="peer_dense",
    )(h2t, cnt, g1, r2, e2, u_bf, vt_bf, x1, mod, ln_g, ln_b)


def _grid_pos_embed():
    rows = LAT_LEN // GRID_W
    r, col = np.meshgrid(np.arange(rows), np.arange(GRID_W), indexing="ij")

    def sincos(pos, dim):
        omega = 1.0 / (10000.0 ** (np.arange(dim // 2, dtype=np.float64) / (dim // 2)))
        ang = pos.reshape(-1).astype(np.float64)[:, None] * omega[None, :]
        return np.concatenate([np.sin(ang), np.cos(ang)], axis=-1)

    pe = np.concatenate([sincos(r, D_MODEL // 2), sincos(col, D_MODEL // 2)], axis=-1)
    return jnp.asarray(pe, F32)


def _state_pack_t(state):
    packed = jnp.transpose(state, (0, 3, 1, 2)).reshape(N_LAT_SEQ, DV, D_K)
    return jnp.concatenate([jnp.zeros((1, DV, D_K), F32), packed], axis=0)


def _state_unpack_t(st):
    return jnp.transpose(st.reshape(N_CTX_SEQ, DV, HEADS, DK), (0, 2, 3, 1))[:, None]


def kernel(x_prompt, x_sample, c, state_fwd, state_bwd, c_ctx, w_ada, b_ada, w_in, pool_w, pool_scale,
           gk_w, gk_b, gla_norm_g, w_o, ln1_g, ln1_b, peer_wq, peer_keys, peer_u, peer_v, ln2_g, ln2_b):
    xp2 = x_prompt.reshape(N_CTX_SEQ * CTX_LEN, D_MODEL)
    xs2 = x_sample.reshape(N_LAT_SEQ * LAT_LEN, D_MODEL)
    pos = _grid_pos_embed()

    cond8 = jnp.zeros((8, D_MODEL), F32).at[0].set(c_ctx).at[1:3].set(c)
    mod = _mod_call(cond8, w_ada[0], b_ada[0][None, :])

    w = w_in[0]
    w_in_r = jnp.concatenate([w[:, :1536], w[:, 1568:], w[:, 1536:1568],
                              jnp.zeros((D_MODEL, P_COLS - 2080), F32)], axis=1).astype(BF16)
    gkw_bd = jnp.zeros((128, 2 * D_K), F32)
    gkw_bd = gkw_bd.at[0:GATE_RANK, 0:D_K].set(gk_w[0, 0]).at[GATE_RANK:2 * GATE_RANK, D_K:].set(gk_w[0, 1])
    gkb = gk_b[0].reshape(1, 2 * D_K)
    xpool, q, k, v, gk, ogs = _proj_call(xp2, xs2, pos, mod, w_in_r, gkw_bd.astype(BF16), gkb)

    o_f, st_f, wqt = _gla_fwd_call(q, k, v, gk, _state_pack_t(state_fwd[:, 0]), _gla_constants(False),
                                   peer_wq[0])
    x1, st_b = _gla_bwd_call(
        q, k, v, gk, _state_pack_t(state_bwd[:, 0]), _gla_constants(True), o_f, xpool, ogs,
        xp2, xs2, pos, mod, _pool_band(), pool_w[0].astype(BF16), pool_scale[0][None, :],
        gla_norm_g[0][None, :], w_o[0].astype(BF16), ln1_g[0][None, :], ln1_b[0][None, :])

    keys = peer_keys[0].reshape(2 * PEER_HEADS, N_KEYS, N_KEYS).astype(BF16)
    h2t, cnt, g1, r2, e2, u_bf, vt_bf = _route_call(x1, mod, wqt, keys, peer_u[0], peer_v[0])
    yp, ys = _dense_call(h2t, cnt, g1, r2, e2, u_bf, vt_bf, x1, mod, ln2_g[0][None, :], ln2_b[0][None, :])

    y_prompt = yp.reshape(N_CTX_SEQ, CTX_LEN, D_MODEL)
    y_sample = ys.reshape(N_LAT_SEQ, LAT_LEN, D_MODEL)
    return (y_prompt, y_sample, _state_unpack_t(st_f), _state_unpack_t(st_b))
```

```python
import functools
import math

import numpy as np
import jax
import jax.numpy as jnp
from jax import lax
from jax.experimental import pallas as pl
from jax.experimental.pallas import tpu as pltpu

F32 = jnp.float32
BF16 = jnp.bfloat16

D_MODEL = 1024
N_CTX_SEQ, CTX_LEN = 16, 256
N_LAT_SEQ, LAT_LEN = 2, 2048
GRID_W = 64
N_TOK = N_CTX_SEQ * CTX_LEN + N_LAT_SEQ * LAT_LEN
TILE = 256
N_TILES = N_TOK // TILE
CTX_TILES = N_CTX_SEQ * CTX_LEN // TILE
LAT_TILES = LAT_LEN // TILE

D_POOL = 512
POOL_GROUPS = 4
POOL_CH = 128
POOL_WINDOWS = (2, 4, 8, 16)
POOL_HALO = 8
HEADS = 4
DK = 64
DV = 128
D_K = HEADS * DK
D_V = HEADS * DV
GATE_RANK = 16
GATE_NORM = 16.0
CHUNK = 64
N_LEVELS = 7
P_COLS = 2048 + 128

PEER_HEADS = 8
N_KEYS = 128
PEER_TOPK = 16
N_EXPERTS = N_KEYS * N_KEYS
WQ_SLABS = PEER_HEADS * 2 * N_KEYS // 128
DENSE_TM = 512
DENSE_CH = 2048
DENSE_CTX_TILES = N_CTX_SEQ * CTX_LEN // DENSE_TM
DENSE_PIECES = 16
ALPHA = 2.0 ** 0.25
EPS = 1e-5
VMEM_LIMIT = 56 * 1024 * 1024


def _cparams(sem, flags=None):
    return pltpu.CompilerParams(dimension_semantics=sem, vmem_limit_bytes=VMEM_LIMIT, flags=flags)


def _silu(x):
    return x * jax.nn.sigmoid(x)


def _split_bf16(x, pieces):
    out = []
    for _ in range(pieces - 1):
        hi = x.astype(BF16)
        out.append(hi)
        x = x - hi.astype(F32)
    out.append(x.astype(BF16))
    return out


def _dot_exact(m01, x, pieces=3):
    c = x.shape[1]
    xs = jnp.concatenate(_split_bf16(x, pieces), axis=1)
    y = jnp.dot(m01, xs, preferred_element_type=F32)
    out = y[:, :c]
    for p in range(1, pieces):
        out = out + y[:, p * c:(p + 1) * c]
    return out


def _layer_norm(y, g, b):
    mu = jnp.mean(y, axis=-1, keepdims=True)
    yc = y - mu
    var = jnp.mean(yc * yc, axis=-1, keepdims=True)
    return yc * lax.rsqrt(var + EPS) * g + b


def _cond_row(tile):
    return (tile >= CTX_TILES).astype(jnp.int32) + (tile >= CTX_TILES + LAT_TILES).astype(jnp.int32)


def _mod_kernel(cond_ref, w_ref, b_ref, out_ref):
    s = _silu(cond_ref[...]).astype(BF16)
    out_ref[...] = jnp.dot(s, w_ref[...].astype(BF16), preferred_element_type=F32) + b_ref[...]


def _mod_call(cond8, w_ada, b_ada):
    n = w_ada.shape[1]
    bn = 1536
    return pl.pallas_call(
        _mod_kernel,
        grid=(n // bn,),
        in_specs=[pl.BlockSpec((8, D_MODEL), lambda j: (0, 0)),
                  pl.BlockSpec((D_MODEL, bn), lambda j: (0, j)),
                  pl.BlockSpec((1, bn), lambda j: (0, j))],
        out_specs=pl.BlockSpec((8, bn), lambda j: (0, j)),
        out_shape=jax.ShapeDtypeStruct((8, n), F32),
        compiler_params=_cparams(("arbitrary",)),
        name="mod",
    )(cond8, w_ada, b_ada)


def _load_x(i, xp_ref, xs_ref, pos_ref):
    return jnp.where(i < CTX_TILES, xp_ref[...], xs_ref[...] + pos_ref[...])


def _mod_rows(mod_ref, r, k):
    return mod_ref[pl.ds(r, 1), k * D_MODEL:(k + 1) * D_MODEL]


def _proj_kernel(xp_ref, xs_ref, pos_ref, mod_ref, w_ref, gkw_ref, gkb_ref,
                 oxp, oq, ok, ov, ogk, oog):
    i = pl.program_id(0)
    r = _cond_row(i)
    x = _load_x(i, xp_ref, xs_ref, pos_ref)
    h = x * (1.0 + _mod_rows(mod_ref, r, 1)) + _mod_rows(mod_ref, r, 0)
    p = jnp.dot(h.astype(BF16), w_ref[...], preferred_element_type=F32)
    oxp[...] = p[:, 0:512]
    oq[...] = p[:, 512:768] * (DK ** -0.5)
    ok[...] = p[:, 768:1024]
    ov[...] = p[:, 1024:1536]
    oog[...] = _silu(p[:, 1536:2048])
    glr = p[:, 2048:2176]
    g2 = jnp.concatenate(_split_bf16(glr, 2), axis=1)
    pre = jnp.dot(g2, gkw_ref[...], preferred_element_type=F32) + gkb_ref[...]
    ogk[...] = (jnp.minimum(pre, 0.0) - jnp.log(1.0 + jnp.exp(-jnp.abs(pre)))) * (1.0 / GATE_NORM)


def _x_specs():
    return [pl.BlockSpec((TILE, D_MODEL), lambda i: (jnp.minimum(i, CTX_TILES - 1), 0)),
            pl.BlockSpec((TILE, D_MODEL), lambda i: (jnp.maximum(i - CTX_TILES, 0), 0)),
            pl.BlockSpec((TILE, D_MODEL), lambda i: (jnp.maximum(i - CTX_TILES, 0) % LAT_TILES, 0))]


def _proj_call(xp2, xs2, pos, mod, w_in_r, gkw_bd, gkb):
    tok = lambda c: pl.BlockSpec((TILE, c), lambda i: (i, 0))
    full = lambda a: pl.BlockSpec(a.shape, lambda i: (0,) * a.ndim)
    outs = [512, 256, 256, 512, 512, 512]
    return pl.pallas_call(
        _proj_kernel,
        grid=(N_TILES,),
        in_specs=_x_specs() + [full(mod), full(w_in_r), full(gkw_bd), full(gkb)],
        out_specs=[tok(c) for c in outs],
        out_shape=[jax.ShapeDtypeStruct((N_TOK, c), F32) for c in outs],
        compiler_params=_cparams(("arbitrary",)),
        name="proj",
    )(xp2, xs2, pos, mod, w_in_r, gkw_bd, gkb)


def _gla_constants(rev):
    t = np.arange(TILE)
    same_chunk = (t[:, None] // CHUNK) == (t[None, :] // CHUNK)
    tri = same_chunk & ((t[None, :] >= t[:, None]) if rev else (t[None, :] <= t[:, None]))
    u = np.arange(CHUNK)
    x = u[:, None] ^ u[None, :]
    lvl = np.where(x > 0, np.floor(np.log2(np.maximum(x, 1))), N_LEVELS - 1).astype(np.int32)
    causal = (u[:, None] <= u[None, :]) if rev else (u[:, None] >= u[None, :])
    lvl = np.where(causal, lvl, -1).astype(np.int32)
    return jnp.asarray(tri, BF16), jnp.asarray(np.tile(lvl, (1, HEADS)), jnp.int32)


def _level_ref(b_scr, level, rev):
    half = 1 << level
    sub = lax.broadcasted_iota(jnp.int32, (8, D_K), 0)
    row = lambda r: jnp.broadcast_to(b_scr[r:r + 1, :], (8, D_K))
    pieces = []
    for m in range(TILE // 8):
        refs = [((8 * m + u) // (2 * half)) * (2 * half) + (half if rev else half - 1) for u in range(8)]
        piece = row(refs[0])
        for u in range(1, 8):
            if refs[u] != refs[u - 1]:
                piece = jnp.where(sub >= u, row(refs[u]), piece)
        pieces.append(piece)
    return jnp.concatenate(pieces, axis=0)


def _head_blocks(x, width):
    head = lax.broadcasted_iota(jnp.int32, x.shape, 1) // width
    return jnp.concatenate([jnp.where(head == h, x, jnp.zeros_like(x)) for h in range(HEADS)], axis=0)


def _gla_tile(q, k, v, g, tri, lmap, st_ref, b_scr, rev):
    b = _dot_exact(tri, g)
    b_scr[...] = b
    qds, kds = [], []
    for l in range(N_LEVELS - 1):
        d = b - _level_ref(b_scr, l, rev)
        qds.append((q * jnp.exp(jnp.minimum(d, 0.0))).astype(BF16))
        kds.append((k * jnp.exp(jnp.minimum(-d, 0.0))).astype(BF16))
    qds.append(q.astype(BF16))
    kds.append(k.astype(BF16))
    qe = (q * jnp.exp(b)).astype(BF16)
    vb = v.astype(BF16)

    cdim = (((1,), (1,)), ((), ()))
    st_head = lax.broadcasted_iota(jnp.int32, (DV, D_K), 1) // DK
    st = st_ref[...]
    outs = [None] * (TILE // CHUNK)
    for c in (reversed(range(TILE // CHUNK)) if rev else range(TILE // CHUNK)):
        rows = slice(c * CHUNK, (c + 1) * CHUNK)
        o = lax.dot_general(qe[rows], _head_blocks(st.astype(BF16), DK), cdim,
                            preferred_element_type=F32)
        p = jnp.zeros((CHUNK, HEADS * CHUNK), F32)
        for l in range(N_LEVELS):
            z = lax.dot_general(qds[l][rows], _head_blocks(kds[l][rows], DK), cdim,
                                preferred_element_type=F32)
            p = jnp.where(lmap == l, z, p)
        outs[c] = o + jnp.dot(p.astype(BF16), _head_blocks(vb[rows], DV), preferred_element_type=F32)

        b_c = b[rows]
        b_end = b_c[0:1, :] if rev else b_c[CHUNK - 1:CHUNK, :]
        kdec = (k[rows] * jnp.exp(b_end - b_c)).astype(BF16)
        kv = lax.dot_general(vb[rows], kdec, (((0,), (0,)), ((), ())), preferred_element_type=F32)
        kvd = kv[0:DV]
        for h in range(1, HEADS):
            kvd = jnp.where(st_head == h, kv[h * DV:(h + 1) * DV], kvd)
        st = st * jnp.exp(b_end) + kvd
    st_ref[...] = st
    return jnp.concatenate(outs, axis=0)


def _gla_fwd_kernel(q_ref, k_ref, v_ref, g_ref, st0_ref, tri_ref, lmap_ref, wq_ref,
                    of_ref, stout_ref, wqt_ref, st_scr, b_scr):
    i = pl.program_id(0)
    seq_start = (i <= CTX_TILES) | (i == CTX_TILES + LAT_TILES)

    @pl.when(seq_start)
    def _():
        st_scr[...] = st0_ref[0]

    @pl.when(i < WQ_SLABS)
    def _():
        wqt_ref[...] = wq_ref[...].T.astype(BF16)

    of_ref[...] = _gla_tile(q_ref[...], k_ref[...], v_ref[...], g_ref[...], tri_ref[...],
                            lmap_ref[...], st_scr, b_scr, rev=False)

    @pl.when(i < CTX_TILES)
    def _():
        stout_ref[0] = st_scr[...]


def _gla_fwd_call(q, k, v, gk, st0, consts, wq):
    tri, lmap = consts
    tok = lambda c: pl.BlockSpec((TILE, c), lambda i: (i, 0))
    full = lambda a: pl.BlockSpec(a.shape, lambda i: (0,) * a.ndim)
    slab = lambda i: jnp.minimum(i, WQ_SLABS - 1)
    return pl.pallas_call(
        _gla_fwd_kernel,
        grid=(N_TILES,),
        in_specs=[tok(D_K), tok(D_K), tok(D_V), pl.BlockSpec((TILE, D_K), lambda i: (i, 0)),
                  pl.BlockSpec((1, DV, D_K), lambda i: (_cond_row(i), 0, 0)),
                  full(tri), full(lmap),
                  pl.BlockSpec((D_MODEL, 128), lambda i: (0, slab(i)))],
        out_specs=[tok(D_V),
                   pl.BlockSpec((1, DV, D_K), lambda i: (jnp.minimum(i, CTX_TILES - 1), 0, 0)),
                   pl.BlockSpec((128, D_MODEL), lambda i: (slab(i), 0))],
        out_shape=[jax.ShapeDtypeStruct((N_TOK, D_V), F32),
                   jax.ShapeDtypeStruct((N_CTX_SEQ, DV, D_K), F32),
                   jax.ShapeDtypeStruct((WQ_SLABS * 128, D_MODEL), BF16)],
        scratch_shapes=[pltpu.VMEM((DV, D_K), F32), pltpu.VMEM((TILE, D_K), F32)],
        compiler_params=_cparams(("arbitrary",)),
        name="gla_fwd",
    )(q, k, v, gk, st0, tri, lmap, wq)


def _pool_band():
    t = np.arange(TILE)[:, None]
    e = np.arange(TILE + 2 * POOL_HALO)[None, :]
    bands = []
    for w in POOL_WINDOWS:
        lo = t + POOL_HALO - w // 2
        bands.append(((e >= lo) & (e < lo + w)).astype(np.float32))
    return jnp.asarray(np.stack(bands), BF16)


def _gla_bwd_kernel(q_ref, k_ref, v_ref, g_ref, st0_ref, tri_ref, lmap_ref,
                    of_ref, xpool_ref, prev_ref, next_ref, ogs_ref, xp_ref, xs_ref, pos_ref, mod_ref,
                    band_ref, poolw_ref, pscale_ref, gnorm_ref, wo_ref, ln_g_ref, ln_b_ref,
                    x1_ref, stout_ref, st_scr, b_scr):
    j = N_TILES - 1 - pl.program_id(0)
    lat_idx = jnp.maximum(j - CTX_TILES, 0) % LAT_TILES
    is_ctx = j < CTX_TILES
    seq_first = is_ctx | (lat_idx == 0)
    seq_last = is_ctx | (lat_idx == LAT_TILES - 1)

    @pl.when(seq_last)
    def _():
        st_scr[...] = st0_ref[0]

    o_b = _gla_tile(q_ref[...], k_ref[...], v_ref[...], g_ref[...], tri_ref[...],
                    lmap_ref[...], st_scr, b_scr, rev=True)

    @pl.when(is_ctx)
    def _():
        stout_ref[0] = st_scr[...]

    xpool = xpool_ref[...]
    prev = jnp.where(seq_first, 0.0, prev_ref[...])
    nxt = jnp.where(seq_last, 0.0, next_ref[...])
    ext = jnp.concatenate([prev, xpool, nxt], axis=0)
    tpos = jnp.where(is_ctx, 0, lat_idx) * TILE + lax.broadcasted_iota(jnp.int32, (TILE, POOL_CH), 0)
    seq_len = jnp.where(is_ctx, CTX_LEN, LAT_LEN)
    pool_parts = []
    for gi, w in enumerate(POOL_WINDOWS):
        sl = slice(gi * POOL_CH, (gi + 1) * POOL_CH)
        wsum = _dot_exact(band_ref[gi], ext[:, sl], pieces=2)
        lo = jnp.maximum(tpos - w // 2, 0)
        hi = jnp.minimum(tpos - w // 2 + w, seq_len)
        dmean = wsum / (hi - lo).astype(F32) - xpool[:, sl]
        y = jnp.dot(dmean.astype(BF16), poolw_ref[gi], preferred_element_type=F32)
        pool_parts.append(y * pscale_ref[:, sl])

    o = of_ref[...] + o_b
    ogs = ogs_ref[...]
    for h in range(HEADS):
        sl = slice(h * DV, (h + 1) * DV)
        oh = o[:, sl]
        oh = oh * lax.rsqrt(jnp.mean(oh * oh, axis=-1, keepdims=True) + EPS)
        pool_parts.append(oh * gnorm_ref[:, sl] * ogs[:, sl])
    mix_in = jnp.concatenate(pool_parts, axis=1).astype(BF16)
    mix = jnp.dot(mix_in, wo_ref[...], preferred_element_type=F32)

    r = _cond_row(j)
    x = _load_x(j, xp_ref, xs_ref, pos_ref)
    y = ALPHA * x + _mod_rows(mod_ref, r, 2) * mix
    x1_ref[...] = _layer_norm(y, ln_g_ref[...], ln_b_ref[...])


def _gla_bwd_call(q, k, v, gk, st0, consts, o_f, xpool, ogs, xp2, xs2, pos, mod,
                  band, poolw, pscale, gnorm, wo, ln_g, ln_b):
    tri, lmap = consts
    rv = lambda i: N_TILES - 1 - i
    tok = lambda c: pl.BlockSpec((TILE, c), lambda i: (rv(i), 0))
    full = lambda a: pl.BlockSpec(a.shape, lambda i: (0,) * a.ndim)
    halo_blocks = TILE // POOL_HALO
    n_halo = N_TOK // POOL_HALO
    xspecs = [pl.BlockSpec((TILE, D_MODEL), lambda i: (jnp.minimum(rv(i), CTX_TILES - 1), 0)),
              pl.BlockSpec((TILE, D_MODEL), lambda i: (jnp.maximum(rv(i) - CTX_TILES, 0), 0)),
              pl.BlockSpec((TILE, D_MODEL), lambda i: (jnp.maximum(rv(i) - CTX_TILES, 0) % LAT_TILES, 0))]
    return pl.pallas_call(
        _gla_bwd_kernel,
        grid=(N_TILES,),
        in_specs=[tok(D_K), tok(D_K), tok(D_V), pl.BlockSpec((TILE, D_K), lambda i: (rv(i), 1)),
                  pl.BlockSpec((1, DV, D_K), lambda i: (_cond_row(rv(i)), 0, 0)),
                  full(tri), full(lmap),
                  tok(D_V), tok(D_POOL),
                  pl.BlockSpec((POOL_HALO, D_POOL), lambda i: (jnp.maximum(rv(i) * halo_blocks - 1, 0), 0)),
                  pl.BlockSpec((POOL_HALO, D_POOL),
                               lambda i: (jnp.minimum((rv(i) + 1) * halo_blocks, n_halo - 1), 0)),
                  tok(D_V)] + xspecs + [full(mod), full(band), full(poolw), full(pscale), full(gnorm),
                                        full(wo), full(ln_g), full(ln_b)],
        out_specs=[tok(D_MODEL),
                   pl.BlockSpec((1, DV, D_K), lambda i: (jnp.minimum(rv(i), CTX_TILES - 1), 0, 0))],
        out_shape=[jax.ShapeDtypeStruct((N_TOK, D_MODEL), F32),
                   jax.ShapeDtypeStruct((N_CTX_SEQ, DV, D_K), F32)],
        scratch_shapes=[pltpu.VMEM((DV, D_K), F32), pltpu.VMEM((TILE, D_K), F32)],
        compiler_params=_cparams(("arbitrary",)),
        name="gla_bwd_mix",
    )(q, k, v, gk, st0, tri, lmap, o_f, xpool, xpool, xpool, ogs, xp2, xs2, pos, mod,
      band, poolw, pscale, gnorm, wo, ln_g, ln_b)


def _top_values(x, n, with_rank=False):
    vals = []
    rank = jnp.full(x.shape, float(n), F32)
    for j in range(n):
        m = jnp.max(x, axis=0, keepdims=True)
        vals.append(m)
        hit = x == m
        if with_rank:
            rank = jnp.where(hit, float(j), rank)
        x = jnp.where(hit, -jnp.inf, x)
    return (vals, rank) if with_rank else vals


def _route_kernel(x1_ref, mod_ref, wqt_ref, keys_ref, u_ref, v_ref,
                  h2t_ref, cnt_ref, g1_ref, r2_ref, e2_ref, ubf_ref, vtbf_ref):
    i = pl.program_id(0)
    r = _cond_row(i)
    ubf_ref[...] = u_ref[...].astype(BF16)
    vtbf_ref[0] = v_ref[...].T.astype(BF16)
    h2 = x1_ref[...] * (1.0 + _mod_rows(mod_ref, r, 4)) + _mod_rows(mod_ref, r, 3)
    h2t = h2.T.astype(BF16)
    h2t_ref[...] = h2t
    qt = jnp.dot(wqt_ref[...], h2t, preferred_element_type=F32)
    for h in range(PEER_HEADS):
        s = []
        for p in range(2):
            row = (2 * h + p) * N_KEYS
            qhp = qt[row:row + N_KEYS, :].astype(BF16)
            s.append(jnp.dot(keys_ref[2 * h + p], qhp, preferred_element_type=F32))
        sv1, rank1 = _top_values(s[0], PEER_TOPK, with_rank=True)
        sv2, rank2 = _top_values(s[1], PEER_TOPK, with_rank=True)
        a1 = jnp.concatenate(sv1, axis=0)
        a2 = jnp.concatenate(sv2, axis=0)
        row8 = lax.broadcasted_iota(jnp.int32, (8, TILE), 0)
        cand = [a2 + sv1[0]]
        for a in range(1, 8):
            cand.append(jnp.where(row8 < PEER_TOPK // (a + 1), a2[0:8] + sv1[a], -jnp.inf))
        cand.append(a1[8:16] + sv2[0])
        fv = _top_values(jnp.concatenate(cand, axis=0), PEER_TOPK)
        denom = jnp.zeros_like(fv[0])
        for f in fv:
            denom = denom + jnp.exp(f - fv[0])
        thr = fv[PEER_TOPK - 1]
        cnt = jnp.zeros_like(rank1)
        for a in range(PEER_TOPK):
            blk = cand[a] if a < 8 else cand[8][a - 8:a - 7]
            n_a = jnp.sum((blk >= thr).astype(F32), axis=0, keepdims=True)
            cnt = jnp.where(rank1 == float(a), n_a, cnt)
        cnt_ref[h] = cnt
        g1_ref[h] = jnp.exp(s[0] - sv1[0]) / denom
        r2_ref[h] = rank2.astype(BF16)
        e2_ref[h] = jnp.exp(s[1] - sv2[0]).astype(BF16)


def _route_call(x1, mod, wqt, keys, u, v):
    full = lambda a: pl.BlockSpec(a.shape, lambda i: (0,) * a.ndim)
    hk = pl.BlockSpec((PEER_HEADS, N_KEYS, TILE), lambda i: (0, 0, i))
    per_key = lambda dt: jax.ShapeDtypeStruct((PEER_HEADS, N_KEYS, N_TOK), dt)
    slab = N_EXPERTS // N_TILES
    slabs_per_block = DENSE_CH // slab
    table = pl.BlockSpec((slab, D_MODEL), lambda i: (i, 0))
    return pl.pallas_call(
        _route_kernel,
        grid=(N_TILES,),
        in_specs=[pl.BlockSpec((TILE, D_MODEL), lambda i: (i, 0)), full(mod), full(wqt), full(keys),
                  table, table],
        out_specs=[pl.BlockSpec((D_MODEL, TILE), lambda i: (0, i)), hk, hk, hk, hk, table,
                   pl.BlockSpec((1, D_MODEL, slab), lambda i: (i // slabs_per_block, 0, i % slabs_per_block))],
        out_shape=[jax.ShapeDtypeStruct((D_MODEL, N_TOK), BF16),
                   per_key(F32), per_key(F32), per_key(BF16), per_key(BF16),
                   jax.ShapeDtypeStruct((N_EXPERTS, D_MODEL), BF16),
                   jax.ShapeDtypeStruct((N_EXPERTS // DENSE_CH, D_MODEL, DENSE_CH), BF16)],
        compiler_params=_cparams(("arbitrary",)),
        name="peer_route",
    )(x1, mod, wqt, keys, u, v)


def _gelu_tanh(x):
    c1 = math.sqrt(2.0 / math.pi)
    c2 = c1 * 0.044715
    return x * (0.5 + 0.5 * jnp.tanh(x * (c1 + c2 * (x * x))))


def _dense_kernel(h2t_ref, cnt_ref, g1_ref, r2_ref, e2_ref, u_ref, vt_ref, x1_ref, mod_ref,
                  ln_g_ref, ln_b_ref, yp_ref, ys_ref, acc_ref):
    tt = pl.program_id(0)
    c = pl.program_id(1)
    n_sub = DENSE_CH // N_KEYS

    @pl.when(c == 0)
    def _():
        acc_ref[...] = jnp.zeros_like(acc_ref)

    i1_base = pl.multiple_of(c * n_sub, n_sub)
    cnt8 = [cnt_ref[h, pl.ds(i1_base, n_sub), :].astype(BF16) for h in range(PEER_HEADS)]
    g18 = [g1_ref[h, pl.ds(i1_base, n_sub), :].astype(BF16) for h in range(PEER_HEADS)]
    def gate(ci):
        wsum = jnp.zeros((N_KEYS, DENSE_TM), BF16)
        for h in range(PEER_HEADS):
            sel = r2_ref[h] < cnt8[h][ci:ci + 1, :]
            wsum = wsum + jnp.where(sel, e2_ref[h], jnp.zeros((), BF16)) * g18[h][ci:ci + 1, :]
        return wsum

    sub_per_piece = n_sub // DENSE_PIECES
    gates, a_pieces = [], []
    for pc in range(DENSE_PIECES):
        rows = slice(pc * sub_per_piece * N_KEYS, (pc + 1) * sub_per_piece * N_KEYS)
        lhs = u_ref[rows, :]
        if pc > 0:
            zero = gates[-1][0:16, 0:128] * jnp.zeros((), BF16)
            top = jnp.concatenate([lhs[0:16, 0:128] + zero, lhs[0:16, 128:]], axis=1)
            lhs = jnp.concatenate([top, lhs[16:]], axis=0)
        a_pieces.append(jnp.dot(lhs, h2t_ref[...], preferred_element_type=F32))
        gates.extend(gate(ci) for ci in range(pc * sub_per_piece, (pc + 1) * sub_per_piece))
    a = jnp.concatenate(a_pieces, axis=0)
    w_parts = [_gelu_tanh(a[ci * N_KEYS:(ci + 1) * N_KEYS, :].astype(BF16)) * gates[ci] for ci in range(n_sub)]
    zero = gates[-1][0:16, 0:128] * jnp.zeros((), BF16)
    w0 = w_parts[0]
    top = jnp.concatenate([w0[0:16, 0:128] + zero, w0[0:16, 128:]], axis=1)
    w_parts[0] = jnp.concatenate([top, w0[16:]], axis=0)
    w = jnp.concatenate(w_parts, axis=0)
    acc_ref[...] += jnp.dot(vt_ref[0], w, preferred_element_type=F32)

    @pl.when(c == pl.num_programs(1) - 1)
    def _():
        r = _cond_row(tt * (DENSE_TM // TILE))
        ffn = acc_ref[...].T
        y = _layer_norm(ALPHA * x1_ref[...] + _mod_rows(mod_ref, r, 5) * ffn, ln_g_ref[...], ln_b_ref[...])

        @pl.when(tt < DENSE_CTX_TILES)
        def _():
            yp_ref[...] = y

        @pl.when(tt >= DENSE_CTX_TILES)
        def _():
            ys_ref[...] = y


def _dense_call(h2t, cnt, g1, r2, e2, u_bf, vt_bf, x1, mod, ln_g, ln_b):
    full = lambda a: pl.BlockSpec(a.shape, lambda t, c: (0,) * a.ndim)
    hk = pl.BlockSpec((PEER_HEADS, N_KEYS, DENSE_TM), lambda t, c: (0, 0, t))
    n_half = N_CTX_SEQ * CTX_LEN
    return pl.pallas_call(
        _dense_kernel,
        grid=(N_TOK // DENSE_TM, N_EXPERTS // DENSE_CH),
        in_specs=[pl.BlockSpec((D_MODEL, DENSE_TM), lambda t, c: (0, t)),
                  hk, hk, hk, hk,
                  pl.BlockSpec((DENSE_CH, D_MODEL), lambda t, c: (c, 0)),
                  pl.BlockSpec((1, D_MODEL, DENSE_CH), lambda t, c: (c, 0, 0)),
                  pl.BlockSpec((DENSE_TM, D_MODEL), lambda t, c: (t, 0)),
                  full(mod), full(ln_g), full(ln_b)],
        out_specs=[pl.BlockSpec((DENSE_TM, D_MODEL), lambda t, c: (jnp.minimum(t, DENSE_CTX_TILES - 1), 0)),
                   pl.BlockSpec((DENSE_TM, D_MODEL), lambda t, c: (jnp.maximum(t - DENSE_CTX_TILES, 0), 0))],
        out_shape=[jax.ShapeDtypeStruct((n_half, D_MODEL), F32),
                   jax.ShapeDtypeStruct((N_TOK - n_half, D_MODEL), F32)],
        scratch_shapes=[pltpu.VMEM((D_MODEL, DENSE_TM), F32)],
        compiler_params=_cparams(("arbitrary", "arbitrary")),
        name="peer_dense",
    )(h2t, cnt, g1, r2, e2, u_bf, vt_bf, x1, mod, ln_g, ln_b)


def _grid_pos_embed():
    rows = LAT_LEN // GRID_W
    r, col = np.meshgrid(np.arange(rows), np.arange(GRID_W), indexing="ij")

    def sincos(pos, dim):
        omega = 1.0 / (10000.0 ** (np.arange(dim // 2, dtype=np.float64) / (dim // 2)))
        ang = pos.reshape(-1).astype(np.float64)[:, None] * omega[None, :]
        return np.concatenate([np.sin(ang), np.cos(ang)], axis=-1)

    pe = np.concatenate([sincos(r, D_MODEL // 2), sincos(col, D_MODEL // 2)], axis=-1)
    return jnp.asarray(pe, F32)


def _state_pack_t(state):
    packed = jnp.transpose(state, (0, 3, 1, 2)).reshape(N_LAT_SEQ, DV, D_K)
    return jnp.concatenate([jnp.zeros((1, DV, D_K), F32), packed], axis=0)


def _state_unpack_t(st):
    return jnp.transpose(st.reshape(N_CTX_SEQ, DV, HEADS, DK), (0, 2, 3, 1))[:, None]


def kernel(x_prompt, x_sample, c, state_fwd, state_bwd, c_ctx, w_ada, b_ada, w_in, pool_w, pool_scale,
           gk_w, gk_b, gla_norm_g, w_o, ln1_g, ln1_b, peer_wq, peer_keys, peer_u, peer_v, ln2_g, ln2_b):
    xp2 = x_prompt.reshape(N_CTX_SEQ * CTX_LEN, D_MODEL)
    xs2 = x_sample.reshape(N_LAT_SEQ * LAT_LEN, D_MODEL)
    pos = _grid_pos_embed()

    cond8 = jnp.zeros((8, D_MODEL), F32).at[0].set(c_ctx).at[1:3].set(c)
    mod = _mod_call(cond8, w_ada[0], b_ada[0][None, :])

    w = w_in[0]
    w_in_r = jnp.concatenate([w[:, :1536], w[:, 1568:], w[:, 1536:1568],
                              jnp.zeros((D_MODEL, P_COLS - 2080), F32)], axis=1).astype(BF16)
    gkw_bd = jnp.zeros((128, 2 * D_K), F32)
    gkw_bd = gkw_bd.at[0:GATE_RANK, 0:D_K].set(gk_w[0, 0]).at[GATE_RANK:2 * GATE_RANK, D_K:].set(gk_w[0, 1])
    gkb = gk_b[0].reshape(1, 2 * D_K)
    gkw2 = jnp.concatenate([gkw_bd, gkw_bd], axis=0).astype(BF16)
    xpool, q, k, v, gk, ogs = _proj_call(xp2, xs2, pos, mod, w_in_r, gkw2, gkb)

    o_f, st_f, wqt = _gla_fwd_call(q, k, v, gk, _state_pack_t(state_fwd[:, 0]), _gla_constants(False),
                                   peer_wq[0])
    x1, st_b = _gla_bwd_call(
        q, k, v, gk, _state_pack_t(state_bwd[:, 0]), _gla_constants(True), o_f, xpool, ogs,
        xp2, xs2, pos, mod, _pool_band(), pool_w[0].astype(BF16), pool_scale[0][None, :],
        gla_norm_g[0][None, :], w_o[0].astype(BF16), ln1_g[0][None, :], ln1_b[0][None, :])

    keys = peer_keys[0].reshape(2 * PEER_HEADS, N_KEYS, N_KEYS).astype(BF16)
    h2t, cnt, g1, r2, e2, u_bf, vt_bf = _route_call(x1, mod, wqt, keys, peer_u[0], peer_v[0])
    yp, ys = _dense_call(h2t, cnt, g1, r2, e2, u_bf, vt_bf, x1, mod, ln2_g[0][None, :], ln2_b[0][None, :])

    y_prompt = yp.reshape(N_CTX_SEQ, CTX_LEN, D_MODEL)
    y_sample = ys.reshape(N_LAT_SEQ, LAT_LEN, D_MODEL)
    return (y_prompt, y_sample, _state_unpack_t(st_f), _state_unpack_t(st_b))
```

```python
import math

import numpy as np
import jax
import jax.numpy as jnp
from jax import lax
from jax.experimental import pallas as pl
from jax.experimental.pallas import tpu as pltpu

F32 = jnp.float32
BF16 = jnp.bfloat16

D_MODEL = 1024
N_CTX_SEQ, CTX_LEN = 16, 256
N_LAT_SEQ, LAT_LEN = 2, 2048
GRID_W = 64
N_TOK = N_CTX_SEQ * CTX_LEN + N_LAT_SEQ * LAT_LEN
TILE = 256
N_TILES = N_TOK // TILE
CTX_TILES = N_CTX_SEQ * CTX_LEN // TILE
LAT_TILES = LAT_LEN // TILE

D_POOL = 512
POOL_GROUPS = 4
POOL_CH = 128
POOL_WINDOWS = (2, 4, 8, 16)
POOL_HALO = 8
HEADS = 4
DK = 64
DV = 128
D_K = HEADS * DK
D_V = HEADS * DV
GATE_RANK = 16
GATE_NORM = 16.0
CHUNK = 64
N_LEVELS = 7
P_COLS = 2048 + 128

PEER_HEADS = 8
N_KEYS = 128
PEER_TOPK = 16
N_EXPERTS = N_KEYS * N_KEYS
WQ_SLABS = PEER_HEADS * 2 * N_KEYS // 128
DENSE_TM = 512
DENSE_CH = 2048
DENSE_CTX_TILES = N_CTX_SEQ * CTX_LEN // DENSE_TM
DENSE_PIECES = 16
MOD_COLS = 1536
ALPHA = 2.0 ** 0.25
EPS = 1e-5
V7X_VMEM_BYTES = 64 * 1024 * 1024
VMEM_LIMIT = V7X_VMEM_BYTES * 7 // 8


def _cparams(sem):
    return pltpu.CompilerParams(dimension_semantics=sem, vmem_limit_bytes=VMEM_LIMIT)


def _silu(x):
    return x * jax.nn.sigmoid(x)


def _split_bf16(x, pieces):
    out = []
    for _ in range(pieces - 1):
        hi = x.astype(BF16)
        out.append(hi)
        x = x - hi.astype(F32)
    out.append(x.astype(BF16))
    return out


def _dot_exact(m01, x, pieces=3):
    c = x.shape[1]
    xs = jnp.concatenate(_split_bf16(x, pieces), axis=1)
    y = jnp.dot(m01, xs, preferred_element_type=F32)
    out = y[:, :c]
    for p in range(1, pieces):
        out = out + y[:, p * c:(p + 1) * c]
    return out


def _layer_norm(y, g, b):
    mu = jnp.mean(y, axis=-1, keepdims=True)
    yc = y - mu
    var = jnp.mean(yc * yc, axis=-1, keepdims=True)
    return yc * lax.rsqrt(var + EPS) * g + b


def _cond_row(tile):
    return (tile >= CTX_TILES).astype(jnp.int32) + (tile >= CTX_TILES + LAT_TILES).astype(jnp.int32)


def _mod_kernel(cond_ref, w_ref, b_ref, out_ref):
    s = _silu(cond_ref[...]).astype(BF16)
    out_ref[...] = jnp.dot(s, w_ref[...].astype(BF16), preferred_element_type=F32) + b_ref[...]


def _mod_call(cond8, w_ada, b_ada):
    n = w_ada.shape[1]
    bn = MOD_COLS
    return pl.pallas_call(
        _mod_kernel,
        grid=(n // bn,),
        in_specs=[pl.BlockSpec((8, D_MODEL), lambda j: (0, 0)),
                  pl.BlockSpec((D_MODEL, bn), lambda j: (0, j)),
                  pl.BlockSpec((1, bn), lambda j: (0, j))],
        out_specs=pl.BlockSpec((8, bn), lambda j: (0, j)),
        out_shape=jax.ShapeDtypeStruct((8, n), F32),
        compiler_params=_cparams(("arbitrary",)),
        name="mod",
    )(cond8, w_ada, b_ada)


def _load_x(i, xp_ref, xs_ref, pos_ref):
    return jnp.where(i < CTX_TILES, xp_ref[...], xs_ref[...] + pos_ref[...])


def _mod_rows(mod_ref, r, k):
    return mod_ref[pl.ds(r, 1), k * D_MODEL:(k + 1) * D_MODEL]


def _proj_kernel(xp_ref, xs_ref, pos_ref, mod_ref, w_ref, gkw_ref, gkb_ref,
                 oxp, oq, ok, ov, ogk, oog):
    i = pl.program_id(0)
    r = _cond_row(i)
    x = _load_x(i, xp_ref, xs_ref, pos_ref)
    h = x * (1.0 + _mod_rows(mod_ref, r, 1)) + _mod_rows(mod_ref, r, 0)
    p = jnp.dot(h.astype(BF16), w_ref[...], preferred_element_type=F32)
    oxp[...] = p[:, 0:512]
    oq[...] = p[:, 512:768] * (DK ** -0.5)
    ok[...] = p[:, 768:1024]
    ov[...] = p[:, 1024:1536]
    oog[...] = _silu(p[:, 1536:2048])
    glr = p[:, 2048:2176]
    g2 = jnp.concatenate(_split_bf16(glr, 2), axis=1)
    pre = jnp.dot(g2, gkw_ref[...], preferred_element_type=F32) + gkb_ref[...]
    ogk[...] = (jnp.minimum(pre, 0.0) - jnp.log(1.0 + jnp.exp(-jnp.abs(pre)))) * (1.0 / GATE_NORM)


def _x_specs():
    return [pl.BlockSpec((TILE, D_MODEL), lambda i: (jnp.minimum(i, CTX_TILES - 1), 0)),
            pl.BlockSpec((TILE, D_MODEL), lambda i: (jnp.maximum(i - CTX_TILES, 0), 0)),
            pl.BlockSpec((TILE, D_MODEL), lambda i: (jnp.maximum(i - CTX_TILES, 0) % LAT_TILES, 0))]


def _proj_call(xp2, xs2, pos, mod, w_in_r, gkw_bd, gkb):
    tok = lambda c: pl.BlockSpec((TILE, c), lambda i: (i, 0))
    full = lambda a: pl.BlockSpec(a.shape, lambda i: (0,) * a.ndim)
    outs = [512, 256, 256, 512, 512, 512]
    return pl.pallas_call(
        _proj_kernel,
        grid=(N_TILES,),
        in_specs=_x_specs() + [full(mod), full(w_in_r), full(gkw_bd), full(gkb)],
        out_specs=[tok(c) for c in outs],
        out_shape=[jax.ShapeDtypeStruct((N_TOK, c), F32) for c in outs],
        compiler_params=_cparams(("arbitrary",)),
        name="proj",
    )(xp2, xs2, pos, mod, w_in_r, gkw_bd, gkb)


def _gla_constants(rev):
    t = np.arange(TILE)
    same_chunk = (t[:, None] // CHUNK) == (t[None, :] // CHUNK)
    tri = same_chunk & ((t[None, :] >= t[:, None]) if rev else (t[None, :] <= t[:, None]))
    u = np.arange(CHUNK)
    x = u[:, None] ^ u[None, :]
    lvl = np.where(x > 0, np.floor(np.log2(np.maximum(x, 1))), N_LEVELS - 1).astype(np.int32)
    causal = (u[:, None] <= u[None, :]) if rev else (u[:, None] >= u[None, :])
    lvl = np.where(causal, lvl, -1).astype(np.int32)
    return jnp.asarray(tri, BF16), jnp.asarray(np.tile(lvl, (1, HEADS)), jnp.int32)


def _level_ref(b_scr, level, rev):
    half = 1 << level
    sub = lax.broadcasted_iota(jnp.int32, (8, D_K), 0)
    row = lambda r: jnp.broadcast_to(b_scr[r:r + 1, :], (8, D_K))
    pieces = []
    for m in range(TILE // 8):
        refs = [((8 * m + u) // (2 * half)) * (2 * half) + (half if rev else half - 1) for u in range(8)]
        piece = row(refs[0])
        for u in range(1, 8):
            if refs[u] != refs[u - 1]:
                piece = jnp.where(sub >= u, row(refs[u]), piece)
        pieces.append(piece)
    return jnp.concatenate(pieces, axis=0)


def _head_blocks(x, width):
    head = lax.broadcasted_iota(jnp.int32, x.shape, 1) // width
    return jnp.concatenate([jnp.where(head == h, x, jnp.zeros_like(x)) for h in range(HEADS)], axis=0)


def _gla_tile(q, k, v, g, tri, lmap, st_ref, b_scr, rev):
    b = _dot_exact(tri, g)
    b_scr[...] = b
    qds, kds = [], []
    for l in range(N_LEVELS - 1):
        d = b - _level_ref(b_scr, l, rev)
        qds.append((q * jnp.exp(jnp.minimum(d, 0.0))).astype(BF16))
        kds.append((k * jnp.exp(jnp.minimum(-d, 0.0))).astype(BF16))
    qds.append(q.astype(BF16))
    kds.append(k.astype(BF16))
    qe = (q * jnp.exp(b)).astype(BF16)
    vb = v.astype(BF16)

    cdim = (((1,), (1,)), ((), ()))
    st_head = lax.broadcasted_iota(jnp.int32, (DV, D_K), 1) // DK
    st = st_ref[...]
    outs = [None] * (TILE // CHUNK)
    for c in (reversed(range(TILE // CHUNK)) if rev else range(TILE // CHUNK)):
        rows = slice(c * CHUNK, (c + 1) * CHUNK)
        o = lax.dot_general(qe[rows], _head_blocks(st.astype(BF16), DK), cdim,
                            preferred_element_type=F32)
        p = jnp.zeros((CHUNK, HEADS * CHUNK), F32)
        for l in range(N_LEVELS):
            z = lax.dot_general(qds[l][rows], _head_blocks(kds[l][rows], DK), cdim,
                                preferred_element_type=F32)
            p = jnp.where(lmap == l, z, p)
        outs[c] = o + jnp.dot(p.astype(BF16), _head_blocks(vb[rows], DV), preferred_element_type=F32)

        b_c = b[rows]
        b_end = b_c[0:1, :] if rev else b_c[CHUNK - 1:CHUNK, :]
        kdec = (k[rows] * jnp.exp(b_end - b_c)).astype(BF16)
        kv = lax.dot_general(vb[rows], kdec, (((0,), (0,)), ((), ())), preferred_element_type=F32)
        kvd = kv[0:DV]
        for h in range(1, HEADS):
            kvd = jnp.where(st_head == h, kv[h * DV:(h + 1) * DV], kvd)
        st = st * jnp.exp(b_end) + kvd
    st_ref[...] = st
    return jnp.concatenate(outs, axis=0)


def _gla_fwd_kernel(q_ref, k_ref, v_ref, g_ref, st0_ref, tri_ref, lmap_ref, wq_ref,
                    of_ref, stout_ref, wqt_ref, st_scr, b_scr):
    i = pl.program_id(0)
    seq_start = (i <= CTX_TILES) | (i == CTX_TILES + LAT_TILES)

    @pl.when(seq_start)
    def _():
        st_scr[...] = st0_ref[0]

    @pl.when(i < WQ_SLABS)
    def _():
        wqt_ref[...] = wq_ref[...].T.astype(BF16)

    of_ref[...] = _gla_tile(q_ref[...], k_ref[...], v_ref[...], g_ref[...], tri_ref[...],
                            lmap_ref[...], st_scr, b_scr, rev=False)

    @pl.when(i < CTX_TILES)
    def _():
        stout_ref[0] = st_scr[...]


def _gla_fwd_call(q, k, v, gk, st0, consts, wq):
    tri, lmap = consts
    tok = lambda c: pl.BlockSpec((TILE, c), lambda i: (i, 0))
    full = lambda a: pl.BlockSpec(a.shape, lambda i: (0,) * a.ndim)
    slab = lambda i: jnp.minimum(i, WQ_SLABS - 1)
    return pl.pallas_call(
        _gla_fwd_kernel,
        grid=(N_TILES,),
        in_specs=[tok(D_K), tok(D_K), tok(D_V), pl.BlockSpec((TILE, D_K), lambda i: (i, 0)),
                  pl.BlockSpec((1, DV, D_K), lambda i: (_cond_row(i), 0, 0)),
                  full(tri), full(lmap),
                  pl.BlockSpec((D_MODEL, 128), lambda i: (0, slab(i)))],
        out_specs=[tok(D_V),
                   pl.BlockSpec((1, DV, D_K), lambda i: (jnp.minimum(i, CTX_TILES - 1), 0, 0)),
                   pl.BlockSpec((128, D_MODEL), lambda i: (slab(i), 0))],
        out_shape=[jax.ShapeDtypeStruct((N_TOK, D_V), F32),
                   jax.ShapeDtypeStruct((N_CTX_SEQ, DV, D_K), F32),
                   jax.ShapeDtypeStruct((WQ_SLABS * 128, D_MODEL), BF16)],
        scratch_shapes=[pltpu.VMEM((DV, D_K), F32), pltpu.VMEM((TILE, D_K), F32)],
        compiler_params=_cparams(("arbitrary",)),
        name="gla_fwd",
    )(q, k, v, gk, st0, tri, lmap, wq)


def _pool_band():
    t = np.arange(TILE)[:, None]
    e = np.arange(TILE + 2 * POOL_HALO)[None, :]
    bands = []
    for w in POOL_WINDOWS:
        lo = t + POOL_HALO - w // 2
        bands.append(((e >= lo) & (e < lo + w)).astype(np.float32))
    return jnp.asarray(np.stack(bands), BF16)


def _gla_bwd_kernel(q_ref, k_ref, v_ref, g_ref, st0_ref, tri_ref, lmap_ref,
                    of_ref, xpool_ref, prev_ref, next_ref, ogs_ref, xp_ref, xs_ref, pos_ref, mod_ref,
                    band_ref, poolw_ref, pscale_ref, gnorm_ref, wo_ref, ln_g_ref, ln_b_ref,
                    x1_ref, stout_ref, st_scr, b_scr):
    j = N_TILES - 1 - pl.program_id(0)
    lat_idx = jnp.maximum(j - CTX_TILES, 0) % LAT_TILES
    is_ctx = j < CTX_TILES
    seq_first = is_ctx | (lat_idx == 0)
    seq_last = is_ctx | (lat_idx == LAT_TILES - 1)

    @pl.when(seq_last)
    def _():
        st_scr[...] = st0_ref[0]

    o_b = _gla_tile(q_ref[...], k_ref[...], v_ref[...], g_ref[...], tri_ref[...],
                    lmap_ref[...], st_scr, b_scr, rev=True)

    @pl.when(is_ctx)
    def _():
        stout_ref[0] = st_scr[...]

    xpool = xpool_ref[...]
    prev = jnp.where(seq_first, 0.0, prev_ref[...])
    nxt = jnp.where(seq_last, 0.0, next_ref[...])
    ext = jnp.concatenate([prev, xpool, nxt], axis=0)
    tpos = jnp.where(is_ctx, 0, lat_idx) * TILE + lax.broadcasted_iota(jnp.int32, (TILE, POOL_CH), 0)
    seq_len = jnp.where(is_ctx, CTX_LEN, LAT_LEN)
    pool_parts = []
    for gi, w in enumerate(POOL_WINDOWS):
        sl = slice(gi * POOL_CH, (gi + 1) * POOL_CH)
        wsum = _dot_exact(band_ref[gi], ext[:, sl], pieces=2)
        lo = jnp.maximum(tpos - w // 2, 0)
        hi = jnp.minimum(tpos - w // 2 + w, seq_len)
        dmean = wsum / (hi - lo).astype(F32) - xpool[:, sl]
        y = jnp.dot(dmean.astype(BF16), poolw_ref[gi], preferred_element_type=F32)
        pool_parts.append(y * pscale_ref[:, sl])

    o = of_ref[...] + o_b
    ogs = ogs_ref[...]
    for h in range(HEADS):
        sl = slice(h * DV, (h + 1) * DV)
        oh = o[:, sl]
        oh = oh * lax.rsqrt(jnp.mean(oh * oh, axis=-1, keepdims=True) + EPS)
        pool_parts.append(oh * gnorm_ref[:, sl] * ogs[:, sl])
    mix_in = jnp.concatenate(pool_parts, axis=1).astype(BF16)
    mix = jnp.dot(mix_in, wo_ref[...], preferred_element_type=F32)

    r = _cond_row(j)
    x = _load_x(j, xp_ref, xs_ref, pos_ref)
    y = ALPHA * x + _mod_rows(mod_ref, r, 2) * mix
    x1_ref[...] = _layer_norm(y, ln_g_ref[...], ln_b_ref[...])


def _gla_bwd_call(q, k, v, gk, st0, consts, o_f, xpool, ogs, xp2, xs2, pos, mod,
                  band, poolw, pscale, gnorm, wo, ln_g, ln_b):
    tri, lmap = consts
    rv = lambda i: N_TILES - 1 - i
    tok = lambda c: pl.BlockSpec((TILE, c), lambda i: (rv(i), 0))
    full = lambda a: pl.BlockSpec(a.shape, lambda i: (0,) * a.ndim)
    halo_blocks = TILE // POOL_HALO
    n_halo = N_TOK // POOL_HALO
    xspecs = [pl.BlockSpec((TILE, D_MODEL), lambda i: (jnp.minimum(rv(i), CTX_TILES - 1), 0)),
              pl.BlockSpec((TILE, D_MODEL), lambda i: (jnp.maximum(rv(i) - CTX_TILES, 0), 0)),
              pl.BlockSpec((TILE, D_MODEL), lambda i: (jnp.maximum(rv(i) - CTX_TILES, 0) % LAT_TILES, 0))]
    return pl.pallas_call(
        _gla_bwd_kernel,
        grid=(N_TILES,),
        in_specs=[tok(D_K), tok(D_K), tok(D_V), pl.BlockSpec((TILE, D_K), lambda i: (rv(i), 1)),
                  pl.BlockSpec((1, DV, D_K), lambda i: (_cond_row(rv(i)), 0, 0)),
                  full(tri), full(lmap),
                  tok(D_V), tok(D_POOL),
                  pl.BlockSpec((POOL_HALO, D_POOL), lambda i: (jnp.maximum(rv(i) * halo_blocks - 1, 0), 0)),
                  pl.BlockSpec((POOL_HALO, D_POOL),
                               lambda i: (jnp.minimum((rv(i) + 1) * halo_blocks, n_halo - 1), 0)),
                  tok(D_V)] + xspecs + [full(mod), full(band), full(poolw), full(pscale), full(gnorm),
                                        full(wo), full(ln_g), full(ln_b)],
        out_specs=[tok(D_MODEL),
                   pl.BlockSpec((1, DV, D_K), lambda i: (jnp.minimum(rv(i), CTX_TILES - 1), 0, 0))],
        out_shape=[jax.ShapeDtypeStruct((N_TOK, D_MODEL), F32),
                   jax.ShapeDtypeStruct((N_CTX_SEQ, DV, D_K), F32)],
        scratch_shapes=[pltpu.VMEM((DV, D_K), F32), pltpu.VMEM((TILE, D_K), F32)],
        compiler_params=_cparams(("arbitrary",)),
        name="gla_bwd_mix",
    )(q, k, v, gk, st0, tri, lmap, o_f, xpool, xpool, xpool, ogs, xp2, xs2, pos, mod,
      band, poolw, pscale, gnorm, wo, ln_g, ln_b)


def _top_values(x, n, with_rank=False):
    vals = []
    rank = jnp.full(x.shape, float(n), F32)
    for j in range(n):
        m = jnp.max(x, axis=0, keepdims=True)
        vals.append(m)
        hit = x == m
        if with_rank:
            rank = jnp.where(hit, float(j), rank)
        x = jnp.where(hit, -jnp.inf, x)
    return (vals, rank) if with_rank else vals


def _route_kernel(x1_ref, mod_ref, wqt_ref, keys_ref, u_ref, v_ref,
                  h2t_ref, cnt_ref, g1_ref, r2_ref, e2_ref, ubf_ref, vtbf_ref):
    i = pl.program_id(0)
    r = _cond_row(i)
    ubf_ref[...] = u_ref[...].astype(BF16)
    vtbf_ref[0] = v_ref[...].T.astype(BF16)
    h2 = x1_ref[...] * (1.0 + _mod_rows(mod_ref, r, 4)) + _mod_rows(mod_ref, r, 3)
    h2t = h2.T.astype(BF16)
    h2t_ref[...] = h2t
    qt = jnp.dot(wqt_ref[...], h2t, preferred_element_type=F32)
    for h in range(PEER_HEADS):
        s = []
        for p in range(2):
            row = (2 * h + p) * N_KEYS
            qhp = qt[row:row + N_KEYS, :].astype(BF16)
            s.append(jnp.dot(keys_ref[2 * h + p], qhp, preferred_element_type=F32))
        sv1, rank1 = _top_values(s[0], PEER_TOPK, with_rank=True)
        sv2, rank2 = _top_values(s[1], PEER_TOPK, with_rank=True)
        a1 = jnp.concatenate(sv1, axis=0)
        a2 = jnp.concatenate(sv2, axis=0)
        row8 = lax.broadcasted_iota(jnp.int32, (8, TILE), 0)
        cand = [a2 + sv1[0]]
        for a in range(1, 8):
            cand.append(jnp.where(row8 < PEER_TOPK // (a + 1), a2[0:8] + sv1[a], -jnp.inf))
        cand.append(a1[8:16] + sv2[0])
        fv = _top_values(jnp.concatenate(cand, axis=0), PEER_TOPK)
        denom = jnp.zeros_like(fv[0])
        for f in fv:
            denom = denom + jnp.exp(f - fv[0])
        thr = fv[PEER_TOPK - 1]
        rank1 = rank1.astype(BF16)
        cnt = jnp.zeros_like(rank1)
        for a in range(PEER_TOPK):
            blk = cand[a] if a < 8 else cand[8][a - 8:a - 7]
            n_a = jnp.sum((blk >= thr).astype(F32), axis=0, keepdims=True).astype(BF16)
            cnt = jnp.where(rank1 == float(a), n_a, cnt)
        cnt_ref[h] = cnt
        g1_ref[h] = (jnp.exp(s[0] - sv1[0]) / denom).astype(BF16)
        r2_ref[h] = rank2.astype(BF16)
        e2_ref[h] = jnp.exp(s[1] - sv2[0]).astype(BF16)


def _route_call(x1, mod, wqt, keys, u, v):
    full = lambda a: pl.BlockSpec(a.shape, lambda i: (0,) * a.ndim)
    hk = pl.BlockSpec((PEER_HEADS, N_KEYS, TILE), lambda i: (0, 0, i))
    per_key = lambda dt: jax.ShapeDtypeStruct((PEER_HEADS, N_KEYS, N_TOK), dt)
    slab = N_EXPERTS // N_TILES
    slabs_per_block = DENSE_CH // slab
    table = pl.BlockSpec((slab, D_MODEL), lambda i: (i, 0))
    return pl.pallas_call(
        _route_kernel,
        grid=(N_TILES,),
        in_specs=[pl.BlockSpec((TILE, D_MODEL), lambda i: (i, 0)), full(mod), full(wqt), full(keys),
                  table, table],
        out_specs=[pl.BlockSpec((D_MODEL, TILE), lambda i: (0, i)), hk, hk, hk, hk, table,
                   pl.BlockSpec((1, D_MODEL, slab), lambda i: (i // slabs_per_block, 0, i % slabs_per_block))],
        out_shape=[jax.ShapeDtypeStruct((D_MODEL, N_TOK), BF16),
                   per_key(BF16), per_key(BF16), per_key(BF16), per_key(BF16),
                   jax.ShapeDtypeStruct((N_EXPERTS, D_MODEL), BF16),
                   jax.ShapeDtypeStruct((N_EXPERTS // DENSE_CH, D_MODEL, DENSE_CH), BF16)],
        compiler_params=_cparams(("arbitrary",)),
        name="peer_route",
    )(x1, mod, wqt, keys, u, v)


def _gelu_tanh(x):
    c1 = math.sqrt(2.0 / math.pi)
    c2 = c1 * 0.044715
    return x * (0.5 + 0.5 * jnp.tanh(x * (c1 + c2 * (x * x))))


def _dense_kernel(h2t_ref, cnt_ref, g1_ref, r2_ref, e2_ref, u_ref, vt_ref, x1_ref, mod_ref,
                  ln_g_ref, ln_b_ref, yp_ref, ys_ref, acc_ref):
    tt = pl.program_id(0)
    c = pl.program_id(1)
    n_sub = DENSE_CH // N_KEYS
    assert n_sub % 16 == 0, "count / gate rows are read as whole packed bf16 row groups"

    @pl.when(c == 0)
    def _():
        acc_ref[...] = jnp.zeros_like(acc_ref)

    i1_base = pl.multiple_of(c * n_sub, n_sub)
    cnt8 = [cnt_ref[h, pl.ds(i1_base, n_sub), :] for h in range(PEER_HEADS)]
    g18 = [g1_ref[h, pl.ds(i1_base, n_sub), :] for h in range(PEER_HEADS)]
    def gate(ci):
        wsum = jnp.zeros((N_KEYS, DENSE_TM), BF16)
        for h in range(PEER_HEADS):
            sel = r2_ref[h] < cnt8[h][ci:ci + 1, :]
            wsum = wsum + jnp.where(sel, e2_ref[h], jnp.zeros((), BF16)) * g18[h][ci:ci + 1, :]
        return wsum

    sub_per_piece = n_sub // DENSE_PIECES
    gates, a_pieces = [], []
    for pc in range(DENSE_PIECES):
        rows = slice(pc * sub_per_piece * N_KEYS, (pc + 1) * sub_per_piece * N_KEYS)
        lhs = u_ref[rows, :]
        if pc > 0:
            zero = gates[-1][0:16, 0:128] * jnp.zeros((), BF16)
            top = jnp.concatenate([lhs[0:16, 0:128] + zero, lhs[0:16, 128:]], axis=1)
            lhs = jnp.concatenate([top, lhs[16:]], axis=0)
        a_pieces.append(jnp.dot(lhs, h2t_ref[...], preferred_element_type=F32))
        gates.extend(gate(ci) for ci in range(pc * sub_per_piece, (pc + 1) * sub_per_piece))
    a = jnp.concatenate(a_pieces, axis=0)
    w_parts = [_gelu_tanh(a[ci * N_KEYS:(ci + 1) * N_KEYS, :].astype(BF16)) * gates[ci] for ci in range(n_sub)]
    zero = gates[-1][0:16, 0:128] * jnp.zeros((), BF16)
    w0 = w_parts[0]
    top = jnp.concatenate([w0[0:16, 0:128] + zero, w0[0:16, 128:]], axis=1)
    w_parts[0] = jnp.concatenate([top, w0[16:]], axis=0)
    w = jnp.concatenate(w_parts, axis=0)
    acc_ref[...] += jnp.dot(vt_ref[0], w, preferred_element_type=F32)

    @pl.when(c == pl.num_programs(1) - 1)
    def _():
        r = _cond_row(tt * (DENSE_TM // TILE))
        ffn = acc_ref[...].T
        y = _layer_norm(ALPHA * x1_ref[...] + _mod_rows(mod_ref, r, 5) * ffn, ln_g_ref[...], ln_b_ref[...])

        @pl.when(tt < DENSE_CTX_TILES)
        def _():
            yp_ref[...] = y

        @pl.when(tt >= DENSE_CTX_TILES)
        def _():
            ys_ref[...] = y


def _dense_call(h2t, cnt, g1, r2, e2, u_bf, vt_bf, x1, mod, ln_g, ln_b):
    full = lambda a: pl.BlockSpec(a.shape, lambda t, c: (0,) * a.ndim)
    hk = pl.BlockSpec((PEER_HEADS, N_KEYS, DENSE_TM), lambda t, c: (0, 0, t))
    n_half = N_CTX_SEQ * CTX_LEN
    return pl.pallas_call(
        _dense_kernel,
        grid=(N_TOK // DENSE_TM, N_EXPERTS // DENSE_CH),
        in_specs=[pl.BlockSpec((D_MODEL, DENSE_TM), lambda t, c: (0, t)),
                  hk, hk, hk, hk,
                  pl.BlockSpec((DENSE_CH, D_MODEL), lambda t, c: (c, 0)),
                  pl.BlockSpec((1, D_MODEL, DENSE_CH), lambda t, c: (c, 0, 0)),
                  pl.BlockSpec((DENSE_TM, D_MODEL), lambda t, c: (t, 0)),
                  full(mod), full(ln_g), full(ln_b)],
        out_specs=[pl.BlockSpec((DENSE_TM, D_MODEL), lambda t, c: (jnp.minimum(t, DENSE_CTX_TILES - 1), 0)),
                   pl.BlockSpec((DENSE_TM, D_MODEL), lambda t, c: (jnp.maximum(t - DENSE_CTX_TILES, 0), 0))],
        out_shape=[jax.ShapeDtypeStruct((n_half, D_MODEL), F32),
                   jax.ShapeDtypeStruct((N_TOK - n_half, D_MODEL), F32)],
        scratch_shapes=[pltpu.VMEM((D_MODEL, DENSE_TM), F32)],
        compiler_params=_cparams(("arbitrary", "arbitrary")),
        name="peer_dense",
    )(h2t, cnt, g1, r2, e2, u_bf, vt_bf, x1, mod, ln_g, ln_b)


def _grid_pos_embed():
    rows = LAT_LEN // GRID_W
    r, col = np.meshgrid(np.arange(rows), np.arange(GRID_W), indexing="ij")

    def sincos(pos, dim):
        omega = 1.0 / (10000.0 ** (np.arange(dim // 2, dtype=np.float64) / (dim // 2)))
        ang = pos.reshape(-1).astype(np.float64)[:, None] * omega[None, :]
        return np.concatenate([np.sin(ang), np.cos(ang)], axis=-1)

    pe = np.concatenate([sincos(r, D_MODEL // 2), sincos(col, D_MODEL // 2)], axis=-1)
    return jnp.asarray(pe, F32)


def _state_pack_t(state):
    packed = jnp.transpose(state, (0, 3, 1, 2)).reshape(N_LAT_SEQ, DV, D_K)
    return jnp.concatenate([jnp.zeros((1, DV, D_K), F32), packed], axis=0)


def _state_unpack_t(st):
    return jnp.transpose(st.reshape(N_CTX_SEQ, DV, HEADS, DK), (0, 2, 3, 1))[:, None]


def kernel(x_prompt, x_sample, c, state_fwd, state_bwd, c_ctx, w_ada, b_ada, w_in, pool_w, pool_scale,
           gk_w, gk_b, gla_norm_g, w_o, ln1_g, ln1_b, peer_wq, peer_keys, peer_u, peer_v, ln2_g, ln2_b):
    xp2 = x_prompt.reshape(N_CTX_SEQ * CTX_LEN, D_MODEL)
    xs2 = x_sample.reshape(N_LAT_SEQ * LAT_LEN, D_MODEL)
    pos = _grid_pos_embed()

    cond8 = jnp.zeros((8, D_MODEL), F32).at[0].set(c_ctx).at[1:3].set(c)
    mod = _mod_call(cond8, w_ada[0], b_ada[0][None, :])

    w = w_in[0]
    w_in_r = jnp.concatenate([w[:, :1536], w[:, 1568:], w[:, 1536:1568],
                              jnp.zeros((D_MODEL, P_COLS - 2080), F32)], axis=1).astype(BF16)
    gkw_bd = jnp.zeros((128, 2 * D_K), F32)
    gkw_bd = gkw_bd.at[0:GATE_RANK, 0:D_K].set(gk_w[0, 0]).at[GATE_RANK:2 * GATE_RANK, D_K:].set(gk_w[0, 1])
    gkb = gk_b[0].reshape(1, 2 * D_K)
    gkw2 = jnp.concatenate([gkw_bd, gkw_bd], axis=0).astype(BF16)
    xpool, q, k, v, gk, ogs = _proj_call(xp2, xs2, pos, mod, w_in_r, gkw2, gkb)

    o_f, st_f, wqt = _gla_fwd_call(q, k, v, gk, _state_pack_t(state_fwd[:, 0]), _gla_constants(False),
                                   peer_wq[0])
    x1, st_b = _gla_bwd_call(
        q, k, v, gk, _state_pack_t(state_bwd[:, 0]), _gla_constants(True), o_f, xpool, ogs,
        xp2, xs2, pos, mod, _pool_band(), pool_w[0].astype(BF16), pool_scale[0][None, :],
        gla_norm_g[0][None, :], w_o[0].astype(BF16), ln1_g[0][None, :], ln1_b[0][None, :])

    keys = peer_keys[0].reshape(2 * PEER_HEADS, N_KEYS, N_KEYS).astype(BF16)
    h2t, cnt, g1, r2, e2, u_bf, vt_bf = _route_call(x1, mod, wqt, keys, peer_u[0], peer_v[0])
    yp, ys = _dense_call(h2t, cnt, g1, r2, e2, u_bf, vt_bf, x1, mod, ln2_g[0][None, :], ln2_b[0][None, :])

    y_prompt = yp.reshape(N_CTX_SEQ, CTX_LEN, D_MODEL)
    y_sample = ys.reshape(N_LAT_SEQ, LAT_LEN, D_MODEL)
    return (y_prompt, y_sample, _state_unpack_t(st_f), _state_unpack_t(st_b))
```

```python
import math

import numpy as np
import jax
import jax.numpy as jnp
from jax import lax
from jax.experimental import pallas as pl
from jax.experimental.pallas import tpu as pltpu

F32 = jnp.float32
BF16 = jnp.bfloat16

D_MODEL = 1024
N_CTX_SEQ, CTX_LEN = 16, 256
N_LAT_SEQ, LAT_LEN = 2, 2048
GRID_W = 64
N_TOK = N_CTX_SEQ * CTX_LEN + N_LAT_SEQ * LAT_LEN
TILE = 256
N_TILES = N_TOK // TILE
CTX_TILES = N_CTX_SEQ * CTX_LEN // TILE
LAT_TILES = LAT_LEN // TILE

D_POOL = 512
POOL_GROUPS = 4
POOL_CH = 128
POOL_WINDOWS = (2, 4, 8, 16)
POOL_HALO = 8
HEADS = 4
DK = 64
DV = 128
D_K = HEADS * DK
D_V = HEADS * DV
GATE_RANK = 16
GATE_NORM = 16.0
CHUNK = 64
N_LEVELS = 7
P_COLS = 2048 + 128

PEER_HEADS = 8
N_KEYS = 128
PEER_TOPK = 16
N_EXPERTS = N_KEYS * N_KEYS
WQ_SLABS = PEER_HEADS * 2 * N_KEYS // 128
DENSE_TM = 512
DENSE_CH = 2048
DENSE_CTX_TILES = N_CTX_SEQ * CTX_LEN // DENSE_TM
DENSE_PIECES = 16
MOD_COLS = 1536
ALPHA = 2.0 ** 0.25
EPS = 1e-5
V7X_VMEM_BYTES = 64 * 1024 * 1024
VMEM_LIMIT = V7X_VMEM_BYTES * 7 // 8


def _cparams(sem):
    return pltpu.CompilerParams(dimension_semantics=sem, vmem_limit_bytes=VMEM_LIMIT)


def _silu(x):
    return x * jax.nn.sigmoid(x)


def _split_bf16(x, pieces):
    out = []
    for _ in range(pieces - 1):
        hi = x.astype(BF16)
        out.append(hi)
        x = x - hi.astype(F32)
    out.append(x.astype(BF16))
    return out


def _dot_exact(m01, x, pieces=3):
    c = x.shape[1]
    xs = jnp.concatenate(_split_bf16(x, pieces), axis=1)
    y = jnp.dot(m01, xs, preferred_element_type=F32)
    out = y[:, :c]
    for p in range(1, pieces):
        out = out + y[:, p * c:(p + 1) * c]
    return out


def _layer_norm(y, g, b):
    mu = jnp.mean(y, axis=-1, keepdims=True)
    yc = y - mu
    var = jnp.mean(yc * yc, axis=-1, keepdims=True)
    return yc * lax.rsqrt(var + EPS) * g + b


def _cond_row(tile):
    return (tile >= CTX_TILES).astype(jnp.int32) + (tile >= CTX_TILES + LAT_TILES).astype(jnp.int32)


def _mod_kernel(cond_ref, w_ref, b_ref, out_ref):
    s = _silu(cond_ref[...]).astype(BF16)
    out_ref[...] = jnp.dot(s, w_ref[...].astype(BF16), preferred_element_type=F32) + b_ref[...]


def _mod_call(cond8, w_ada, b_ada):
    n = w_ada.shape[1]
    bn = MOD_COLS
    return pl.pallas_call(
        _mod_kernel,
        grid=(n // bn,),
        in_specs=[pl.BlockSpec((8, D_MODEL), lambda j: (0, 0)),
                  pl.BlockSpec((D_MODEL, bn), lambda j: (0, j)),
                  pl.BlockSpec((1, bn), lambda j: (0, j))],
        out_specs=pl.BlockSpec((8, bn), lambda j: (0, j)),
        out_shape=jax.ShapeDtypeStruct((8, n), F32),
        compiler_params=_cparams(("arbitrary",)),
        name="mod",
    )(cond8, w_ada, b_ada)


def _load_x(i, xp_ref, xs_ref, pos_ref):
    return jnp.where(i < CTX_TILES, xp_ref[...], xs_ref[...] + pos_ref[...])


def _mod_rows(mod_ref, r, k):
    return mod_ref[pl.ds(r, 1), k * D_MODEL:(k + 1) * D_MODEL]


def _proj_kernel(xp_ref, xs_ref, pos_ref, mod_ref, w_ref, gkw_ref, gkb_ref,
                 oxp, oq, ok, ov, ogk, oog):
    i = pl.program_id(0)
    r = _cond_row(i)
    x = _load_x(i, xp_ref, xs_ref, pos_ref)
    h = x * (1.0 + _mod_rows(mod_ref, r, 1)) + _mod_rows(mod_ref, r, 0)
    p = jnp.dot(h.astype(BF16), w_ref[...], preferred_element_type=F32)
    oxp[...] = p[:, 0:512]
    oq[...] = p[:, 512:768] * (DK ** -0.5)
    ok[...] = p[:, 768:1024]
    ov[...] = p[:, 1024:1536]
    oog[...] = _silu(p[:, 1536:2048])
    glr = p[:, 2048:2176]
    g2 = jnp.concatenate(_split_bf16(glr, 2), axis=1)
    pre = jnp.dot(g2, gkw_ref[...], preferred_element_type=F32) + gkb_ref[...]
    ogk[...] = (jnp.minimum(pre, 0.0) - jnp.log(1.0 + jnp.exp(-jnp.abs(pre)))) * (1.0 / GATE_NORM)


def _x_specs():
    return [pl.BlockSpec((TILE, D_MODEL), lambda i: (jnp.minimum(i, CTX_TILES - 1), 0)),
            pl.BlockSpec((TILE, D_MODEL), lambda i: (jnp.maximum(i - CTX_TILES, 0), 0)),
            pl.BlockSpec((TILE, D_MODEL), lambda i: (jnp.maximum(i - CTX_TILES, 0) % LAT_TILES, 0))]


def _proj_call(xp2, xs2, pos, mod, w_in_r, gkw_bd, gkb):
    tok = lambda c: pl.BlockSpec((TILE, c), lambda i: (i, 0))
    full = lambda a: pl.BlockSpec(a.shape, lambda i: (0,) * a.ndim)
    outs = [512, 256, 256, 512, 512, 512]
    return pl.pallas_call(
        _proj_kernel,
        grid=(N_TILES,),
        in_specs=_x_specs() + [full(mod), full(w_in_r), full(gkw_bd), full(gkb)],
        out_specs=[tok(c) for c in outs],
        out_shape=[jax.ShapeDtypeStruct((N_TOK, c), F32) for c in outs],
        compiler_params=_cparams(("arbitrary",)),
        name="proj",
    )(xp2, xs2, pos, mod, w_in_r, gkw_bd, gkb)


def _gla_constants(rev):
    t = np.arange(TILE)
    same_chunk = (t[:, None] // CHUNK) == (t[None, :] // CHUNK)
    tri = same_chunk & ((t[None, :] >= t[:, None]) if rev else (t[None, :] <= t[:, None]))
    u = np.arange(CHUNK)
    x = u[:, None] ^ u[None, :]
    lvl = np.where(x > 0, np.floor(np.log2(np.maximum(x, 1))), N_LEVELS - 1).astype(np.int32)
    causal = (u[:, None] <= u[None, :]) if rev else (u[:, None] >= u[None, :])
    lvl = np.where(causal, lvl, -1).astype(np.int32)
    return jnp.asarray(tri, BF16), jnp.asarray(np.tile(lvl, (1, HEADS)), jnp.int32)


def _level_ref(b_scr, level, rev):
    half = 1 << level
    sub = lax.broadcasted_iota(jnp.int32, (8, D_K), 0)
    row = lambda r: jnp.broadcast_to(b_scr[r:r + 1, :], (8, D_K))
    pieces = []
    for m in range(TILE // 8):
        refs = [((8 * m + u) // (2 * half)) * (2 * half) + (half if rev else half - 1) for u in range(8)]
        piece = row(refs[0])
        for u in range(1, 8):
            if refs[u] != refs[u - 1]:
                piece = jnp.where(sub >= u, row(refs[u]), piece)
        pieces.append(piece)
    return jnp.concatenate(pieces, axis=0)


def _head_blocks(x, width):
    head = lax.broadcasted_iota(jnp.int32, x.shape, 1) // width
    return jnp.concatenate([jnp.where(head == h, x, jnp.zeros_like(x)) for h in range(HEADS)], axis=0)


def _gla_tile(q, k, v, g, tri, lmap, st_ref, b_scr, rev):
    b = _dot_exact(tri, g)
    b_scr[...] = b
    qds, kds = [], []
    for l in range(N_LEVELS - 1):
        d = b - _level_ref(b_scr, l, rev)
        qds.append((q * jnp.exp(jnp.minimum(d, 0.0))).astype(BF16))
        kds.append((k * jnp.exp(jnp.minimum(-d, 0.0))).astype(BF16))
    qds.append(q.astype(BF16))
    kds.append(k.astype(BF16))
    qe = (q * jnp.exp(b)).astype(BF16)
    vb = v.astype(BF16)

    cdim = (((1,), (1,)), ((), ()))
    st_head = lax.broadcasted_iota(jnp.int32, (DV, D_K), 1) // DK
    st = st_ref[...]
    outs = [None] * (TILE // CHUNK)
    for c in (reversed(range(TILE // CHUNK)) if rev else range(TILE // CHUNK)):
        rows = slice(c * CHUNK, (c + 1) * CHUNK)
        o = lax.dot_general(qe[rows], _head_blocks(st.astype(BF16), DK), cdim,
                            preferred_element_type=F32)
        p = jnp.zeros((CHUNK, HEADS * CHUNK), F32)
        for l in range(N_LEVELS):
            z = lax.dot_general(qds[l][rows], _head_blocks(kds[l][rows], DK), cdim,
                                preferred_element_type=F32)
            p = jnp.where(lmap == l, z, p)
        outs[c] = o + jnp.dot(p.astype(BF16), _head_blocks(vb[rows], DV), preferred_element_type=F32)

        b_c = b[rows]
        b_end = b_c[0:1, :] if rev else b_c[CHUNK - 1:CHUNK, :]
        kdec = (k[rows] * jnp.exp(b_end - b_c)).astype(BF16)
        kv = lax.dot_general(vb[rows], kdec, (((0,), (0,)), ((), ())), preferred_element_type=F32)
        kvd = kv[0:DV]
        for h in range(1, HEADS):
            kvd = jnp.where(st_head == h, kv[h * DV:(h + 1) * DV], kvd)
        st = st * jnp.exp(b_end) + kvd
    st_ref[...] = st
    return jnp.concatenate(outs, axis=0)


def _gla_fwd_kernel(q_ref, k_ref, v_ref, g_ref, st0_ref, tri_ref, lmap_ref, wq_ref,
                    of_ref, stout_ref, wqt_ref, st_scr, b_scr):
    i = pl.program_id(0)
    seq_start = (i <= CTX_TILES) | (i == CTX_TILES + LAT_TILES)

    @pl.when(seq_start)
    def _():
        st_scr[...] = st0_ref[0]

    @pl.when(i < WQ_SLABS)
    def _():
        wqt_ref[...] = wq_ref[...].T.astype(BF16)

    of_ref[...] = _gla_tile(q_ref[...], k_ref[...], v_ref[...], g_ref[...], tri_ref[...],
                            lmap_ref[...], st_scr, b_scr, rev=False)

    @pl.when(i < CTX_TILES)
    def _():
        stout_ref[0] = st_scr[...]


def _gla_fwd_call(q, k, v, gk, st0, consts, wq):
    tri, lmap = consts
    tok = lambda c: pl.BlockSpec((TILE, c), lambda i: (i, 0))
    full = lambda a: pl.BlockSpec(a.shape, lambda i: (0,) * a.ndim)
    slab = lambda i: jnp.minimum(i, WQ_SLABS - 1)
    return pl.pallas_call(
        _gla_fwd_kernel,
        grid=(N_TILES,),
        in_specs=[tok(D_K), tok(D_K), tok(D_V), pl.BlockSpec((TILE, D_K), lambda i: (i, 0)),
                  pl.BlockSpec((1, DV, D_K), lambda i: (_cond_row(i), 0, 0)),
                  full(tri), full(lmap),
                  pl.BlockSpec((D_MODEL, 128), lambda i: (0, slab(i)))],
        out_specs=[tok(D_V),
                   pl.BlockSpec((1, DV, D_K), lambda i: (jnp.minimum(i, CTX_TILES - 1), 0, 0)),
                   pl.BlockSpec((128, D_MODEL), lambda i: (slab(i), 0))],
        out_shape=[jax.ShapeDtypeStruct((N_TOK, D_V), F32),
                   jax.ShapeDtypeStruct((N_CTX_SEQ, DV, D_K), F32),
                   jax.ShapeDtypeStruct((WQ_SLABS * 128, D_MODEL), BF16)],
        scratch_shapes=[pltpu.VMEM((DV, D_K), F32), pltpu.VMEM((TILE, D_K), F32)],
        compiler_params=_cparams(("arbitrary",)),
        name="gla_fwd",
    )(q, k, v, gk, st0, tri, lmap, wq)


def _pool_band():
    t = np.arange(TILE)[:, None]
    e = np.arange(TILE + 2 * POOL_HALO)[None, :]
    bands = []
    for w in POOL_WINDOWS:
        lo = t + POOL_HALO - w // 2
        bands.append(((e >= lo) & (e < lo + w)).astype(np.float32))
    return jnp.asarray(np.stack(bands), BF16)


def _gla_bwd_kernel(q_ref, k_ref, v_ref, g_ref, st0_ref, tri_ref, lmap_ref,
                    of_ref, xpool_ref, prev_ref, next_ref, ogs_ref, xp_ref, xs_ref, pos_ref, mod_ref,
                    band_ref, poolw_ref, pscale_ref, gnorm_ref, wo_ref, ln_g_ref, ln_b_ref,
                    x1_ref, stout_ref, st_scr, b_scr):
    j = N_TILES - 1 - pl.program_id(0)
    lat_idx = jnp.maximum(j - CTX_TILES, 0) % LAT_TILES
    is_ctx = j < CTX_TILES
    seq_first = is_ctx | (lat_idx == 0)
    seq_last = is_ctx | (lat_idx == LAT_TILES - 1)

    @pl.when(seq_last)
    def _():
        st_scr[...] = st0_ref[0]

    o_b = _gla_tile(q_ref[...], k_ref[...], v_ref[...], g_ref[...], tri_ref[...],
                    lmap_ref[...], st_scr, b_scr, rev=True)

    @pl.when(is_ctx)
    def _():
        stout_ref[0] = st_scr[...]

    xpool = xpool_ref[...]
    prev = jnp.where(seq_first, 0.0, prev_ref[...])
    nxt = jnp.where(seq_last, 0.0, next_ref[...])
    ext = jnp.concatenate([prev, xpool, nxt], axis=0)
    tpos = jnp.where(is_ctx, 0, lat_idx) * TILE + lax.broadcasted_iota(jnp.int32, (TILE, POOL_CH), 0)
    seq_len = jnp.where(is_ctx, CTX_LEN, LAT_LEN)
    pool_parts = []
    for gi, w in enumerate(POOL_WINDOWS):
        sl = slice(gi * POOL_CH, (gi + 1) * POOL_CH)
        wsum = _dot_exact(band_ref[gi], ext[:, sl], pieces=2)
        lo = jnp.maximum(tpos - w // 2, 0)
        hi = jnp.minimum(tpos - w // 2 + w, seq_len)
        dmean = wsum / (hi - lo).astype(F32) - xpool[:, sl]
        y = jnp.dot(dmean.astype(BF16), poolw_ref[gi], preferred_element_type=F32)
        pool_parts.append(y * pscale_ref[:, sl])

    o = of_ref[...] + o_b
    ogs = ogs_ref[...]
    for h in range(HEADS):
        sl = slice(h * DV, (h + 1) * DV)
        oh = o[:, sl]
        oh = oh * lax.rsqrt(jnp.mean(oh * oh, axis=-1, keepdims=True) + EPS)
        pool_parts.append(oh * gnorm_ref[:, sl] * ogs[:, sl])
    mix_in = jnp.concatenate(pool_parts, axis=1).astype(BF16)
    mix = jnp.dot(mix_in, wo_ref[...], preferred_element_type=F32)

    r = _cond_row(j)
    x = _load_x(j, xp_ref, xs_ref, pos_ref)
    y = ALPHA * x + _mod_rows(mod_ref, r, 2) * mix
    x1_ref[...] = _layer_norm(y, ln_g_ref[...], ln_b_ref[...])


def _gla_bwd_call(q, k, v, gk, st0, consts, o_f, xpool, ogs, xp2, xs2, pos, mod,
                  band, poolw, pscale, gnorm, wo, ln_g, ln_b):
    tri, lmap = consts
    rv = lambda i: N_TILES - 1 - i
    tok = lambda c: pl.BlockSpec((TILE, c), lambda i: (rv(i), 0))
    full = lambda a: pl.BlockSpec(a.shape, lambda i: (0,) * a.ndim)
    halo_blocks = TILE // POOL_HALO
    n_halo = N_TOK // POOL_HALO
    xspecs = [pl.BlockSpec((TILE, D_MODEL), lambda i: (jnp.minimum(rv(i), CTX_TILES - 1), 0)),
              pl.BlockSpec((TILE, D_MODEL), lambda i: (jnp.maximum(rv(i) - CTX_TILES, 0), 0)),
              pl.BlockSpec((TILE, D_MODEL), lambda i: (jnp.maximum(rv(i) - CTX_TILES, 0) % LAT_TILES, 0))]
    return pl.pallas_call(
        _gla_bwd_kernel,
        grid=(N_TILES,),
        in_specs=[tok(D_K), tok(D_K), tok(D_V), pl.BlockSpec((TILE, D_K), lambda i: (rv(i), 1)),
                  pl.BlockSpec((1, DV, D_K), lambda i: (_cond_row(rv(i)), 0, 0)),
                  full(tri), full(lmap),
                  tok(D_V), tok(D_POOL),
                  pl.BlockSpec((POOL_HALO, D_POOL), lambda i: (jnp.maximum(rv(i) * halo_blocks - 1, 0), 0)),
                  pl.BlockSpec((POOL_HALO, D_POOL),
                               lambda i: (jnp.minimum((rv(i) + 1) * halo_blocks, n_halo - 1), 0)),
                  tok(D_V)] + xspecs + [full(mod), full(band), full(poolw), full(pscale), full(gnorm),
                                        full(wo), full(ln_g), full(ln_b)],
        out_specs=[tok(D_MODEL),
                   pl.BlockSpec((1, DV, D_K), lambda i: (jnp.minimum(rv(i), CTX_TILES - 1), 0, 0))],
        out_shape=[jax.ShapeDtypeStruct((N_TOK, D_MODEL), F32),
                   jax.ShapeDtypeStruct((N_CTX_SEQ, DV, D_K), F32)],
        scratch_shapes=[pltpu.VMEM((DV, D_K), F32), pltpu.VMEM((TILE, D_K), F32)],
        compiler_params=_cparams(("arbitrary",)),
        name="gla_bwd_mix",
    )(q, k, v, gk, st0, tri, lmap, o_f, xpool, xpool, xpool, ogs, xp2, xs2, pos, mod,
      band, poolw, pscale, gnorm, wo, ln_g, ln_b)


def _top_values(x, n, with_rank=False, first_only=False):
    vals = []
    rank = jnp.full(x.shape, float(n), F32)
    row = lax.broadcasted_iota(jnp.int32, x.shape, 0) if first_only else None
    for j in range(n):
        m = jnp.max(x, axis=0, keepdims=True)
        vals.append(m)
        hit = x == m
        if first_only:
            hit = row == jnp.min(jnp.where(hit, row, x.shape[0]), axis=0, keepdims=True)
        if with_rank:
            rank = jnp.where(hit, float(j), rank)
        x = jnp.where(hit, -jnp.inf, x)
    return (vals, rank) if with_rank else vals


def _route_select(s1, s2, exact_ties):
    n_tok = s1.shape[1]
    sv1, rank1 = _top_values(s1, PEER_TOPK, with_rank=True, first_only=exact_ties)
    sv2, rank2 = _top_values(s2, PEER_TOPK, with_rank=True, first_only=exact_ties)
    a1 = jnp.concatenate(sv1, axis=0)
    a2 = jnp.concatenate(sv2, axis=0)
    row8 = lax.broadcasted_iota(jnp.int32, (8, n_tok), 0)
    cand = [a2 + sv1[0]]
    for a in range(1, 8):
        cand.append(jnp.where(row8 < PEER_TOPK // (a + 1), a2[0:8] + sv1[a], -jnp.inf))
    cand.append(a1[8:16] + sv2[0])
    cand_all = jnp.concatenate(cand, axis=0)
    if exact_ties:
        fv, crank = _top_values(cand_all, PEER_TOPK, with_rank=True, first_only=True)
        taken = (crank < float(PEER_TOPK)).astype(F32)
        blocks = [taken[0:16]] + [taken[8 + 8 * a:16 + 8 * a] for a in range(1, 8)]
        blocks += [taken[72 + a:73 + a] for a in range(8)]
    else:
        fv = _top_values(cand_all, PEER_TOPK)
        thr = fv[PEER_TOPK - 1]
        blocks = [(cand[a] >= thr).astype(F32) for a in range(8)]
        blocks += [(cand[8][a:a + 1] >= thr).astype(F32) for a in range(8)]
    denom = jnp.zeros_like(fv[0])
    for f in fv:
        denom = denom + jnp.exp(f - fv[0])
    rank1 = rank1.astype(BF16)
    rank2 = rank2.astype(BF16)
    cnt = jnp.zeros_like(rank1)
    n_sel = jnp.zeros((1, n_tok), F32)
    for a in range(PEER_TOPK):
        n_a = jnp.sum(blocks[a], axis=0, keepdims=True)
        n_sel = n_sel + n_a
        cnt = jnp.where(rank1 == float(a), n_a.astype(BF16), cnt)
    g1 = (jnp.exp(s1 - sv1[0]) / denom).astype(BF16)
    e2 = jnp.exp(s2 - sv2[0]).astype(BF16)
    if exact_ties:
        return cnt, g1, rank2, e2, jnp.zeros((1, n_tok), F32)
    took = lambda rk: jnp.sum((rk < float(PEER_TOPK)).astype(F32), axis=0, keepdims=True)
    k = float(PEER_TOPK)
    tied = ((took(rank1) != k) | (took(rank2) != k) | (n_sel != k)).astype(F32)
    return cnt, g1, rank2, e2, tied


def _route_kernel(x1_ref, mod_ref, wqt_ref, keys_ref, u_ref, v_ref,
                  h2t_ref, cnt_ref, g1_ref, r2_ref, e2_ref, ubf_ref, vtbf_ref, s_scr):
    i = pl.program_id(0)
    r = _cond_row(i)
    ubf_ref[...] = u_ref[...].astype(BF16)
    vtbf_ref[0] = v_ref[...].T.astype(BF16)
    h2 = x1_ref[...] * (1.0 + _mod_rows(mod_ref, r, 4)) + _mod_rows(mod_ref, r, 3)
    h2t = h2.T.astype(BF16)
    h2t_ref[...] = h2t
    qt = jnp.dot(wqt_ref[...], h2t, preferred_element_type=F32)

    def store(h, sel):
        cnt_ref[h], g1_ref[h], r2_ref[h], e2_ref[h] = sel[0], sel[1], sel[2], sel[3]

    tied = jnp.zeros((1, TILE), F32)
    for h in range(PEER_HEADS):
        s = []
        for p in range(2):
            row = (2 * h + p) * N_KEYS
            qhp = qt[row:row + N_KEYS, :].astype(BF16)
            s.append(jnp.dot(keys_ref[2 * h + p], qhp, preferred_element_type=F32))
            s_scr[2 * h + p] = s[p]
        sel = _route_select(s[0], s[1], exact_ties=False)
        store(h, sel)
        tied = tied + sel[4]

    @pl.when(jnp.max(tied) > 0.0)
    def _():
        def head(h, carry):
            store(h, _route_select(s_scr[2 * h], s_scr[2 * h + 1], exact_ties=True))
            return carry
        lax.fori_loop(0, PEER_HEADS, head, 0)


def _route_call(x1, mod, wqt, keys, u, v):
    full = lambda a: pl.BlockSpec(a.shape, lambda i: (0,) * a.ndim)
    hk = pl.BlockSpec((PEER_HEADS, N_KEYS, TILE), lambda i: (0, 0, i))
    per_key = lambda dt: jax.ShapeDtypeStruct((PEER_HEADS, N_KEYS, N_TOK), dt)
    slab = N_EXPERTS // N_TILES
    slabs_per_block = DENSE_CH // slab
    table = pl.BlockSpec((slab, D_MODEL), lambda i: (i, 0))
    return pl.pallas_call(
        _route_kernel,
        grid=(N_TILES,),
        in_specs=[pl.BlockSpec((TILE, D_MODEL), lambda i: (i, 0)), full(mod), full(wqt), full(keys),
                  table, table],
        out_specs=[pl.BlockSpec((D_MODEL, TILE), lambda i: (0, i)), hk, hk, hk, hk, table,
                   pl.BlockSpec((1, D_MODEL, slab), lambda i: (i // slabs_per_block, 0, i % slabs_per_block))],
        out_shape=[jax.ShapeDtypeStruct((D_MODEL, N_TOK), BF16),
                   per_key(BF16), per_key(BF16), per_key(BF16), per_key(BF16),
                   jax.ShapeDtypeStruct((N_EXPERTS, D_MODEL), BF16),
                   jax.ShapeDtypeStruct((N_EXPERTS // DENSE_CH, D_MODEL, DENSE_CH), BF16)],
        scratch_shapes=[pltpu.VMEM((2 * PEER_HEADS, N_KEYS, TILE), F32)],
        compiler_params=_cparams(("arbitrary",)),
        name="peer_route",
    )(x1, mod, wqt, keys, u, v)


def _gelu_tanh(x):
    c1 = math.sqrt(2.0 / math.pi)
    c2 = c1 * 0.044715
    return x * (0.5 + 0.5 * jnp.tanh(x * (c1 + c2 * (x * x))))


def _dense_kernel(h2t_ref, cnt_ref, g1_ref, r2_ref, e2_ref, u_ref, vt_ref, x1_ref, mod_ref,
                  ln_g_ref, ln_b_ref, yp_ref, ys_ref, acc_ref):
    tt = pl.program_id(0)
    c = pl.program_id(1)
    n_sub = DENSE_CH // N_KEYS
    assert n_sub % 16 == 0, "count / gate rows are read as whole packed bf16 row groups"

    @pl.when(c == 0)
    def _():
        acc_ref[...] = jnp.zeros_like(acc_ref)

    i1_base = pl.multiple_of(c * n_sub, n_sub)
    cnt8 = [cnt_ref[h, pl.ds(i1_base, n_sub), :] for h in range(PEER_HEADS)]
    g18 = [g1_ref[h, pl.ds(i1_base, n_sub), :] for h in range(PEER_HEADS)]
    def gate(ci):
        wsum = jnp.zeros((N_KEYS, DENSE_TM), BF16)
        for h in range(PEER_HEADS):
            sel = r2_ref[h] < cnt8[h][ci:ci + 1, :]
            wsum = wsum + jnp.where(sel, e2_ref[h], jnp.zeros((), BF16)) * g18[h][ci:ci + 1, :]
        return wsum

    sub_per_piece = n_sub // DENSE_PIECES
    gates, a_pieces = [], []
    for pc in range(DENSE_PIECES):
        rows = slice(pc * sub_per_piece * N_KEYS, (pc + 1) * sub_per_piece * N_KEYS)
        lhs = u_ref[rows, :]
        if pc > 0:
            zero = gates[-1][0:16, 0:128] * jnp.zeros((), BF16)
            top = jnp.concatenate([lhs[0:16, 0:128] + zero, lhs[0:16, 128:]], axis=1)
            lhs = jnp.concatenate([top, lhs[16:]], axis=0)
        a_pieces.append(jnp.dot(lhs, h2t_ref[...], preferred_element_type=F32))
        gates.extend(gate(ci) for ci in range(pc * sub_per_piece, (pc + 1) * sub_per_piece))
    a = jnp.concatenate(a_pieces, axis=0)
    w_parts = [_gelu_tanh(a[ci * N_KEYS:(ci + 1) * N_KEYS, :].astype(BF16)) * gates[ci] for ci in range(n_sub)]
    zero = gates[-1][0:16, 0:128] * jnp.zeros((), BF16)
    w0 = w_parts[0]
    top = jnp.concatenate([w0[0:16, 0:128] + zero, w0[0:16, 128:]], axis=1)
    w_parts[0] = jnp.concatenate([top, w0[16:]], axis=0)
    w = jnp.concatenate(w_parts, axis=0)
    acc_ref[...] += jnp.dot(vt_ref[0], w, preferred_element_type=F32)

    @pl.when(c == pl.num_programs(1) - 1)
    def _():
        r = _cond_row(tt * (DENSE_TM // TILE))
        ffn = acc_ref[...].T
        y = _layer_norm(ALPHA * x1_ref[...] + _mod_rows(mod_ref, r, 5) * ffn, ln_g_ref[...], ln_b_ref[...])

        @pl.when(tt < DENSE_CTX_TILES)
        def _():
            yp_ref[...] = y

        @pl.when(tt >= DENSE_CTX_TILES)
        def _():
            ys_ref[...] = y


def _dense_call(h2t, cnt, g1, r2, e2, u_bf, vt_bf, x1, mod, ln_g, ln_b):
    full = lambda a: pl.BlockSpec(a.shape, lambda t, c: (0,) * a.ndim)
    hk = pl.BlockSpec((PEER_HEADS, N_KEYS, DENSE_TM), lambda t, c: (0, 0, t))
    n_half = N_CTX_SEQ * CTX_LEN
    return pl.pallas_call(
        _dense_kernel,
        grid=(N_TOK // DENSE_TM, N_EXPERTS // DENSE_CH),
        in_specs=[pl.BlockSpec((D_MODEL, DENSE_TM), lambda t, c: (0, t)),
                  hk, hk, hk, hk,
                  pl.BlockSpec((DENSE_CH, D_MODEL), lambda t, c: (c, 0)),
                  pl.BlockSpec((1, D_MODEL, DENSE_CH), lambda t, c: (c, 0, 0)),
                  pl.BlockSpec((DENSE_TM, D_MODEL), lambda t, c: (t, 0)),
                  full(mod), full(ln_g), full(ln_b)],
        out_specs=[pl.BlockSpec((DENSE_TM, D_MODEL), lambda t, c: (jnp.minimum(t, DENSE_CTX_TILES - 1), 0)),
                   pl.BlockSpec((DENSE_TM, D_MODEL), lambda t, c: (jnp.maximum(t - DENSE_CTX_TILES, 0), 0))],
        out_shape=[jax.ShapeDtypeStruct((n_half, D_MODEL), F32),
                   jax.ShapeDtypeStruct((N_TOK - n_half, D_MODEL), F32)],
        scratch_shapes=[pltpu.VMEM((D_MODEL, DENSE_TM), F32)],
        compiler_params=_cparams(("arbitrary", "arbitrary")),
        name="peer_dense",
    )(h2t, cnt, g1, r2, e2, u_bf, vt_bf, x1, mod, ln_g, ln_b)


def _grid_pos_embed():
    rows = LAT_LEN // GRID_W
    r, col = np.meshgrid(np.arange(rows), np.arange(GRID_W), indexing="ij")

    def sincos(pos, dim):
        omega = 1.0 / (10000.0 ** (np.arange(dim // 2, dtype=np.float64) / (dim // 2)))
        ang = pos.reshape(-1).astype(np.float64)[:, None] * omega[None, :]
        return np.concatenate([np.sin(ang), np.cos(ang)], axis=-1)

    pe = np.concatenate([sincos(r, D_MODEL // 2), sincos(col, D_MODEL // 2)], axis=-1)
    return jnp.asarray(pe, F32)


def _state_pack_t(state):
    packed = jnp.transpose(state, (0, 3, 1, 2)).reshape(N_LAT_SEQ, DV, D_K)
    return jnp.concatenate([jnp.zeros((1, DV, D_K), F32), packed], axis=0)


def _state_unpack_t(st):
    return jnp.transpose(st.reshape(N_CTX_SEQ, DV, HEADS, DK), (0, 2, 3, 1))[:, None]


def kernel(x_prompt, x_sample, c, state_fwd, state_bwd, c_ctx, w_ada, b_ada, w_in, pool_w, pool_scale,
           gk_w, gk_b, gla_norm_g, w_o, ln1_g, ln1_b, peer_wq, peer_keys, peer_u, peer_v, ln2_g, ln2_b):
    xp2 = x_prompt.reshape(N_CTX_SEQ * CTX_LEN, D_MODEL)
    xs2 = x_sample.reshape(N_LAT_SEQ * LAT_LEN, D_MODEL)
    pos = _grid_pos_embed()

    cond8 = jnp.zeros((8, D_MODEL), F32).at[0].set(c_ctx).at[1:3].set(c)
    mod = _mod_call(cond8, w_ada[0], b_ada[0][None, :])

    w = w_in[0]
    w_in_r = jnp.concatenate([w[:, :1536], w[:, 1568:], w[:, 1536:1568],
                              jnp.zeros((D_MODEL, P_COLS - 2080), F32)], axis=1).astype(BF16)
    gkw_bd = jnp.zeros((128, 2 * D_K), F32)
    gkw_bd = gkw_bd.at[0:GATE_RANK, 0:D_K].set(gk_w[0, 0]).at[GATE_RANK:2 * GATE_RANK, D_K:].set(gk_w[0, 1])
    gkb = gk_b[0].reshape(1, 2 * D_K)
    gkw2 = jnp.concatenate([gkw_bd, gkw_bd], axis=0).astype(BF16)
    xpool, q, k, v, gk, ogs = _proj_call(xp2, xs2, pos, mod, w_in_r, gkw2, gkb)

    o_f, st_f, wqt = _gla_fwd_call(q, k, v, gk, _state_pack_t(state_fwd[:, 0]), _gla_constants(False),
                                   peer_wq[0])
    x1, st_b = _gla_bwd_call(
        q, k, v, gk, _state_pack_t(state_bwd[:, 0]), _gla_constants(True), o_f, xpool, ogs,
        xp2, xs2, pos, mod, _pool_band(), pool_w[0].astype(BF16), pool_scale[0][None, :],
        gla_norm_g[0][None, :], w_o[0].astype(BF16), ln1_g[0][None, :], ln1_b[0][None, :])

    keys = peer_keys[0].reshape(2 * PEER_HEADS, N_KEYS, N_KEYS).astype(BF16)
    h2t, cnt, g1, r2, e2, u_bf, vt_bf = _route_call(x1, mod, wqt, keys, peer_u[0], peer_v[0])
    yp, ys = _dense_call(h2t, cnt, g1, r2, e2, u_bf, vt_bf, x1, mod, ln2_g[0][None, :], ln2_b[0][None, :])

    y_prompt = yp.reshape(N_CTX_SEQ, CTX_LEN, D_MODEL)
    y_sample = ys.reshape(N_LAT_SEQ, LAT_LEN, D_MODEL)
    return (y_prompt, y_sample, _state_unpack_t(st_f), _state_unpack_t(st_b))
```

```python
import math

import numpy as np
import jax
import jax.numpy as jnp
from jax import lax
from jax.experimental import pallas as pl
from jax.experimental.pallas import tpu as pltpu

F32 = jnp.float32
BF16 = jnp.bfloat16

D_MODEL = 1024
N_CTX_SEQ, CTX_LEN = 16, 256
N_LAT_SEQ, LAT_LEN = 2, 2048
GRID_W = 64
N_TOK = N_CTX_SEQ * CTX_LEN + N_LAT_SEQ * LAT_LEN
TILE = 256
N_TILES = N_TOK // TILE
CTX_TILES = N_CTX_SEQ * CTX_LEN // TILE
LAT_TILES = LAT_LEN // TILE

D_POOL = 512
POOL_GROUPS = 4
POOL_CH = 128
POOL_WINDOWS = (2, 4, 8, 16)
POOL_HALO = 8
HEADS = 4
DK = 64
DV = 128
D_K = HEADS * DK
D_V = HEADS * DV
GATE_RANK = 16
GATE_NORM = 16.0
CHUNK = 64
N_LEVELS = 7
P_COLS = 2048 + 128

PEER_HEADS = 8
N_KEYS = 128
PEER_TOPK = 16
N_EXPERTS = N_KEYS * N_KEYS
WQ_SLABS = PEER_HEADS * 2 * N_KEYS // 128
DENSE_TM = 512
DENSE_CH = 2048
DENSE_CTX_TILES = N_CTX_SEQ * CTX_LEN // DENSE_TM
DENSE_PIECES = 16
MOD_COLS = 1536
ALPHA = 2.0 ** 0.25
EPS = 1e-5
V7X_VMEM_BYTES = 64 * 1024 * 1024
VMEM_LIMIT = V7X_VMEM_BYTES * 7 // 8


def _cparams(sem):
    return pltpu.CompilerParams(dimension_semantics=sem, vmem_limit_bytes=VMEM_LIMIT)


def _silu(x):
    return x * jax.nn.sigmoid(x)


def _split_bf16(x, pieces):
    out = []
    for _ in range(pieces - 1):
        hi = x.astype(BF16)
        out.append(hi)
        x = x - hi.astype(F32)
    out.append(x.astype(BF16))
    return out


def _dot_exact(m01, x, pieces=3):
    c = x.shape[1]
    xs = jnp.concatenate(_split_bf16(x, pieces), axis=1)
    y = jnp.dot(m01, xs, preferred_element_type=F32)
    out = y[:, :c]
    for p in range(1, pieces):
        out = out + y[:, p * c:(p + 1) * c]
    return out


def _layer_norm(y, g, b):
    mu = jnp.mean(y, axis=-1, keepdims=True)
    yc = y - mu
    var = jnp.mean(yc * yc, axis=-1, keepdims=True)
    return yc * lax.rsqrt(var + EPS) * g + b


def _cond_row(tile):
    return (tile >= CTX_TILES).astype(jnp.int32) + (tile >= CTX_TILES + LAT_TILES).astype(jnp.int32)


def _mod_kernel(cond_ref, w_ref, b_ref, out_ref):
    s = _silu(cond_ref[...]).astype(BF16)
    out_ref[...] = jnp.dot(s, w_ref[...].astype(BF16), preferred_element_type=F32) + b_ref[...]


def _mod_call(cond8, w_ada, b_ada):
    n = w_ada.shape[1]
    bn = MOD_COLS
    return pl.pallas_call(
        _mod_kernel,
        grid=(n // bn,),
        in_specs=[pl.BlockSpec((8, D_MODEL), lambda j: (0, 0)),
                  pl.BlockSpec((D_MODEL, bn), lambda j: (0, j)),
                  pl.BlockSpec((1, bn), lambda j: (0, j))],
        out_specs=pl.BlockSpec((8, bn), lambda j: (0, j)),
        out_shape=jax.ShapeDtypeStruct((8, n), F32),
        compiler_params=_cparams(("arbitrary",)),
        name="mod",
    )(cond8, w_ada, b_ada)


def _load_x(i, xp_ref, xs_ref, pos_ref):
    return jnp.where(i < CTX_TILES, xp_ref[...], xs_ref[...] + pos_ref[...])


def _mod_rows(mod_ref, r, k):
    return mod_ref[pl.ds(r, 1), k * D_MODEL:(k + 1) * D_MODEL]


def _proj_kernel(xp_ref, xs_ref, pos_ref, mod_ref, w_ref, gkw_ref, gkb_ref,
                 oxp, oq, ok, ov, ogk, oog):
    i = pl.program_id(0)
    r = _cond_row(i)
    x = _load_x(i, xp_ref, xs_ref, pos_ref)
    h = x * (1.0 + _mod_rows(mod_ref, r, 1)) + _mod_rows(mod_ref, r, 0)
    p = jnp.dot(h.astype(BF16), w_ref[...], preferred_element_type=F32)
    oxp[...] = p[:, 0:512]
    oq[...] = p[:, 512:768] * (DK ** -0.5)
    ok[...] = p[:, 768:1024]
    ov[...] = p[:, 1024:1536]
    oog[...] = _silu(p[:, 1536:2048])
    glr = p[:, 2048:2176]
    g2 = jnp.concatenate(_split_bf16(glr, 2), axis=1)
    pre = jnp.dot(g2, gkw_ref[...], preferred_element_type=F32) + gkb_ref[...]
    ogk[...] = (jnp.minimum(pre, 0.0) - jnp.log(1.0 + jnp.exp(-jnp.abs(pre)))) * (1.0 / GATE_NORM)


def _x_specs():
    return [pl.BlockSpec((TILE, D_MODEL), lambda i: (jnp.minimum(i, CTX_TILES - 1), 0)),
            pl.BlockSpec((TILE, D_MODEL), lambda i: (jnp.maximum(i - CTX_TILES, 0), 0)),
            pl.BlockSpec((TILE, D_MODEL), lambda i: (jnp.maximum(i - CTX_TILES, 0) % LAT_TILES, 0))]


def _proj_call(xp2, xs2, pos, mod, w_in_r, gkw_bd, gkb):
    tok = lambda c: pl.BlockSpec((TILE, c), lambda i: (i, 0))
    full = lambda a: pl.BlockSpec(a.shape, lambda i: (0,) * a.ndim)
    outs = [512, 256, 256, 512, 512, 512]
    return pl.pallas_call(
        _proj_kernel,
        grid=(N_TILES,),
        in_specs=_x_specs() + [full(mod), full(w_in_r), full(gkw_bd), full(gkb)],
        out_specs=[tok(c) for c in outs],
        out_shape=[jax.ShapeDtypeStruct((N_TOK, c), F32) for c in outs],
        compiler_params=_cparams(("arbitrary",)),
        name="proj",
    )(xp2, xs2, pos, mod, w_in_r, gkw_bd, gkb)


def _gla_constants(rev):
    t = np.arange(TILE)
    same_chunk = (t[:, None] // CHUNK) == (t[None, :] // CHUNK)
    tri = same_chunk & ((t[None, :] >= t[:, None]) if rev else (t[None, :] <= t[:, None]))
    u = np.arange(CHUNK)
    x = u[:, None] ^ u[None, :]
    lvl = np.where(x > 0, np.floor(np.log2(np.maximum(x, 1))), N_LEVELS - 1).astype(np.int32)
    causal = (u[:, None] <= u[None, :]) if rev else (u[:, None] >= u[None, :])
    lvl = np.where(causal, lvl, -1).astype(np.int32)
    return jnp.asarray(tri, BF16), jnp.asarray(np.tile(lvl, (1, HEADS)), jnp.int32)


def _level_ref(b_scr, level, rev):
    half = 1 << level
    sub = lax.broadcasted_iota(jnp.int32, (8, D_K), 0)
    row = lambda r: jnp.broadcast_to(b_scr[r:r + 1, :], (8, D_K))
    pieces = []
    for m in range(TILE // 8):
        refs = [((8 * m + u) // (2 * half)) * (2 * half) + (half if rev else half - 1) for u in range(8)]
        piece = row(refs[0])
        for u in range(1, 8):
            if refs[u] != refs[u - 1]:
                piece = jnp.where(sub >= u, row(refs[u]), piece)
        pieces.append(piece)
    return jnp.concatenate(pieces, axis=0)


def _head_blocks(x, width):
    head = lax.broadcasted_iota(jnp.int32, x.shape, 1) // width
    return jnp.concatenate([jnp.where(head == h, x, jnp.zeros_like(x)) for h in range(HEADS)], axis=0)


def _gla_tile(q, k, v, g, tri, lmap, st_ref, b_scr, rev):
    b = _dot_exact(tri, g)
    b_scr[...] = b
    qds, kds = [], []
    for l in range(N_LEVELS - 1):
        d = b - _level_ref(b_scr, l, rev)
        qds.append((q * jnp.exp(jnp.minimum(d, 0.0))).astype(BF16))
        kds.append((k * jnp.exp(jnp.minimum(-d, 0.0))).astype(BF16))
    qds.append(q.astype(BF16))
    kds.append(k.astype(BF16))
    qe = (q * jnp.exp(b)).astype(BF16)
    vb = v.astype(BF16)

    cdim = (((1,), (1,)), ((), ()))
    st_head = lax.broadcasted_iota(jnp.int32, (DV, D_K), 1) // DK
    st = st_ref[...]
    outs = [None] * (TILE // CHUNK)
    for c in (reversed(range(TILE // CHUNK)) if rev else range(TILE // CHUNK)):
        rows = slice(c * CHUNK, (c + 1) * CHUNK)
        o = lax.dot_general(qe[rows], _head_blocks(st.astype(BF16), DK), cdim,
                            preferred_element_type=F32)
        p = jnp.zeros((CHUNK, HEADS * CHUNK), F32)
        for l in range(N_LEVELS):
            z = lax.dot_general(qds[l][rows], _head_blocks(kds[l][rows], DK), cdim,
                                preferred_element_type=F32)
            p = jnp.where(lmap == l, z, p)
        outs[c] = o + jnp.dot(p.astype(BF16), _head_blocks(vb[rows], DV), preferred_element_type=F32)

        b_c = b[rows]
        b_end = b_c[0:1, :] if rev else b_c[CHUNK - 1:CHUNK, :]
        kdec = (k[rows] * jnp.exp(b_end - b_c)).astype(BF16)
        kv = lax.dot_general(vb[rows], kdec, (((0,), (0,)), ((), ())), preferred_element_type=F32)
        kvd = kv[0:DV]
        for h in range(1, HEADS):
            kvd = jnp.where(st_head == h, kv[h * DV:(h + 1) * DV], kvd)
        st = st * jnp.exp(b_end) + kvd
    st_ref[...] = st
    return jnp.concatenate(outs, axis=0)


def _gla_fwd_kernel(q_ref, k_ref, v_ref, g_ref, st0_ref, tri_ref, lmap_ref, wq_ref,
                    of_ref, stout_ref, wqt_ref, st_scr, b_scr):
    i = pl.program_id(0)
    seq_start = (i <= CTX_TILES) | (i == CTX_TILES + LAT_TILES)

    @pl.when(seq_start)
    def _():
        st_scr[...] = st0_ref[0]

    @pl.when(i < WQ_SLABS)
    def _():
        wqt_ref[...] = wq_ref[...].T.astype(BF16)

    of_ref[...] = _gla_tile(q_ref[...], k_ref[...], v_ref[...], g_ref[...], tri_ref[...],
                            lmap_ref[...], st_scr, b_scr, rev=False)

    @pl.when(i < CTX_TILES)
    def _():
        stout_ref[0] = st_scr[...]


def _gla_fwd_call(q, k, v, gk, st0, consts, wq):
    tri, lmap = consts
    tok = lambda c: pl.BlockSpec((TILE, c), lambda i: (i, 0))
    full = lambda a: pl.BlockSpec(a.shape, lambda i: (0,) * a.ndim)
    slab = lambda i: jnp.minimum(i, WQ_SLABS - 1)
    return pl.pallas_call(
        _gla_fwd_kernel,
        grid=(N_TILES,),
        in_specs=[tok(D_K), tok(D_K), tok(D_V), pl.BlockSpec((TILE, D_K), lambda i: (i, 0)),
                  pl.BlockSpec((1, DV, D_K), lambda i: (_cond_row(i), 0, 0)),
                  full(tri), full(lmap),
                  pl.BlockSpec((D_MODEL, 128), lambda i: (0, slab(i)))],
        out_specs=[tok(D_V),
                   pl.BlockSpec((1, DV, D_K), lambda i: (jnp.minimum(i, CTX_TILES - 1), 0, 0)),
                   pl.BlockSpec((128, D_MODEL), lambda i: (slab(i), 0))],
        out_shape=[jax.ShapeDtypeStruct((N_TOK, D_V), F32),
                   jax.ShapeDtypeStruct((N_CTX_SEQ, DV, D_K), F32),
                   jax.ShapeDtypeStruct((WQ_SLABS * 128, D_MODEL), BF16)],
        scratch_shapes=[pltpu.VMEM((DV, D_K), F32), pltpu.VMEM((TILE, D_K), F32)],
        compiler_params=_cparams(("arbitrary",)),
        name="gla_fwd",
    )(q, k, v, gk, st0, tri, lmap, wq)


def _pool_band():
    t = np.arange(TILE)[:, None]
    e = np.arange(TILE + 2 * POOL_HALO)[None, :]
    bands = []
    for w in POOL_WINDOWS:
        lo = t + POOL_HALO - w // 2
        bands.append(((e >= lo) & (e < lo + w)).astype(np.float32))
    return jnp.asarray(np.stack(bands), BF16)


def _gla_bwd_kernel(q_ref, k_ref, v_ref, g_ref, st0_ref, tri_ref, lmap_ref,
                    of_ref, xpool_ref, prev_ref, next_ref, ogs_ref, xp_ref, xs_ref, pos_ref, mod_ref,
                    band_ref, poolw_ref, pscale_ref, gnorm_ref, wo_ref, ln_g_ref, ln_b_ref,
                    x1_ref, stout_ref, st_scr, b_scr):
    j = N_TILES - 1 - pl.program_id(0)
    lat_idx = jnp.maximum(j - CTX_TILES, 0) % LAT_TILES
    is_ctx = j < CTX_TILES
    seq_first = is_ctx | (lat_idx == 0)
    seq_last = is_ctx | (lat_idx == LAT_TILES - 1)

    @pl.when(seq_last)
    def _():
        st_scr[...] = st0_ref[0]

    o_b = _gla_tile(q_ref[...], k_ref[...], v_ref[...], g_ref[...], tri_ref[...],
                    lmap_ref[...], st_scr, b_scr, rev=True)

    @pl.when(is_ctx)
    def _():
        stout_ref[0] = st_scr[...]

    xpool = xpool_ref[...]
    prev = jnp.where(seq_first, 0.0, prev_ref[...])
    nxt = jnp.where(seq_last, 0.0, next_ref[...])
    ext = jnp.concatenate([prev, xpool, nxt], axis=0)
    tpos = jnp.where(is_ctx, 0, lat_idx) * TILE + lax.broadcasted_iota(jnp.int32, (TILE, POOL_CH), 0)
    seq_len = jnp.where(is_ctx, CTX_LEN, LAT_LEN)
    pool_parts = []
    for gi, w in enumerate(POOL_WINDOWS):
        sl = slice(gi * POOL_CH, (gi + 1) * POOL_CH)
        wsum = _dot_exact(band_ref[gi], ext[:, sl], pieces=2)
        lo = jnp.maximum(tpos - w // 2, 0)
        hi = jnp.minimum(tpos - w // 2 + w, seq_len)
        dmean = wsum / (hi - lo).astype(F32) - xpool[:, sl]
        y = jnp.dot(dmean.astype(BF16), poolw_ref[gi], preferred_element_type=F32)
        pool_parts.append(y * pscale_ref[:, sl])

    o = of_ref[...] + o_b
    ogs = ogs_ref[...]
    for h in range(HEADS):
        sl = slice(h * DV, (h + 1) * DV)
        oh = o[:, sl]
        oh = oh * lax.rsqrt(jnp.mean(oh * oh, axis=-1, keepdims=True) + EPS)
        pool_parts.append(oh * gnorm_ref[:, sl] * ogs[:, sl])
    mix_in = jnp.concatenate(pool_parts, axis=1).astype(BF16)
    mix = jnp.dot(mix_in, wo_ref[...], preferred_element_type=F32)

    r = _cond_row(j)
    x = _load_x(j, xp_ref, xs_ref, pos_ref)
    y = ALPHA * x + _mod_rows(mod_ref, r, 2) * mix
    x1_ref[...] = _layer_norm(y, ln_g_ref[...], ln_b_ref[...])


def _gla_bwd_call(q, k, v, gk, st0, consts, o_f, xpool, ogs, xp2, xs2, pos, mod,
                  band, poolw, pscale, gnorm, wo, ln_g, ln_b):
    tri, lmap = consts
    rv = lambda i: N_TILES - 1 - i
    tok = lambda c: pl.BlockSpec((TILE, c), lambda i: (rv(i), 0))
    full = lambda a: pl.BlockSpec(a.shape, lambda i: (0,) * a.ndim)
    halo_blocks = TILE // POOL_HALO
    n_halo = N_TOK // POOL_HALO
    xspecs = [pl.BlockSpec((TILE, D_MODEL), lambda i: (jnp.minimum(rv(i), CTX_TILES - 1), 0)),
              pl.BlockSpec((TILE, D_MODEL), lambda i: (jnp.maximum(rv(i) - CTX_TILES, 0), 0)),
              pl.BlockSpec((TILE, D_MODEL), lambda i: (jnp.maximum(rv(i) - CTX_TILES, 0) % LAT_TILES, 0))]
    return pl.pallas_call(
        _gla_bwd_kernel,
        grid=(N_TILES,),
        in_specs=[tok(D_K), tok(D_K), tok(D_V), pl.BlockSpec((TILE, D_K), lambda i: (rv(i), 1)),
                  pl.BlockSpec((1, DV, D_K), lambda i: (_cond_row(rv(i)), 0, 0)),
                  full(tri), full(lmap),
                  tok(D_V), tok(D_POOL),
                  pl.BlockSpec((POOL_HALO, D_POOL), lambda i: (jnp.maximum(rv(i) * halo_blocks - 1, 0), 0)),
                  pl.BlockSpec((POOL_HALO, D_POOL),
                               lambda i: (jnp.minimum((rv(i) + 1) * halo_blocks, n_halo - 1), 0)),
                  tok(D_V)] + xspecs + [full(mod), full(band), full(poolw), full(pscale), full(gnorm),
                                        full(wo), full(ln_g), full(ln_b)],
        out_specs=[tok(D_MODEL),
                   pl.BlockSpec((1, DV, D_K), lambda i: (jnp.minimum(rv(i), CTX_TILES - 1), 0, 0))],
        out_shape=[jax.ShapeDtypeStruct((N_TOK, D_MODEL), F32),
                   jax.ShapeDtypeStruct((N_CTX_SEQ, DV, D_K), F32)],
        scratch_shapes=[pltpu.VMEM((DV, D_K), F32), pltpu.VMEM((TILE, D_K), F32)],
        compiler_params=_cparams(("arbitrary",)),
        name="gla_bwd_mix",
    )(q, k, v, gk, st0, tri, lmap, o_f, xpool, xpool, xpool, ogs, xp2, xs2, pos, mod,
      band, poolw, pscale, gnorm, wo, ln_g, ln_b)


def _top_values(x, n, with_rank=False, first_only=False):
    vals = []
    rank = jnp.full(x.shape, float(n), F32)
    row = lax.broadcasted_iota(jnp.int32, x.shape, 0) if first_only else None
    for j in range(n):
        m = jnp.max(x, axis=0, keepdims=True)
        vals.append(m)
        hit = x == m
        if first_only:
            hit = row == jnp.min(jnp.where(hit, row, x.shape[0]), axis=0, keepdims=True)
        if with_rank:
            rank = jnp.where(hit, float(j), rank)
        x = jnp.where(hit, -jnp.inf, x)
    return (vals, rank) if with_rank else vals


def _route_select(s1, s2, exact_ties):
    n_tok = s1.shape[1]
    sv1, rank1 = _top_values(s1, PEER_TOPK, with_rank=True, first_only=exact_ties)
    sv2, rank2 = _top_values(s2, PEER_TOPK, with_rank=True, first_only=exact_ties)
    a1 = jnp.concatenate(sv1, axis=0)
    a2 = jnp.concatenate(sv2, axis=0)
    row8 = lax.broadcasted_iota(jnp.int32, (8, n_tok), 0)
    cand = [a2 + sv1[0]]
    for a in range(1, 8):
        cand.append(jnp.where(row8 < PEER_TOPK // (a + 1), a2[0:8] + sv1[a], -jnp.inf))
    cand.append(a1[8:16] + sv2[0])
    cand_all = jnp.concatenate(cand, axis=0)
    if exact_ties:
        fv, crank = _top_values(cand_all, PEER_TOPK, with_rank=True, first_only=True)
        taken = (crank < float(PEER_TOPK)).astype(F32)
        blocks = [taken[0:16]] + [taken[8 + 8 * a:16 + 8 * a] for a in range(1, 8)]
        blocks += [taken[72 + a:73 + a] for a in range(8)]
    else:
        fv = _top_values(cand_all, PEER_TOPK)
        thr = fv[PEER_TOPK - 1]
        blocks = [(cand[a] >= thr).astype(F32) for a in range(8)]
        blocks += [(cand[8][a:a + 1] >= thr).astype(F32) for a in range(8)]
    denom = jnp.zeros_like(fv[0])
    for f in fv:
        denom = denom + jnp.exp(f - fv[0])
    rank1 = rank1.astype(BF16)
    rank2 = rank2.astype(BF16)
    cnt = jnp.zeros_like(rank1)
    n_sel = jnp.zeros((1, n_tok), F32)
    for a in range(PEER_TOPK):
        n_a = jnp.sum(blocks[a], axis=0, keepdims=True)
        n_sel = n_sel + n_a
        cnt = jnp.where(rank1 == float(a), n_a.astype(BF16), cnt)
    g1 = (jnp.exp(s1 - sv1[0]) / denom).astype(BF16)
    e2 = jnp.exp(s2 - sv2[0]).astype(BF16)
    if exact_ties:
        return cnt, g1, rank2, e2, jnp.zeros((1, n_tok), F32)
    took = lambda rk: jnp.sum((rk < float(PEER_TOPK)).astype(F32), axis=0, keepdims=True)
    k = float(PEER_TOPK)
    tied = ((took(rank1) != k) | (took(rank2) != k) | (n_sel != k)).astype(F32)
    return cnt, g1, rank2, e2, tied


def _route_kernel(x1_ref, mod_ref, wqt_ref, keys_ref, u_ref, v_ref,
                  h2t_ref, cnt_ref, g1_ref, r2_ref, e2_ref, ubf_ref, vtbf_ref, s_scr, tied_ref):
    i = pl.program_id(0)
    r = _cond_row(i)
    ubf_ref[...] = u_ref[...].astype(BF16)
    vtbf_ref[0] = v_ref[...].T.astype(BF16)
    h2 = x1_ref[...] * (1.0 + _mod_rows(mod_ref, r, 4)) + _mod_rows(mod_ref, r, 3)
    h2t = h2.T.astype(BF16)
    h2t_ref[...] = h2t
    qt = jnp.dot(wqt_ref[...], h2t, preferred_element_type=F32)

    def store(h, sel):
        cnt_ref[h], g1_ref[h], r2_ref[h], e2_ref[h] = sel[0], sel[1], sel[2], sel[3]

    for h in range(PEER_HEADS):
        s = []
        for p in range(2):
            row = (2 * h + p) * N_KEYS
            qhp = qt[row:row + N_KEYS, :].astype(BF16)
            s.append(jnp.dot(keys_ref[2 * h + p], qhp, preferred_element_type=F32))
            s_scr[2 * h + p] = s[p]
        sel = _route_select(s[0], s[1], exact_ties=False)
        store(h, sel)
        tied_ref[h] = jnp.max(sel[4])

    def redo(h, carry):
        @pl.when(tied_ref[h] > 0.0)
        def _():
            store(h, _route_select(s_scr[2 * h], s_scr[2 * h + 1], exact_ties=True))
        return carry
    lax.fori_loop(0, PEER_HEADS, redo, 0)


def _route_call(x1, mod, wqt, keys, u, v):
    full = lambda a: pl.BlockSpec(a.shape, lambda i: (0,) * a.ndim)
    hk = pl.BlockSpec((PEER_HEADS, N_KEYS, TILE), lambda i: (0, 0, i))
    per_key = lambda dt: jax.ShapeDtypeStruct((PEER_HEADS, N_KEYS, N_TOK), dt)
    slab = N_EXPERTS // N_TILES
    slabs_per_block = DENSE_CH // slab
    table = pl.BlockSpec((slab, D_MODEL), lambda i: (i, 0))
    return pl.pallas_call(
        _route_kernel,
        grid=(N_TILES,),
        in_specs=[pl.BlockSpec((TILE, D_MODEL), lambda i: (i, 0)), full(mod), full(wqt), full(keys),
                  table, table],
        out_specs=[pl.BlockSpec((D_MODEL, TILE), lambda i: (0, i)), hk, hk, hk, hk, table,
                   pl.BlockSpec((1, D_MODEL, slab), lambda i: (i // slabs_per_block, 0, i % slabs_per_block))],
        out_shape=[jax.ShapeDtypeStruct((D_MODEL, N_TOK), BF16),
                   per_key(BF16), per_key(BF16), per_key(BF16), per_key(BF16),
                   jax.ShapeDtypeStruct((N_EXPERTS, D_MODEL), BF16),
                   jax.ShapeDtypeStruct((N_EXPERTS // DENSE_CH, D_MODEL, DENSE_CH), BF16)],
        scratch_shapes=[pltpu.VMEM((2 * PEER_HEADS, N_KEYS, TILE), F32), pltpu.SMEM((PEER_HEADS,), F32)],
        compiler_params=_cparams(("arbitrary",)),
        name="peer_route",
    )(x1, mod, wqt, keys, u, v)


def _gelu_tanh(x):
    c1 = math.sqrt(2.0 / math.pi)
    c2 = c1 * 0.044715
    return x * (0.5 + 0.5 * jnp.tanh(x * (c1 + c2 * (x * x))))


def _dense_kernel(h2t_ref, cnt_ref, g1_ref, r2_ref, e2_ref, u_ref, vt_ref, x1_ref, mod_ref,
                  ln_g_ref, ln_b_ref, yp_ref, ys_ref, acc_ref):
    tt = pl.program_id(0)
    c = pl.program_id(1)
    n_sub = DENSE_CH // N_KEYS
    assert n_sub % 16 == 0, "count / gate rows are read as whole packed bf16 row groups"

    @pl.when(c == 0)
    def _():
        acc_ref[...] = jnp.zeros_like(acc_ref)

    i1_base = pl.multiple_of(c * n_sub, n_sub)
    cnt8 = [cnt_ref[h, pl.ds(i1_base, n_sub), :] for h in range(PEER_HEADS)]
    g18 = [g1_ref[h, pl.ds(i1_base, n_sub), :] for h in range(PEER_HEADS)]
    def gate(ci):
        wsum = jnp.zeros((N_KEYS, DENSE_TM), BF16)
        for h in range(PEER_HEADS):
            sel = r2_ref[h] < cnt8[h][ci:ci + 1, :]
            wsum = wsum + jnp.where(sel, e2_ref[h], jnp.zeros((), BF16)) * g18[h][ci:ci + 1, :]
        return wsum

    sub_per_piece = n_sub // DENSE_PIECES
    gates, a_pieces = [], []
    for pc in range(DENSE_PIECES):
        rows = slice(pc * sub_per_piece * N_KEYS, (pc + 1) * sub_per_piece * N_KEYS)
        lhs = u_ref[rows, :]
        if pc > 0:
            zero = gates[-1][0:16, 0:128] * jnp.zeros((), BF16)
            top = jnp.concatenate([lhs[0:16, 0:128] + zero, lhs[0:16, 128:]], axis=1)
            lhs = jnp.concatenate([top, lhs[16:]], axis=0)
        a_pieces.append(jnp.dot(lhs, h2t_ref[...], preferred_element_type=F32))
        gates.extend(gate(ci) for ci in range(pc * sub_per_piece, (pc + 1) * sub_per_piece))
    a = jnp.concatenate(a_pieces, axis=0)
    w_parts = [_gelu_tanh(a[ci * N_KEYS:(ci + 1) * N_KEYS, :].astype(BF16)) * gates[ci] for ci in range(n_sub)]
    zero = gates[-1][0:16, 0:128] * jnp.zeros((), BF16)
    w0 = w_parts[0]
    top = jnp.concatenate([w0[0:16, 0:128] + zero, w0[0:16, 128:]], axis=1)
    w_parts[0] = jnp.concatenate([top, w0[16:]], axis=0)
    w = jnp.concatenate(w_parts, axis=0)
    acc_ref[...] += jnp.dot(vt_ref[0], w, preferred_element_type=F32)

    @pl.when(c == pl.num_programs(1) - 1)
    def _():
        r = _cond_row(tt * (DENSE_TM // TILE))
        ffn = acc_ref[...].T
        y = _layer_norm(ALPHA * x1_ref[...] + _mod_rows(mod_ref, r, 5) * ffn, ln_g_ref[...], ln_b_ref[...])

        @pl.when(tt < DENSE_CTX_TILES)
        def _():
            yp_ref[...] = y

        @pl.when(tt >= DENSE_CTX_TILES)
        def _():
            ys_ref[...] = y


def _dense_call(h2t, cnt, g1, r2, e2, u_bf, vt_bf, x1, mod, ln_g, ln_b):
    full = lambda a: pl.BlockSpec(a.shape, lambda t, c: (0,) * a.ndim)
    hk = pl.BlockSpec((PEER_HEADS, N_KEYS, DENSE_TM), lambda t, c: (0, 0, t))
    n_half = N_CTX_SEQ * CTX_LEN
    return pl.pallas_call(
        _dense_kernel,
        grid=(N_TOK // DENSE_TM, N_EXPERTS // DENSE_CH),
        in_specs=[pl.BlockSpec((D_MODEL, DENSE_TM), lambda t, c: (0, t)),
                  hk, hk, hk, hk,
                  pl.BlockSpec((DENSE_CH, D_MODEL), lambda t, c: (c, 0)),
                  pl.BlockSpec((1, D_MODEL, DENSE_CH), lambda t, c: (c, 0, 0)),
                  pl.BlockSpec((DENSE_TM, D_MODEL), lambda t, c: (t, 0)),
                  full(mod), full(ln_g), full(ln_b)],
        out_specs=[pl.BlockSpec((DENSE_TM, D_MODEL), lambda t, c: (jnp.minimum(t, DENSE_CTX_TILES - 1), 0)),
                   pl.BlockSpec((DENSE_TM, D_MODEL), lambda t, c: (jnp.maximum(t - DENSE_CTX_TILES, 0), 0))],
        out_shape=[jax.ShapeDtypeStruct((n_half, D_MODEL), F32),
                   jax.ShapeDtypeStruct((N_TOK - n_half, D_MODEL), F32)],
        scratch_shapes=[pltpu.VMEM((D_MODEL, DENSE_TM), F32)],
        compiler_params=_cparams(("arbitrary", "arbitrary")),
        name="peer_dense",
    )(h2t, cnt, g1, r2, e2, u_bf, vt_bf, x1, mod, ln_g, ln_b)


def _grid_pos_embed():
    rows = LAT_LEN // GRID_W
    r, col = np.meshgrid(np.arange(rows), np.arange(GRID_W), indexing="ij")

    def sincos(pos, dim):
        omega = 1.0 / (10000.0 ** (np.arange(dim // 2, dtype=np.float64) / (dim // 2)))
        ang = pos.reshape(-1).astype(np.float64)[:, None] * omega[None, :]
        return np.concatenate([np.sin(ang), np.cos(ang)], axis=-1)

    pe = np.concatenate([sincos(r, D_MODEL // 2), sincos(col, D_MODEL // 2)], axis=-1)
    return jnp.asarray(pe, F32)


def _state_pack_t(state):
    packed = jnp.transpose(state, (0, 3, 1, 2)).reshape(N_LAT_SEQ, DV, D_K)
    return jnp.concatenate([jnp.zeros((1, DV, D_K), F32), packed], axis=0)


def _state_unpack_t(st):
    return jnp.transpose(st.reshape(N_CTX_SEQ, DV, HEADS, DK), (0, 2, 3, 1))[:, None]


def kernel(x_prompt, x_sample, c, state_fwd, state_bwd, c_ctx, w_ada, b_ada, w_in, pool_w, pool_scale,
           gk_w, gk_b, gla_norm_g, w_o, ln1_g, ln1_b, peer_wq, peer_keys, peer_u, peer_v, ln2_g, ln2_b):
    xp2 = x_prompt.reshape(N_CTX_SEQ * CTX_LEN, D_MODEL)
    xs2 = x_sample.reshape(N_LAT_SEQ * LAT_LEN, D_MODEL)
    pos = _grid_pos_embed()

    cond8 = jnp.zeros((8, D_MODEL), F32).at[0].set(c_ctx).at[1:3].set(c)
    mod = _mod_call(cond8, w_ada[0], b_ada[0][None, :])

    w = w_in[0]
    w_in_r = jnp.concatenate([w[:, :1536], w[:, 1568:], w[:, 1536:1568],
                              jnp.zeros((D_MODEL, P_COLS - 2080), F32)], axis=1).astype(BF16)
    gkw_bd = jnp.zeros((128, 2 * D_K), F32)
    gkw_bd = gkw_bd.at[0:GATE_RANK, 0:D_K].set(gk_w[0, 0]).at[GATE_RANK:2 * GATE_RANK, D_K:].set(gk_w[0, 1])
    gkb = gk_b[0].reshape(1, 2 * D_K)
    gkw2 = jnp.concatenate([gkw_bd, gkw_bd], axis=0).astype(BF16)
    xpool, q, k, v, gk, ogs = _proj_call(xp2, xs2, pos, mod, w_in_r, gkw2, gkb)

    o_f, st_f, wqt = _gla_fwd_call(q, k, v, gk, _state_pack_t(state_fwd[:, 0]), _gla_constants(False),
                                   peer_wq[0])
    x1, st_b = _gla_bwd_call(
        q, k, v, gk, _state_pack_t(state_bwd[:, 0]), _gla_constants(True), o_f, xpool, ogs,
        xp2, xs2, pos, mod, _pool_band(), pool_w[0].astype(BF16), pool_scale[0][None, :],
        gla_norm_g[0][None, :], w_o[0].astype(BF16), ln1_g[0][None, :], ln1_b[0][None, :])

    keys = peer_keys[0].reshape(2 * PEER_HEADS, N_KEYS, N_KEYS).astype(BF16)
    h2t, cnt, g1, r2, e2, u_bf, vt_bf = _route_call(x1, mod, wqt, keys, peer_u[0], peer_v[0])
    yp, ys = _dense_call(h2t, cnt, g1, r2, e2, u_bf, vt_bf, x1, mod, ln2_g[0][None, :], ln2_b[0][None, :])

    y_prompt = yp.reshape(N_CTX_SEQ, CTX_LEN, D_MODEL)
    y_sample = ys.reshape(N_LAT_SEQ, LAT_LEN, D_MODEL)
    return (y_prompt, y_sample, _state_unpack_t(st_f), _state_unpack_t(st_b))
```

```python
import math

import numpy as np
import jax
import jax.numpy as jnp
from jax import lax
from jax.experimental import pallas as pl
from jax.experimental.pallas import tpu as pltpu

F32 = jnp.float32
BF16 = jnp.bfloat16

D_MODEL = 1024
N_CTX_SEQ, CTX_LEN = 16, 256
N_LAT_SEQ, LAT_LEN = 2, 2048
GRID_W = 64
N_TOK = N_CTX_SEQ * CTX_LEN + N_LAT_SEQ * LAT_LEN
TILE = 256
N_TILES = N_TOK // TILE
CTX_TILES = N_CTX_SEQ * CTX_LEN // TILE
LAT_TILES = LAT_LEN // TILE

D_POOL = 512
POOL_GROUPS = 4
POOL_CH = 128
POOL_WINDOWS = (2, 4, 8, 16)
POOL_HALO = 8
HEADS = 4
DK = 64
DV = 128
D_K = HEADS * DK
D_V = HEADS * DV
GATE_RANK = 16
GATE_NORM = 16.0
CHUNK = 64
N_LEVELS = 7
P_COLS = 2048 + 128

PEER_HEADS = 8
N_KEYS = 128
PEER_TOPK = 16
N_EXPERTS = N_KEYS * N_KEYS
WQ_SLABS = PEER_HEADS * 2 * N_KEYS // 128
DENSE_TM = 512
DENSE_CH = 2048
DENSE_CTX_TILES = N_CTX_SEQ * CTX_LEN // DENSE_TM
DENSE_PIECES = 16
MOD_COLS = 1536
ALPHA = 2.0 ** 0.25
EPS = 1e-5
V7X_VMEM_BYTES = 64 * 1024 * 1024
VMEM_LIMIT = V7X_VMEM_BYTES * 7 // 8


def _cparams(sem):
    return pltpu.CompilerParams(dimension_semantics=sem, vmem_limit_bytes=VMEM_LIMIT)


def _silu(x):
    return x * jax.nn.sigmoid(x)


def _split_bf16(x, pieces):
    out = []
    for _ in range(pieces - 1):
        hi = x.astype(BF16)
        out.append(hi)
        x = x - hi.astype(F32)
    out.append(x.astype(BF16))
    return out


def _dot_exact(m01, x, pieces=3):
    c = x.shape[1]
    xs = jnp.concatenate(_split_bf16(x, pieces), axis=1)
    y = jnp.dot(m01, xs, preferred_element_type=F32)
    out = y[:, :c]
    for p in range(1, pieces):
        out = out + y[:, p * c:(p + 1) * c]
    return out


def _layer_norm(y, g, b):
    mu = jnp.mean(y, axis=-1, keepdims=True)
    yc = y - mu
    var = jnp.mean(yc * yc, axis=-1, keepdims=True)
    return yc * lax.rsqrt(var + EPS) * g + b


def _cond_row(tile):
    return (tile >= CTX_TILES).astype(jnp.int32) + (tile >= CTX_TILES + LAT_TILES).astype(jnp.int32)


def _mod_kernel(cond_ref, w_ref, b_ref, out_ref):
    s = _silu(cond_ref[...]).astype(BF16)
    out_ref[...] = jnp.dot(s, w_ref[...].astype(BF16), preferred_element_type=F32) + b_ref[...]


def _mod_call(cond8, w_ada, b_ada):
    n = w_ada.shape[1]
    bn = MOD_COLS
    return pl.pallas_call(
        _mod_kernel,
        grid=(n // bn,),
        in_specs=[pl.BlockSpec((8, D_MODEL), lambda j: (0, 0)),
                  pl.BlockSpec((D_MODEL, bn), lambda j: (0, j)),
                  pl.BlockSpec((1, bn), lambda j: (0, j))],
        out_specs=pl.BlockSpec((8, bn), lambda j: (0, j)),
        out_shape=jax.ShapeDtypeStruct((8, n), F32),
        compiler_params=_cparams(("arbitrary",)),
        name="mod",
    )(cond8, w_ada, b_ada)


def _load_x(i, xp_ref, xs_ref, pos_ref):
    return jnp.where(i < CTX_TILES, xp_ref[...], xs_ref[...] + pos_ref[...])


def _mod_rows(mod_ref, r, k):
    return mod_ref[pl.ds(r, 1), k * D_MODEL:(k + 1) * D_MODEL]


def _proj_kernel(xp_ref, xs_ref, pos_ref, mod_ref, w_ref, gkw_ref, gkb_ref,
                 oxp, oq, ok, ov, ogk, oog):
    i = pl.program_id(0)
    r = _cond_row(i)
    x = _load_x(i, xp_ref, xs_ref, pos_ref)
    h = x * (1.0 + _mod_rows(mod_ref, r, 1)) + _mod_rows(mod_ref, r, 0)
    hb = h.astype(BF16)
    proj = lambda lo, hi: jnp.dot(hb, w_ref[:, lo:hi], preferred_element_type=F32)
    glr = proj(2048, P_COLS)
    g2 = jnp.concatenate(_split_bf16(glr, 2), axis=1)
    pre = jnp.dot(g2, gkw_ref[...], preferred_element_type=F32) + gkb_ref[...]
    ogk[...] = (jnp.minimum(pre, 0.0) - jnp.log(1.0 + jnp.exp(-jnp.abs(pre)))) * (1.0 / GATE_NORM)
    oog[...] = _silu(proj(1536, 2048))
    oq[...] = proj(512, 768) * (DK ** -0.5)
    oxp[...] = proj(0, 512)
    ok[...] = proj(768, 1024)
    ov[...] = proj(1024, 1536)


def _x_specs():
    return [pl.BlockSpec((TILE, D_MODEL), lambda i: (jnp.minimum(i, CTX_TILES - 1), 0)),
            pl.BlockSpec((TILE, D_MODEL), lambda i: (jnp.maximum(i - CTX_TILES, 0), 0)),
            pl.BlockSpec((TILE, D_MODEL), lambda i: (jnp.maximum(i - CTX_TILES, 0) % LAT_TILES, 0))]


def _proj_call(xp2, xs2, pos, mod, w_in_r, gkw_bd, gkb):
    tok = lambda c: pl.BlockSpec((TILE, c), lambda i: (i, 0))
    full = lambda a: pl.BlockSpec(a.shape, lambda i: (0,) * a.ndim)
    outs = [512, 256, 256, 512, 512, 512]
    return pl.pallas_call(
        _proj_kernel,
        grid=(N_TILES,),
        in_specs=_x_specs() + [full(mod), full(w_in_r), full(gkw_bd), full(gkb)],
        out_specs=[tok(c) for c in outs],
        out_shape=[jax.ShapeDtypeStruct((N_TOK, c), F32) for c in outs],
        compiler_params=_cparams(("arbitrary",)),
        name="proj",
    )(xp2, xs2, pos, mod, w_in_r, gkw_bd, gkb)


def _gla_constants(rev):
    t = np.arange(TILE)
    same_chunk = (t[:, None] // CHUNK) == (t[None, :] // CHUNK)
    tri = same_chunk & ((t[None, :] >= t[:, None]) if rev else (t[None, :] <= t[:, None]))
    u = np.arange(CHUNK)
    x = u[:, None] ^ u[None, :]
    lvl = np.where(x > 0, np.floor(np.log2(np.maximum(x, 1))), N_LEVELS - 1).astype(np.int32)
    causal = (u[:, None] <= u[None, :]) if rev else (u[:, None] >= u[None, :])
    lvl = np.where(causal, lvl, -1).astype(np.int32)
    return jnp.asarray(tri, BF16), jnp.asarray(np.tile(lvl, (1, HEADS)), jnp.int32)


def _level_ref(b_scr, level, rev):
    half = 1 << level
    sub = lax.broadcasted_iota(jnp.int32, (8, D_K), 0)
    row = lambda r: jnp.broadcast_to(b_scr[r:r + 1, :], (8, D_K))
    pieces = []
    for m in range(TILE // 8):
        refs = [((8 * m + u) // (2 * half)) * (2 * half) + (half if rev else half - 1) for u in range(8)]
        piece = row(refs[0])
        for u in range(1, 8):
            if refs[u] != refs[u - 1]:
                piece = jnp.where(sub >= u, row(refs[u]), piece)
        pieces.append(piece)
    return jnp.concatenate(pieces, axis=0)


def _head_blocks(x, width):
    head = lax.broadcasted_iota(jnp.int32, x.shape, 1) // width
    return jnp.concatenate([jnp.where(head == h, x, jnp.zeros_like(x)) for h in range(HEADS)], axis=0)


def _gla_tile(q, k, v, g, tri, lmap, st_ref, b_scr, rev):
    b = _dot_exact(tri, g)
    b_scr[...] = b
    qds, kds = [], []
    for l in range(N_LEVELS - 1):
        d = b - _level_ref(b_scr, l, rev)
        qds.append((q * jnp.exp(jnp.minimum(d, 0.0))).astype(BF16))
        kds.append((k * jnp.exp(jnp.minimum(-d, 0.0))).astype(BF16))
    qds.append(q.astype(BF16))
    kds.append(k.astype(BF16))
    qe = (q * jnp.exp(b)).astype(BF16)
    vb = v.astype(BF16)

    cdim = (((1,), (1,)), ((), ()))
    st_head = lax.broadcasted_iota(jnp.int32, (DV, D_K), 1) // DK
    n_chunks = TILE // CHUNK
    rows = [slice(c * CHUNK, (c + 1) * CHUNK) for c in range(n_chunks)]
    b_ends, kvds = [], []
    for c in range(n_chunks):
        b_c = b[rows[c]]
        b_end = b_c[0:1, :] if rev else b_c[CHUNK - 1:CHUNK, :]
        kdec = (k[rows[c]] * jnp.exp(b_end - b_c)).astype(BF16)
        kv = lax.dot_general(vb[rows[c]], kdec, (((0,), (0,)), ((), ())), preferred_element_type=F32)
        kvd = kv[0:DV]
        for h in range(1, HEADS):
            kvd = jnp.where(st_head == h, kv[h * DV:(h + 1) * DV], kvd)
        b_ends.append(b_end)
        kvds.append(kvd)
    ps = []
    for c in range(n_chunks):
        p = jnp.zeros((CHUNK, HEADS * CHUNK), F32)
        for l in range(N_LEVELS):
            z = lax.dot_general(qds[l][rows[c]], _head_blocks(kds[l][rows[c]], DK), cdim,
                                preferred_element_type=F32)
            p = jnp.where(lmap == l, z, p)
        ps.append(p.astype(BF16))
    intra = [jnp.dot(ps[c], _head_blocks(vb[rows[c]], DV), preferred_element_type=F32) for c in range(n_chunks)]
    st = st_ref[...]
    outs = [None] * n_chunks
    for c in (reversed(range(n_chunks)) if rev else range(n_chunks)):
        o = lax.dot_general(qe[rows[c]], _head_blocks(st.astype(BF16), DK), cdim,
                            preferred_element_type=F32)
        outs[c] = o + intra[c]
        st = st * jnp.exp(b_ends[c]) + kvds[c]
    st_ref[...] = st
    return jnp.concatenate(outs, axis=0)


def _gla_fwd_kernel(q_ref, k_ref, v_ref, g_ref, st0_ref, tri_ref, lmap_ref, wq_ref,
                    of_ref, stout_ref, wqt_ref, st_scr, b_scr):
    i = pl.program_id(0)
    seq_start = (i <= CTX_TILES) | (i == CTX_TILES + LAT_TILES)

    @pl.when(seq_start)
    def _():
        st_scr[...] = st0_ref[0]

    @pl.when(i < WQ_SLABS)
    def _():
        wqt_ref[...] = wq_ref[...].T.astype(BF16)

    of_ref[...] = _gla_tile(q_ref[...], k_ref[...], v_ref[...], g_ref[...], tri_ref[...],
                            lmap_ref[...], st_scr, b_scr, rev=False)

    @pl.when(i < CTX_TILES)
    def _():
        stout_ref[0] = st_scr[...]


def _gla_fwd_call(q, k, v, gk, st0, consts, wq):
    tri, lmap = consts
    tok = lambda c: pl.BlockSpec((TILE, c), lambda i: (i, 0))
    full = lambda a: pl.BlockSpec(a.shape, lambda i: (0,) * a.ndim)
    slab = lambda i: jnp.minimum(i, WQ_SLABS - 1)
    return pl.pallas_call(
        _gla_fwd_kernel,
        grid=(N_TILES,),
        in_specs=[tok(D_K), tok(D_K), tok(D_V), pl.BlockSpec((TILE, D_K), lambda i: (i, 0)),
                  pl.BlockSpec((1, DV, D_K), lambda i: (_cond_row(i), 0, 0)),
                  full(tri), full(lmap),
                  pl.BlockSpec((D_MODEL, 128), lambda i: (0, slab(i)))],
        out_specs=[tok(D_V),
                   pl.BlockSpec((1, DV, D_K), lambda i: (jnp.minimum(i, CTX_TILES - 1), 0, 0)),
                   pl.BlockSpec((128, D_MODEL), lambda i: (slab(i), 0))],
        out_shape=[jax.ShapeDtypeStruct((N_TOK, D_V), F32),
                   jax.ShapeDtypeStruct((N_CTX_SEQ, DV, D_K), F32),
                   jax.ShapeDtypeStruct((WQ_SLABS * 128, D_MODEL), BF16)],
        scratch_shapes=[pltpu.VMEM((DV, D_K), F32), pltpu.VMEM((TILE, D_K), F32)],
        compiler_params=_cparams(("arbitrary",)),
        name="gla_fwd",
    )(q, k, v, gk, st0, tri, lmap, wq)


def _pool_band():
    t = np.arange(TILE)[:, None]
    e = np.arange(TILE + 2 * POOL_HALO)[None, :]
    bands = []
    for w in POOL_WINDOWS:
        lo = t + POOL_HALO - w // 2
        bands.append(((e >= lo) & (e < lo + w)).astype(np.float32))
    return jnp.asarray(np.stack(bands), BF16)


def _gla_bwd_kernel(q_ref, k_ref, v_ref, g_ref, st0_ref, tri_ref, lmap_ref,
                    of_ref, xpool_ref, prev_ref, next_ref, ogs_ref, xp_ref, xs_ref, pos_ref, mod_ref,
                    band_ref, poolw_ref, pscale_ref, gnorm_ref, wo_ref, ln_g_ref, ln_b_ref,
                    x1_ref, stout_ref, st_scr, b_scr):
    j = N_TILES - 1 - pl.program_id(0)
    lat_idx = jnp.maximum(j - CTX_TILES, 0) % LAT_TILES
    is_ctx = j < CTX_TILES
    seq_first = is_ctx | (lat_idx == 0)
    seq_last = is_ctx | (lat_idx == LAT_TILES - 1)

    @pl.when(seq_last)
    def _():
        st_scr[...] = st0_ref[0]

    xpool = xpool_ref[...]
    prev = jnp.where(seq_first, 0.0, prev_ref[...])
    nxt = jnp.where(seq_last, 0.0, next_ref[...])
    ext = jnp.concatenate([prev, xpool, nxt], axis=0)
    tpos = jnp.where(is_ctx, 0, lat_idx) * TILE + lax.broadcasted_iota(jnp.int32, (TILE, POOL_CH), 0)
    seq_len = jnp.where(is_ctx, CTX_LEN, LAT_LEN)
    pool_parts = []
    for gi, w in enumerate(POOL_WINDOWS):
        sl = slice(gi * POOL_CH, (gi + 1) * POOL_CH)
        wsum = _dot_exact(band_ref[gi], ext[:, sl], pieces=2)
        lo = jnp.maximum(tpos - w // 2, 0)
        hi = jnp.minimum(tpos - w // 2 + w, seq_len)
        dmean = wsum / (hi - lo).astype(F32) - xpool[:, sl]
        y = jnp.dot(dmean.astype(BF16), poolw_ref[gi], preferred_element_type=F32)
        pool_parts.append(y * pscale_ref[:, sl])

    o_b = _gla_tile(q_ref[...], k_ref[...], v_ref[...], g_ref[...], tri_ref[...],
                    lmap_ref[...], st_scr, b_scr, rev=True)

    o = of_ref[...] + o_b
    ogs = ogs_ref[...]
    for h in range(HEADS):
        sl = slice(h * DV, (h + 1) * DV)
        oh = o[:, sl]
        oh = oh * lax.rsqrt(jnp.mean(oh * oh, axis=-1, keepdims=True) + EPS)
        pool_parts.append(oh * gnorm_ref[:, sl] * ogs[:, sl])
    mix_in = jnp.concatenate(pool_parts, axis=1).astype(BF16)
    mix = jnp.dot(mix_in, wo_ref[...], preferred_element_type=F32)

    r = _cond_row(j)
    x = _load_x(j, xp_ref, xs_ref, pos_ref)
    y = ALPHA * x + _mod_rows(mod_ref, r, 2) * mix
    x1_ref[...] = _layer_norm(y, ln_g_ref[...], ln_b_ref[...])

    @pl.when(is_ctx)
    def _():
        stout_ref[0] = st_scr[...]


def _gla_bwd_call(q, k, v, gk, st0, consts, o_f, xpool, ogs, xp2, xs2, pos, mod,
                  band, poolw, pscale, gnorm, wo, ln_g, ln_b):
    tri, lmap = consts
    rv = lambda i: N_TILES - 1 - i
    tok = lambda c: pl.BlockSpec((TILE, c), lambda i: (rv(i), 0))
    full = lambda a: pl.BlockSpec(a.shape, lambda i: (0,) * a.ndim)
    halo_blocks = TILE // POOL_HALO
    n_halo = N_TOK // POOL_HALO
    xspecs = [pl.BlockSpec((TILE, D_MODEL), lambda i: (jnp.minimum(rv(i), CTX_TILES - 1), 0)),
              pl.BlockSpec((TILE, D_MODEL), lambda i: (jnp.maximum(rv(i) - CTX_TILES, 0), 0)),
              pl.BlockSpec((TILE, D_MODEL), lambda i: (jnp.maximum(rv(i) - CTX_TILES, 0) % LAT_TILES, 0))]
    return pl.pallas_call(
        _gla_bwd_kernel,
        grid=(N_TILES,),
        in_specs=[tok(D_K), tok(D_K), tok(D_V), pl.BlockSpec((TILE, D_K), lambda i: (rv(i), 1)),
                  pl.BlockSpec((1, DV, D_K), lambda i: (_cond_row(rv(i)), 0, 0)),
                  full(tri), full(lmap),
                  tok(D_V), tok(D_POOL),
                  pl.BlockSpec((POOL_HALO, D_POOL), lambda i: (jnp.maximum(rv(i) * halo_blocks - 1, 0), 0)),
                  pl.BlockSpec((POOL_HALO, D_POOL),
                               lambda i: (jnp.minimum((rv(i) + 1) * halo_blocks, n_halo - 1), 0)),
                  tok(D_V)] + xspecs + [full(mod), full(band), full(poolw), full(pscale), full(gnorm),
                                        full(wo), full(ln_g), full(ln_b)],
        out_specs=[tok(D_MODEL),
                   pl.BlockSpec((1, DV, D_K), lambda i: (jnp.minimum(rv(i), CTX_TILES - 1), 0, 0))],
        out_shape=[jax.ShapeDtypeStruct((N_TOK, D_MODEL), F32),
                   jax.ShapeDtypeStruct((N_CTX_SEQ, DV, D_K), F32)],
        scratch_shapes=[pltpu.VMEM((DV, D_K), F32), pltpu.VMEM((TILE, D_K), F32)],
        compiler_params=_cparams(("arbitrary",)),
        name="gla_bwd_mix",
    )(q, k, v, gk, st0, tri, lmap, o_f, xpool, xpool, xpool, ogs, xp2, xs2, pos, mod,
      band, poolw, pscale, gnorm, wo, ln_g, ln_b)


def _top_values(x, n, with_rank=False, first_only=False):
    vals = []
    rank = jnp.full(x.shape, float(n), F32)
    row = lax.broadcasted_iota(jnp.int32, x.shape, 0) if first_only else None
    for j in range(n):
        m = jnp.max(x, axis=0, keepdims=True)
        vals.append(m)
        hit = x == m
        if first_only:
            hit = row == jnp.min(jnp.where(hit, row, x.shape[0]), axis=0, keepdims=True)
        if with_rank:
            rank = jnp.where(hit, float(j), rank)
        x = jnp.where(hit, -jnp.inf, x)
    return (vals, rank) if with_rank else vals


def _route_select(s1, s2, exact_ties):
    n_tok = s1.shape[1]
    sv1, rank1 = _top_values(s1, PEER_TOPK, with_rank=True, first_only=exact_ties)
    sv2, rank2 = _top_values(s2, PEER_TOPK, with_rank=True, first_only=exact_ties)
    a1 = jnp.concatenate(sv1, axis=0)
    a2 = jnp.concatenate(sv2, axis=0)
    row8 = lax.broadcasted_iota(jnp.int32, (8, n_tok), 0)
    cand = [a2 + sv1[0]]
    for a in range(1, 8):
        cand.append(jnp.where(row8 < PEER_TOPK // (a + 1), a2[0:8] + sv1[a], -jnp.inf))
    cand.append(a1[8:16] + sv2[0])
    cand_all = jnp.concatenate(cand, axis=0)
    if exact_ties:
        fv, crank = _top_values(cand_all, PEER_TOPK, with_rank=True, first_only=True)
        taken = (crank < float(PEER_TOPK)).astype(F32)
        blocks = [taken[0:16]] + [taken[8 + 8 * a:16 + 8 * a] for a in range(1, 8)]
        blocks += [taken[72 + a:73 + a] for a in range(8)]
    else:
        fv = _top_values(cand_all, PEER_TOPK)
        thr = fv[PEER_TOPK - 1]
        blocks = [(cand[a] >= thr).astype(F32) for a in range(8)]
        blocks += [(cand[8][a:a + 1] >= thr).astype(F32) for a in range(8)]
    denom = jnp.zeros_like(fv[0])
    for f in fv:
        denom = denom + jnp.exp(f - fv[0])
    rank1 = rank1.astype(BF16)
    rank2 = rank2.astype(BF16)
    cnt = jnp.zeros_like(rank1)
    n_sel = jnp.zeros((1, n_tok), F32)
    for a in range(PEER_TOPK):
        n_a = jnp.sum(blocks[a], axis=0, keepdims=True)
        n_sel = n_sel + n_a
        cnt = jnp.where(rank1 == float(a), n_a.astype(BF16), cnt)
    g1 = (jnp.exp(s1 - sv1[0]) / denom).astype(BF16)
    e2 = jnp.exp(s2 - sv2[0]).astype(BF16)
    if exact_ties:
        return cnt, g1, rank2, e2, jnp.zeros((1, n_tok), F32)
    took = lambda rk: jnp.sum((rk < float(PEER_TOPK)).astype(F32), axis=0, keepdims=True)
    k = float(PEER_TOPK)
    tied = ((took(rank1) != k) | (took(rank2) != k) | (n_sel != k)).astype(F32)
    return cnt, g1, rank2, e2, tied


def _route_kernel(x1_ref, mod_ref, wqt_ref, keys_ref, u_ref, v_ref,
                  h2t_ref, cnt_ref, g1_ref, r2_ref, e2_ref, ubf_ref, vtbf_ref, s_scr, tied_ref):
    i = pl.program_id(0)
    r = _cond_row(i)
    ubf_ref[...] = u_ref[...].astype(BF16)
    vtbf_ref[0] = v_ref[...].T.astype(BF16)
    h2 = x1_ref[...] * (1.0 + _mod_rows(mod_ref, r, 4)) + _mod_rows(mod_ref, r, 3)
    h2t = h2.T.astype(BF16)
    h2t_ref[...] = h2t
    qt = jnp.dot(wqt_ref[...], h2t, preferred_element_type=F32)

    def store(h, sel):
        cnt_ref[h], g1_ref[h], r2_ref[h], e2_ref[h] = sel[0], sel[1], sel[2], sel[3]

    for h in range(PEER_HEADS):
        s = []
        for p in range(2):
            row = (2 * h + p) * N_KEYS
            qhp = qt[row:row + N_KEYS, :].astype(BF16)
            s.append(jnp.dot(keys_ref[2 * h + p], qhp, preferred_element_type=F32))
            s_scr[2 * h + p] = s[p]
        sel = _route_select(s[0], s[1], exact_ties=False)
        store(h, sel)
        tied_ref[h] = jnp.max(sel[4])

    def redo(h, carry):
        @pl.when(tied_ref[h] > 0.0)
        def _():
            store(h, _route_select(s_scr[2 * h], s_scr[2 * h + 1], exact_ties=True))
        return carry
    lax.fori_loop(0, PEER_HEADS, redo, 0)


def _route_call(x1, mod, wqt, keys, u, v):
    full = lambda a: pl.BlockSpec(a.shape, lambda i: (0,) * a.ndim)
    hk = pl.BlockSpec((PEER_HEADS, N_KEYS, TILE), lambda i: (0, 0, i))
    per_key = lambda dt: jax.ShapeDtypeStruct((PEER_HEADS, N_KEYS, N_TOK), dt)
    slab = N_EXPERTS // N_TILES
    slabs_per_block = DENSE_CH // slab
    table = pl.BlockSpec((slab, D_MODEL), lambda i: (i, 0))
    return pl.pallas_call(
        _route_kernel,
        grid=(N_TILES,),
        in_specs=[pl.BlockSpec((TILE, D_MODEL), lambda i: (i, 0)), full(mod), full(wqt), full(keys),
                  table, table],
        out_specs=[pl.BlockSpec((D_MODEL, TILE), lambda i: (0, i)), hk, hk, hk, hk, table,
                   pl.BlockSpec((1, D_MODEL, slab), lambda i: (i // slabs_per_block, 0, i % slabs_per_block))],
        out_shape=[jax.ShapeDtypeStruct((D_MODEL, N_TOK), BF16),
                   per_key(BF16), per_key(BF16), per_key(BF16), per_key(BF16),
                   jax.ShapeDtypeStruct((N_EXPERTS, D_MODEL), BF16),
                   jax.ShapeDtypeStruct((N_EXPERTS // DENSE_CH, D_MODEL, DENSE_CH), BF16)],
        scratch_shapes=[pltpu.VMEM((2 * PEER_HEADS, N_KEYS, TILE), F32), pltpu.SMEM((PEER_HEADS,), F32)],
        compiler_params=_cparams(("arbitrary",)),
        name="peer_route",
    )(x1, mod, wqt, keys, u, v)


def _gelu_tanh(x):
    c1 = math.sqrt(2.0 / math.pi)
    c2 = c1 * 0.044715
    return x * (0.5 + 0.5 * jnp.tanh(x * (c1 + c2 * (x * x))))


def _dense_kernel(h2t_ref, cnt_ref, g1_ref, r2_ref, e2_ref, u_ref, vt_ref, x1_ref, mod_ref,
                  ln_g_ref, ln_b_ref, yp_ref, ys_ref, acc_ref):
    tt = pl.program_id(0)
    c = pl.program_id(1)
    n_sub = DENSE_CH // N_KEYS
    assert n_sub % 16 == 0, "count / gate rows are read as whole packed bf16 row groups"

    @pl.when(c == 0)
    def _():
        acc_ref[...] = jnp.zeros_like(acc_ref)

    i1_base = pl.multiple_of(c * n_sub, n_sub)
    cnt8 = [cnt_ref[h, pl.ds(i1_base, n_sub), :] for h in range(PEER_HEADS)]
    g18 = [g1_ref[h, pl.ds(i1_base, n_sub), :] for h in range(PEER_HEADS)]
    def gate(ci):
        wsum = jnp.zeros((N_KEYS, DENSE_TM), BF16)
        for h in range(PEER_HEADS):
            sel = r2_ref[h] < cnt8[h][ci:ci + 1, :]
            wsum = wsum + jnp.where(sel, e2_ref[h], jnp.zeros((), BF16)) * g18[h][ci:ci + 1, :]
        return wsum

    sub_per_piece = n_sub // DENSE_PIECES
    gates, a_pieces = [], []
    for pc in range(DENSE_PIECES):
        rows = slice(pc * sub_per_piece * N_KEYS, (pc + 1) * sub_per_piece * N_KEYS)
        lhs = u_ref[rows, :]
        if pc > 0:
            zero = gates[-1][0:16, 0:128] * jnp.zeros((), BF16)
            top = jnp.concatenate([lhs[0:16, 0:128] + zero, lhs[0:16, 128:]], axis=1)
            lhs = jnp.concatenate([top, lhs[16:]], axis=0)
        a_pieces.append(jnp.dot(lhs, h2t_ref[...], preferred_element_type=F32))
        gates.extend(gate(ci) for ci in range(pc * sub_per_piece, (pc + 1) * sub_per_piece))
    a = jnp.concatenate(a_pieces, axis=0)
    w_parts = [_gelu_tanh(a[ci * N_KEYS:(ci + 1) * N_KEYS, :].astype(BF16)) * gates[ci] for ci in range(n_sub)]
    zero = gates[-1][0:16, 0:128] * jnp.zeros((), BF16)
    w0 = w_parts[0]
    top = jnp.concatenate([w0[0:16, 0:128] + zero, w0[0:16, 128:]], axis=1)
    w_parts[0] = jnp.concatenate([top, w0[16:]], axis=0)
    w = jnp.concatenate(w_parts, axis=0)
    acc_ref[...] += jnp.dot(vt_ref[0], w, preferred_element_type=F32)

    @pl.when(c == pl.num_programs(1) - 1)
    def _():
        r = _cond_row(tt * (DENSE_TM // TILE))
        ffn = acc_ref[...].T
        y = _layer_norm(ALPHA * x1_ref[...] + _mod_rows(mod_ref, r, 5) * ffn, ln_g_ref[...], ln_b_ref[...])

        @pl.when(tt < DENSE_CTX_TILES)
        def _():
            yp_ref[...] = y

        @pl.when(tt >= DENSE_CTX_TILES)
        def _():
            ys_ref[...] = y


def _dense_call(h2t, cnt, g1, r2, e2, u_bf, vt_bf, x1, mod, ln_g, ln_b):
    full = lambda a: pl.BlockSpec(a.shape, lambda t, c: (0,) * a.ndim)
    hk = pl.BlockSpec((PEER_HEADS, N_KEYS, DENSE_TM), lambda t, c: (0, 0, t))
    n_half = N_CTX_SEQ * CTX_LEN
    return pl.pallas_call(
        _dense_kernel,
        grid=(N_TOK // DENSE_TM, N_EXPERTS // DENSE_CH),
        in_specs=[pl.BlockSpec((D_MODEL, DENSE_TM), lambda t, c: (0, t)),
                  hk, hk, hk, hk,
                  pl.BlockSpec((DENSE_CH, D_MODEL), lambda t, c: (c, 0)),
                  pl.BlockSpec((1, D_MODEL, DENSE_CH), lambda t, c: (c, 0, 0)),
                  pl.BlockSpec((DENSE_TM, D_MODEL), lambda t, c: (t, 0)),
                  full(mod), full(ln_g), full(ln_b)],
        out_specs=[pl.BlockSpec((DENSE_TM, D_MODEL), lambda t, c: (jnp.minimum(t, DENSE_CTX_TILES - 1), 0)),
                   pl.BlockSpec((DENSE_TM, D_MODEL), lambda t, c: (jnp.maximum(t - DENSE_CTX_TILES, 0), 0))],
        out_shape=[jax.ShapeDtypeStruct((n_half, D_MODEL), F32),
                   jax.ShapeDtypeStruct((N_TOK - n_half, D_MODEL), F32)],
        scratch_shapes=[pltpu.VMEM((D_MODEL, DENSE_TM), F32)],
        compiler_params=_cparams(("arbitrary", "arbitrary")),
        name="peer_dense",
    )(h2t, cnt, g1, r2, e2, u_bf, vt_bf, x1, mod, ln_g, ln_b)


def _grid_pos_embed():
    rows = LAT_LEN // GRID_W
    r, col = np.meshgrid(np.arange(rows), np.arange(GRID_W), indexing="ij")

    def sincos(pos, dim):
        omega = 1.0 / (10000.0 ** (np.arange(dim // 2, dtype=np.float64) / (dim // 2)))
        ang = pos.reshape(-1).astype(np.float64)[:, None] * omega[None, :]
        return np.concatenate([np.sin(ang), np.cos(ang)], axis=-1)

    pe = np.concatenate([sincos(r, D_MODEL // 2), sincos(col, D_MODEL // 2)], axis=-1)
    return jnp.asarray(pe, F32)


def _state_pack_t(state):
    packed = jnp.transpose(state, (0, 3, 1, 2)).reshape(N_LAT_SEQ, DV, D_K)
    return jnp.concatenate([jnp.zeros((1, DV, D_K), F32), packed], axis=0)


def _state_unpack_t(st):
    return jnp.transpose(st.reshape(N_CTX_SEQ, DV, HEADS, DK), (0, 2, 3, 1))[:, None]


def kernel(x_prompt, x_sample, c, state_fwd, state_bwd, c_ctx, w_ada, b_ada, w_in, pool_w, pool_scale,
           gk_w, gk_b, gla_norm_g, w_o, ln1_g, ln1_b, peer_wq, peer_keys, peer_u, peer_v, ln2_g, ln2_b):
    xp2 = x_prompt.reshape(N_CTX_SEQ * CTX_LEN, D_MODEL)
    xs2 = x_sample.reshape(N_LAT_SEQ * LAT_LEN, D_MODEL)
    pos = _grid_pos_embed()

    cond8 = jnp.zeros((8, D_MODEL), F32).at[0].set(c_ctx).at[1:3].set(c)
    mod = _mod_call(cond8, w_ada[0], b_ada[0][None, :])

    w = w_in[0]
    w_in_r = jnp.concatenate([w[:, :1536], w[:, 1568:], w[:, 1536:1568],
                              jnp.zeros((D_MODEL, P_COLS - 2080), F32)], axis=1).astype(BF16)
    gkw_bd = jnp.zeros((128, 2 * D_K), F32)
    gkw_bd = gkw_bd.at[0:GATE_RANK, 0:D_K].set(gk_w[0, 0]).at[GATE_RANK:2 * GATE_RANK, D_K:].set(gk_w[0, 1])
    gkb = gk_b[0].reshape(1, 2 * D_K)
    gkw2 = jnp.concatenate([gkw_bd, gkw_bd], axis=0).astype(BF16)
    xpool, q, k, v, gk, ogs = _proj_call(xp2, xs2, pos, mod, w_in_r, gkw2, gkb)

    o_f, st_f, wqt = _gla_fwd_call(q, k, v, gk, _state_pack_t(state_fwd[:, 0]), _gla_constants(False),
                                   peer_wq[0])
    x1, st_b = _gla_bwd_call(
        q, k, v, gk, _state_pack_t(state_bwd[:, 0]), _gla_constants(True), o_f, xpool, ogs,
        xp2, xs2, pos, mod, _pool_band(), pool_w[0].astype(BF16), pool_scale[0][None, :],
        gla_norm_g[0][None, :], w_o[0].astype(BF16), ln1_g[0][None, :], ln1_b[0][None, :])

    keys = peer_keys[0].reshape(2 * PEER_HEADS, N_KEYS, N_KEYS).astype(BF16)
    h2t, cnt, g1, r2, e2, u_bf, vt_bf = _route_call(x1, mod, wqt, keys, peer_u[0], peer_v[0])
    yp, ys = _dense_call(h2t, cnt, g1, r2, e2, u_bf, vt_bf, x1, mod, ln2_g[0][None, :], ln2_b[0][None, :])

    y_prompt = yp.reshape(N_CTX_SEQ, CTX_LEN, D_MODEL)
    y_sample = ys.reshape(N_LAT_SEQ, LAT_LEN, D_MODEL)
    return (y_prompt, y_sample, _state_unpack_t(st_f), _state_unpack_t(st_b))
```

```python
import math

import numpy as np
import jax
import jax.numpy as jnp
from jax import lax
from jax.experimental import pallas as pl
from jax.experimental.pallas import tpu as pltpu

F32 = jnp.float32
BF16 = jnp.bfloat16

D_MODEL = 1024
N_CTX_SEQ, CTX_LEN = 16, 256
N_LAT_SEQ, LAT_LEN = 2, 2048
GRID_W = 64
N_TOK = N_CTX_SEQ * CTX_LEN + N_LAT_SEQ * LAT_LEN
TILE = 256
N_TILES = N_TOK // TILE
CTX_TILES = N_CTX_SEQ * CTX_LEN // TILE
LAT_TILES = LAT_LEN // TILE

D_POOL = 512
POOL_GROUPS = 4
POOL_CH = 128
POOL_WINDOWS = (2, 4, 8, 16)
POOL_HALO = 8
HEADS = 4
DK = 64
DV = 128
D_K = HEADS * DK
D_V = HEADS * DV
GATE_RANK = 16
GATE_NORM = 16.0
CHUNK = 64
N_LEVELS = 7
P_COLS = 2048 + 128

PEER_HEADS = 8
N_KEYS = 128
PEER_TOPK = 16
N_EXPERTS = N_KEYS * N_KEYS
WQ_SLABS = PEER_HEADS * 2 * N_KEYS // 128
DENSE_TM = 512
DENSE_CH = 2048
DENSE_CTX_TILES = N_CTX_SEQ * CTX_LEN // DENSE_TM
DENSE_PIECES = 16
MOD_COLS = 1536
ALPHA = 2.0 ** 0.25
EPS = 1e-5
V7X_VMEM_BYTES = 64 * 1024 * 1024
VMEM_LIMIT = V7X_VMEM_BYTES * 7 // 8


def _cparams(sem):
    return pltpu.CompilerParams(dimension_semantics=sem, vmem_limit_bytes=VMEM_LIMIT)


def _silu(x):
    return x * jax.nn.sigmoid(x)


def _split_bf16(x, pieces):
    out = []
    for _ in range(pieces - 1):
        hi = x.astype(BF16)
        out.append(hi)
        x = x - hi.astype(F32)
    out.append(x.astype(BF16))
    return out


def _dot_exact(m01, x, pieces=3):
    c = x.shape[1]
    xs = jnp.concatenate(_split_bf16(x, pieces), axis=1)
    y = jnp.dot(m01, xs, preferred_element_type=F32)
    out = y[:, :c]
    for p in range(1, pieces):
        out = out + y[:, p * c:(p + 1) * c]
    return out


def _layer_norm(y, g, b):
    mu = jnp.mean(y, axis=-1, keepdims=True)
    yc = y - mu
    var = jnp.mean(yc * yc, axis=-1, keepdims=True)
    return yc * lax.rsqrt(var + EPS) * g + b


def _cond_row(tile):
    return (tile >= CTX_TILES).astype(jnp.int32) + (tile >= CTX_TILES + LAT_TILES).astype(jnp.int32)


def _mod_kernel(cond_ref, w_ref, b_ref, out_ref):
    s = _silu(cond_ref[...]).astype(BF16)
    out_ref[...] = jnp.dot(s, w_ref[...].astype(BF16), preferred_element_type=F32) + b_ref[...]


def _mod_call(cond8, w_ada, b_ada):
    n = w_ada.shape[1]
    bn = MOD_COLS
    return pl.pallas_call(
        _mod_kernel,
        grid=(n // bn,),
        in_specs=[pl.BlockSpec((8, D_MODEL), lambda j: (0, 0)),
                  pl.BlockSpec((D_MODEL, bn), lambda j: (0, j)),
                  pl.BlockSpec((1, bn), lambda j: (0, j))],
        out_specs=pl.BlockSpec((8, bn), lambda j: (0, j)),
        out_shape=jax.ShapeDtypeStruct((8, n), F32),
        compiler_params=_cparams(("arbitrary",)),
        name="mod",
    )(cond8, w_ada, b_ada)


def _load_x(i, xp_ref, xs_ref, pos_ref):
    return jnp.where(i < CTX_TILES, xp_ref[...], xs_ref[...] + pos_ref[...])


def _mod_rows(mod_ref, r, k):
    return mod_ref[pl.ds(r, 1), k * D_MODEL:(k + 1) * D_MODEL]


def _proj_kernel(xp_ref, xs_ref, pos_ref, mod_ref, w_ref, gkw_ref, gkb_ref,
                 oxp, oq, ok, ov, ogk, oog):
    i = pl.program_id(0)
    r = _cond_row(i)
    x = _load_x(i, xp_ref, xs_ref, pos_ref)
    h = x * (1.0 + _mod_rows(mod_ref, r, 1)) + _mod_rows(mod_ref, r, 0)
    p = jnp.dot(h.astype(BF16), w_ref[...], preferred_element_type=F32)
    oxp[...] = p[:, 0:512]
    oq[...] = p[:, 512:768] * (DK ** -0.5)
    ok[...] = p[:, 768:1024]
    ov[...] = p[:, 1024:1536]
    oog[...] = _silu(p[:, 1536:2048])
    glr = p[:, 2048:2176]
    g2 = jnp.concatenate(_split_bf16(glr, 2), axis=1)
    pre = jnp.dot(g2, gkw_ref[...], preferred_element_type=F32) + gkb_ref[...]
    ogk[...] = (jnp.minimum(pre, 0.0) - jnp.log(1.0 + jnp.exp(-jnp.abs(pre)))) * (1.0 / GATE_NORM)


def _x_specs():
    return [pl.BlockSpec((TILE, D_MODEL), lambda i: (jnp.minimum(i, CTX_TILES - 1), 0)),
            pl.BlockSpec((TILE, D_MODEL), lambda i: (jnp.maximum(i - CTX_TILES, 0), 0)),
            pl.BlockSpec((TILE, D_MODEL), lambda i: (jnp.maximum(i - CTX_TILES, 0) % LAT_TILES, 0))]


def _proj_call(xp2, xs2, pos, mod, w_in_r, gkw_bd, gkb):
    tok = lambda c: pl.BlockSpec((TILE, c), lambda i: (i, 0))
    full = lambda a: pl.BlockSpec(a.shape, lambda i: (0,) * a.ndim)
    outs = [512, 256, 256, 512, 512, 512]
    return pl.pallas_call(
        _proj_kernel,
        grid=(N_TILES,),
        in_specs=_x_specs() + [full(mod), full(w_in_r), full(gkw_bd), full(gkb)],
        out_specs=[tok(c) for c in outs],
        out_shape=[jax.ShapeDtypeStruct((N_TOK, c), F32) for c in outs],
        compiler_params=_cparams(("arbitrary",)),
        name="proj",
    )(xp2, xs2, pos, mod, w_in_r, gkw_bd, gkb)


def _gla_constants(rev):
    t = np.arange(TILE)
    same_chunk = (t[:, None] // CHUNK) == (t[None, :] // CHUNK)
    tri = same_chunk & ((t[None, :] >= t[:, None]) if rev else (t[None, :] <= t[:, None]))
    u = np.arange(CHUNK)
    x = u[:, None] ^ u[None, :]
    lvl = np.where(x > 0, np.floor(np.log2(np.maximum(x, 1))), N_LEVELS - 1).astype(np.int32)
    causal = (u[:, None] <= u[None, :]) if rev else (u[:, None] >= u[None, :])
    lvl = np.where(causal, lvl, -1).astype(np.int32)
    return jnp.asarray(tri, BF16), jnp.asarray(np.tile(lvl, (1, HEADS)), jnp.int32)


def _level_ref(b_scr, level, rev):
    half = 1 << level
    sub = lax.broadcasted_iota(jnp.int32, (8, D_K), 0)
    row = lambda r: jnp.broadcast_to(b_scr[r:r + 1, :], (8, D_K))
    pieces = []
    for m in range(TILE // 8):
        refs = [((8 * m + u) // (2 * half)) * (2 * half) + (half if rev else half - 1) for u in range(8)]
        piece = row(refs[0])
        for u in range(1, 8):
            if refs[u] != refs[u - 1]:
                piece = jnp.where(sub >= u, row(refs[u]), piece)
        pieces.append(piece)
    return jnp.concatenate(pieces, axis=0)


def _head_blocks(x, width):
    head = lax.broadcasted_iota(jnp.int32, x.shape, 1) // width
    return jnp.concatenate([jnp.where(head == h, x, jnp.zeros_like(x)) for h in range(HEADS)], axis=0)


def _gla_tile(q, k, v, g, tri, lmap, st_ref, b_scr, rev):
    b = _dot_exact(tri, g)
    b_scr[...] = b
    qe = (q * jnp.exp(b)).astype(BF16)
    vb = v.astype(BF16)

    cdim = (((1,), (1,)), ((), ()))
    st_head = lax.broadcasted_iota(jnp.int32, (DV, D_K), 1) // DK
    n_chunks = TILE // CHUNK
    rows = [slice(c * CHUNK, (c + 1) * CHUNK) for c in range(n_chunks)]
    b_ends, kvds = [], []
    for c in range(n_chunks):
        b_c = b[rows[c]]
        b_end = b_c[0:1, :] if rev else b_c[CHUNK - 1:CHUNK, :]
        kdec = (k[rows[c]] * jnp.exp(b_end - b_c)).astype(BF16)
        kv = lax.dot_general(vb[rows[c]], kdec, (((0,), (0,)), ((), ())), preferred_element_type=F32)
        kvd = kv[0:DV]
        for h in range(1, HEADS):
            kvd = jnp.where(st_head == h, kv[h * DV:(h + 1) * DV], kvd)
        b_ends.append(b_end)
        kvds.append(kvd)
    qds, kds = [], []
    for l in range(N_LEVELS - 1):
        d = b - _level_ref(b_scr, l, rev)
        qds.append((q * jnp.exp(jnp.minimum(d, 0.0))).astype(BF16))
        kds.append((k * jnp.exp(jnp.minimum(-d, 0.0))).astype(BF16))
    qds.append(q.astype(BF16))
    kds.append(k.astype(BF16))
    ps = []
    for c in range(n_chunks):
        p = jnp.zeros((CHUNK, HEADS * CHUNK), F32)
        for l in range(N_LEVELS):
            z = lax.dot_general(qds[l][rows[c]], _head_blocks(kds[l][rows[c]], DK), cdim,
                                preferred_element_type=F32)
            p = jnp.where(lmap == l, z, p)
        ps.append(p.astype(BF16))
    intra = [jnp.dot(ps[c], _head_blocks(vb[rows[c]], DV), preferred_element_type=F32) for c in range(n_chunks)]
    st = st_ref[...]
    outs = [None] * n_chunks
    for c in (reversed(range(n_chunks)) if rev else range(n_chunks)):
        o = lax.dot_general(qe[rows[c]], _head_blocks(st.astype(BF16), DK), cdim,
                            preferred_element_type=F32)
        outs[c] = o + intra[c]
        st = st * jnp.exp(b_ends[c]) + kvds[c]
    st_ref[...] = st
    return jnp.concatenate(outs, axis=0)


def _gla_fwd_kernel(q_ref, k_ref, v_ref, g_ref, st0_ref, tri_ref, lmap_ref, wq_ref,
                    of_ref, stout_ref, wqt_ref, st_scr, b_scr):
    i = pl.program_id(0)
    seq_start = (i <= CTX_TILES) | (i == CTX_TILES + LAT_TILES)

    @pl.when(seq_start)
    def _():
        st_scr[...] = st0_ref[0]

    wqt_ref[...] = wq_ref[...].T.astype(BF16)

    of_ref[...] = _gla_tile(q_ref[...], k_ref[...], v_ref[...], g_ref[...], tri_ref[...],
                            lmap_ref[...], st_scr, b_scr, rev=False)

    @pl.when(i < CTX_TILES)
    def _():
        stout_ref[0] = st_scr[...]


def _gla_fwd_call(q, k, v, gk, st0, consts, wq):
    tri, lmap = consts
    tok = lambda c: pl.BlockSpec((TILE, c), lambda i: (i, 0))
    full = lambda a: pl.BlockSpec(a.shape, lambda i: (0,) * a.ndim)
    slab = lambda i: jnp.minimum(i, WQ_SLABS - 1)
    return pl.pallas_call(
        _gla_fwd_kernel,
        grid=(N_TILES,),
        in_specs=[tok(D_K), tok(D_K), tok(D_V), pl.BlockSpec((TILE, D_K), lambda i: (i, 0)),
                  pl.BlockSpec((1, DV, D_K), lambda i: (_cond_row(i), 0, 0)),
                  full(tri), full(lmap),
                  pl.BlockSpec((D_MODEL, 128), lambda i: (0, slab(i)))],
        out_specs=[tok(D_V),
                   pl.BlockSpec((1, DV, D_K), lambda i: (jnp.minimum(i, CTX_TILES - 1), 0, 0)),
                   pl.BlockSpec((128, D_MODEL), lambda i: (slab(i), 0))],
        out_shape=[jax.ShapeDtypeStruct((N_TOK, D_V), F32),
                   jax.ShapeDtypeStruct((N_CTX_SEQ, DV, D_K), F32),
                   jax.ShapeDtypeStruct((WQ_SLABS * 128, D_MODEL), BF16)],
        scratch_shapes=[pltpu.VMEM((DV, D_K), F32), pltpu.VMEM((TILE, D_K), F32)],
        compiler_params=_cparams(("arbitrary",)),
        name="gla_fwd",
    )(q, k, v, gk, st0, tri, lmap, wq)


def _pool_band():
    t = np.arange(TILE)[:, None]
    e = np.arange(TILE + 2 * POOL_HALO)[None, :]
    bands = []
    for w in POOL_WINDOWS:
        lo = t + POOL_HALO - w // 2
        bands.append(((e >= lo) & (e < lo + w)).astype(np.float32))
    return jnp.asarray(np.stack(bands), BF16)


def _gla_bwd_kernel(q_ref, k_ref, v_ref, g_ref, st0_ref, tri_ref, lmap_ref,
                    of_ref, xpool_ref, prev_ref, next_ref, ogs_ref, xp_ref, xs_ref, pos_ref, mod_ref,
                    band_ref, poolw_ref, pscale_ref, gnorm_ref, wo_ref, ln_g_ref, ln_b_ref,
                    x1_ref, stout_ref, st_scr, b_scr):
    j = N_TILES - 1 - pl.program_id(0)
    lat_idx = jnp.maximum(j - CTX_TILES, 0) % LAT_TILES
    is_ctx = j < CTX_TILES
    seq_first = is_ctx | (lat_idx == 0)
    seq_last = is_ctx | (lat_idx == LAT_TILES - 1)

    @pl.when(seq_last)
    def _():
        st_scr[...] = st0_ref[0]

    xpool = xpool_ref[...]
    prev = jnp.where(seq_first, 0.0, prev_ref[...])
    nxt = jnp.where(seq_last, 0.0, next_ref[...])
    ext = jnp.concatenate([prev, xpool, nxt], axis=0)
    tpos = jnp.where(is_ctx, 0, lat_idx) * TILE + lax.broadcasted_iota(jnp.int32, (TILE, POOL_CH), 0)
    seq_len = jnp.where(is_ctx, CTX_LEN, LAT_LEN)
    pool_parts = []
    for gi, w in enumerate(POOL_WINDOWS):
        sl = slice(gi * POOL_CH, (gi + 1) * POOL_CH)
        wsum = _dot_exact(band_ref[gi], ext[:, sl], pieces=2)
        lo = jnp.maximum(tpos - w // 2, 0)
        hi = jnp.minimum(tpos - w // 2 + w, seq_len)
        dmean = wsum / (hi - lo).astype(F32) - xpool[:, sl]
        y = jnp.dot(dmean.astype(BF16), poolw_ref[gi], preferred_element_type=F32)
        pool_parts.append(y * pscale_ref[:, sl])

    o_b = _gla_tile(q_ref[...], k_ref[...], v_ref[...], g_ref[...], tri_ref[...],
                    lmap_ref[...], st_scr, b_scr, rev=True)

    o = of_ref[...] + o_b
    ogs = ogs_ref[...]
    for h in range(HEADS):
        sl = slice(h * DV, (h + 1) * DV)
        oh = o[:, sl]
        oh = oh * lax.rsqrt(jnp.mean(oh * oh, axis=-1, keepdims=True) + EPS)
        pool_parts.append(oh * gnorm_ref[:, sl] * ogs[:, sl])
    mix_in = jnp.concatenate(pool_parts, axis=1).astype(BF16)
    mix = jnp.dot(mix_in, wo_ref[...], preferred_element_type=F32)

    r = _cond_row(j)
    x = _load_x(j, xp_ref, xs_ref, pos_ref)
    y = ALPHA * x + _mod_rows(mod_ref, r, 2) * mix
    x1_ref[...] = _layer_norm(y, ln_g_ref[...], ln_b_ref[...])

    @pl.when(is_ctx)
    def _():
        stout_ref[0] = st_scr[...]


def _gla_bwd_call(q, k, v, gk, st0, consts, o_f, xpool, ogs, xp2, xs2, pos, mod,
                  band, poolw, pscale, gnorm, wo, ln_g, ln_b):
    tri, lmap = consts
    rv = lambda i: N_TILES - 1 - i
    tok = lambda c: pl.BlockSpec((TILE, c), lambda i: (rv(i), 0))
    full = lambda a: pl.BlockSpec(a.shape, lambda i: (0,) * a.ndim)
    halo_blocks = TILE // POOL_HALO
    n_halo = N_TOK // POOL_HALO
    xspecs = [pl.BlockSpec((TILE, D_MODEL), lambda i: (jnp.minimum(rv(i), CTX_TILES - 1), 0)),
              pl.BlockSpec((TILE, D_MODEL), lambda i: (jnp.maximum(rv(i) - CTX_TILES, 0), 0)),
              pl.BlockSpec((TILE, D_MODEL), lambda i: (jnp.maximum(rv(i) - CTX_TILES, 0) % LAT_TILES, 0))]
    return pl.pallas_call(
        _gla_bwd_kernel,
        grid=(N_TILES,),
        in_specs=[tok(D_K), tok(D_K), tok(D_V), pl.BlockSpec((TILE, D_K), lambda i: (rv(i), 1)),
                  pl.BlockSpec((1, DV, D_K), lambda i: (_cond_row(rv(i)), 0, 0)),
                  full(tri), full(lmap),
                  tok(D_V), tok(D_POOL),
                  pl.BlockSpec((POOL_HALO, D_POOL), lambda i: (jnp.maximum(rv(i) * halo_blocks - 1, 0), 0)),
                  pl.BlockSpec((POOL_HALO, D_POOL),
                               lambda i: (jnp.minimum((rv(i) + 1) * halo_blocks, n_halo - 1), 0)),
                  tok(D_V)] + xspecs + [full(mod), full(band), full(poolw), full(pscale), full(gnorm),
                                        full(wo), full(ln_g), full(ln_b)],
        out_specs=[tok(D_MODEL),
                   pl.BlockSpec((1, DV, D_K), lambda i: (jnp.minimum(rv(i), CTX_TILES - 1), 0, 0))],
        out_shape=[jax.ShapeDtypeStruct((N_TOK, D_MODEL), F32),
                   jax.ShapeDtypeStruct((N_CTX_SEQ, DV, D_K), F32)],
        scratch_shapes=[pltpu.VMEM((DV, D_K), F32), pltpu.VMEM((TILE, D_K), F32)],
        compiler_params=_cparams(("arbitrary",)),
        name="gla_bwd_mix",
    )(q, k, v, gk, st0, tri, lmap, o_f, xpool, xpool, xpool, ogs, xp2, xs2, pos, mod,
      band, poolw, pscale, gnorm, wo, ln_g, ln_b)


def _top_values(x, n, with_rank=False, first_only=False):
    vals = []
    rank = jnp.full(x.shape, float(n), F32)
    row = lax.broadcasted_iota(jnp.int32, x.shape, 0) if first_only else None
    for j in range(n):
        m = jnp.max(x, axis=0, keepdims=True)
        vals.append(m)
        hit = x == m
        if first_only:
            hit = row == jnp.min(jnp.where(hit, row, x.shape[0]), axis=0, keepdims=True)
        if with_rank:
            rank = jnp.where(hit, float(j), rank)
        x = jnp.where(hit, -jnp.inf, x)
    return (vals, rank) if with_rank else vals


def _route_select(s1, s2, exact_ties):
    n_tok = s1.shape[1]
    sv1, rank1 = _top_values(s1, PEER_TOPK, with_rank=True, first_only=exact_ties)
    sv2, rank2 = _top_values(s2, PEER_TOPK, with_rank=True, first_only=exact_ties)
    a1 = jnp.concatenate(sv1, axis=0)
    a2 = jnp.concatenate(sv2, axis=0)
    row8 = lax.broadcasted_iota(jnp.int32, (8, n_tok), 0)
    cand = [a2 + sv1[0]]
    for a in range(1, 8):
        cand.append(jnp.where(row8 < PEER_TOPK // (a + 1), a2[0:8] + sv1[a], -jnp.inf))
    cand.append(a1[8:16] + sv2[0])
    cand_all = jnp.concatenate(cand, axis=0)
    if exact_ties:
        fv, crank = _top_values(cand_all, PEER_TOPK, with_rank=True, first_only=True)
        taken = (crank < float(PEER_TOPK)).astype(F32)
        blocks = [taken[0:16]] + [taken[8 + 8 * a:16 + 8 * a] for a in range(1, 8)]
        blocks += [taken[72 + a:73 + a] for a in range(8)]
    else:
        fv = _top_values(cand_all, PEER_TOPK)
        thr = fv[PEER_TOPK - 1]
        blocks = [(cand[a] >= thr).astype(F32) for a in range(8)]
        blocks += [(cand[8][a:a + 1] >= thr).astype(F32) for a in range(8)]
    denom = jnp.zeros_like(fv[0])
    for f in fv:
        denom = denom + jnp.exp(f - fv[0])
    rank1 = rank1.astype(BF16)
    rank2 = rank2.astype(BF16)
    cnt = jnp.zeros_like(rank1)
    n_sel = jnp.zeros((1, n_tok), F32)
    for a in range(PEER_TOPK):
        n_a = jnp.sum(blocks[a], axis=0, keepdims=True)
        n_sel = n_sel + n_a
        cnt = jnp.where(rank1 == float(a), n_a.astype(BF16), cnt)
    g1 = (jnp.exp(s1 - sv1[0]) / denom).astype(BF16)
    e2 = jnp.exp(s2 - sv2[0]).astype(BF16)
    if exact_ties:
        return cnt, g1, rank2, e2, jnp.zeros((1, n_tok), F32)
    took = lambda rk: jnp.sum((rk < float(PEER_TOPK)).astype(F32), axis=0, keepdims=True)
    k = float(PEER_TOPK)
    tied = ((took(rank1) != k) | (took(rank2) != k) | (n_sel != k)).astype(F32)
    return cnt, g1, rank2, e2, tied


def _route_kernel(x1_ref, mod_ref, wqt_ref, keys_ref, u_ref, v_ref,
                  h2t_ref, cnt_ref, g1_ref, r2_ref, e2_ref, ubf_ref, vtbf_ref, s_scr, tied_ref):
    i = pl.program_id(0)
    r = _cond_row(i)
    ubf_ref[...] = u_ref[...].astype(BF16)
    vtbf_ref[0] = v_ref[...].T.astype(BF16)
    h2 = x1_ref[...] * (1.0 + _mod_rows(mod_ref, r, 4)) + _mod_rows(mod_ref, r, 3)
    h2t = h2.T.astype(BF16)
    h2t_ref[...] = h2t
    qt = jnp.dot(wqt_ref[...], h2t, preferred_element_type=F32)

    def store(h, sel):
        cnt_ref[h], g1_ref[h], r2_ref[h], e2_ref[h] = sel[0], sel[1], sel[2], sel[3]

    for h in range(PEER_HEADS):
        s = []
        for p in range(2):
            row = (2 * h + p) * N_KEYS
            qhp = qt[row:row + N_KEYS, :].astype(BF16)
            s.append(jnp.dot(keys_ref[2 * h + p], qhp, preferred_element_type=F32))
            s_scr[2 * h + p] = s[p]
        sel = _route_select(s[0], s[1], exact_ties=False)
        store(h, sel)
        tied_ref[h] = jnp.max(sel[4])

    def redo(h, carry):
        @pl.when(tied_ref[h] > 0.0)
        def _():
            store(h, _route_select(s_scr[2 * h], s_scr[2 * h + 1], exact_ties=True))
        return carry
    lax.fori_loop(0, PEER_HEADS, redo, 0)


def _route_call(x1, mod, wqt, keys, u, v):
    full = lambda a: pl.BlockSpec(a.shape, lambda i: (0,) * a.ndim)
    hk = pl.BlockSpec((PEER_HEADS, N_KEYS, TILE), lambda i: (0, 0, i))
    per_key = lambda dt: jax.ShapeDtypeStruct((PEER_HEADS, N_KEYS, N_TOK), dt)
    slab = N_EXPERTS // N_TILES
    slabs_per_block = DENSE_CH // slab
    table = pl.BlockSpec((slab, D_MODEL), lambda i: (i, 0))
    return pl.pallas_call(
        _route_kernel,
        grid=(N_TILES,),
        in_specs=[pl.BlockSpec((TILE, D_MODEL), lambda i: (i, 0)), full(mod), full(wqt), full(keys),
                  table, table],
        out_specs=[pl.BlockSpec((D_MODEL, TILE), lambda i: (0, i)), hk, hk, hk, hk, table,
                   pl.BlockSpec((1, D_MODEL, slab), lambda i: (i // slabs_per_block, 0, i % slabs_per_block))],
        out_shape=[jax.ShapeDtypeStruct((D_MODEL, N_TOK), BF16),
                   per_key(BF16), per_key(BF16), per_key(BF16), per_key(BF16),
                   jax.ShapeDtypeStruct((N_EXPERTS, D_MODEL), BF16),
                   jax.ShapeDtypeStruct((N_EXPERTS // DENSE_CH, D_MODEL, DENSE_CH), BF16)],
        scratch_shapes=[pltpu.VMEM((2 * PEER_HEADS, N_KEYS, TILE), F32), pltpu.SMEM((PEER_HEADS,), F32)],
        compiler_params=_cparams(("arbitrary",)),
        name="peer_route",
    )(x1, mod, wqt, keys, u, v)


def _gelu_tanh(x):
    c1 = math.sqrt(2.0 / math.pi)
    c2 = c1 * 0.044715
    return x * (0.5 + 0.5 * jnp.tanh(x * (c1 + c2 * (x * x))))


def _dense_kernel(h2t_ref, cnt_ref, g1_ref, r2_ref, e2_ref, u_ref, vt_ref, x1_ref, mod_ref,
                  ln_g_ref, ln_b_ref, yp_ref, ys_ref, acc_ref):
    tt = pl.program_id(0)
    c = pl.program_id(1)
    n_sub = DENSE_CH // N_KEYS
    assert n_sub % 16 == 0, "count / gate rows are read as whole packed bf16 row groups"

    @pl.when(c == 0)
    def _():
        acc_ref[...] = jnp.zeros_like(acc_ref)

    i1_base = pl.multiple_of(c * n_sub, n_sub)
    cnt8 = [cnt_ref[h, pl.ds(i1_base, n_sub), :] for h in range(PEER_HEADS)]
    g18 = [g1_ref[h, pl.ds(i1_base, n_sub), :] for h in range(PEER_HEADS)]
    def gate(ci):
        wsum = jnp.zeros((N_KEYS, DENSE_TM), BF16)
        for h in range(PEER_HEADS):
            sel = r2_ref[h] < cnt8[h][ci:ci + 1, :]
            wsum = wsum + jnp.where(sel, e2_ref[h], jnp.zeros((), BF16)) * g18[h][ci:ci + 1, :]
        return wsum

    sub_per_piece = n_sub // DENSE_PIECES
    gates, a_pieces = [], []
    for pc in range(DENSE_PIECES):
        rows = slice(pc * sub_per_piece * N_KEYS, (pc + 1) * sub_per_piece * N_KEYS)
        lhs = u_ref[rows, :]
        if pc > 0:
            zero = gates[-1][0:16, 0:128] * jnp.zeros((), BF16)
            top = jnp.concatenate([lhs[0:16, 0:128] + zero, lhs[0:16, 128:]], axis=1)
            lhs = jnp.concatenate([top, lhs[16:]], axis=0)
        a_pieces.append(jnp.dot(lhs, h2t_ref[...], preferred_element_type=F32))
        gates.extend(gate(ci) for ci in range(pc * sub_per_piece, (pc + 1) * sub_per_piece))
    a = jnp.concatenate(a_pieces, axis=0)
    w_parts = [_gelu_tanh(a[ci * N_KEYS:(ci + 1) * N_KEYS, :].astype(BF16)) * gates[ci] for ci in range(n_sub)]
    zero = gates[-1][0:16, 0:128] * jnp.zeros((), BF16)
    w0 = w_parts[0]
    top = jnp.concatenate([w0[0:16, 0:128] + zero, w0[0:16, 128:]], axis=1)
    w_parts[0] = jnp.concatenate([top, w0[16:]], axis=0)
    w = jnp.concatenate(w_parts, axis=0)
    acc_ref[...] += jnp.dot(vt_ref[0], w, preferred_element_type=F32)

    @pl.when(c == pl.num_programs(1) - 1)
    def _():
        r = _cond_row(tt * (DENSE_TM // TILE))
        ffn = acc_ref[...].T
        y = _layer_norm(ALPHA * x1_ref[...] + _mod_rows(mod_ref, r, 5) * ffn, ln_g_ref[...], ln_b_ref[...])

        @pl.when(tt < DENSE_CTX_TILES)
        def _():
            yp_ref[...] = y

        @pl.when(tt >= DENSE_CTX_TILES)
        def _():
            ys_ref[...] = y


def _dense_call(h2t, cnt, g1, r2, e2, u_bf, vt_bf, x1, mod, ln_g, ln_b):
    full = lambda a: pl.BlockSpec(a.shape, lambda t, c: (0,) * a.ndim)
    hk = pl.BlockSpec((PEER_HEADS, N_KEYS, DENSE_TM), lambda t, c: (0, 0, t))
    n_half = N_CTX_SEQ * CTX_LEN
    return pl.pallas_call(
        _dense_kernel,
        grid=(N_TOK // DENSE_TM, N_EXPERTS // DENSE_CH),
        in_specs=[pl.BlockSpec((D_MODEL, DENSE_TM), lambda t, c: (0, t)),
                  hk, hk, hk, hk,
                  pl.BlockSpec((DENSE_CH, D_MODEL), lambda t, c: (c, 0)),
                  pl.BlockSpec((1, D_MODEL, DENSE_CH), lambda t, c: (c, 0, 0)),
                  pl.BlockSpec((DENSE_TM, D_MODEL), lambda t, c: (t, 0)),
                  full(mod), full(ln_g), full(ln_b)],
        out_specs=[pl.BlockSpec((DENSE_TM, D_MODEL), lambda t, c: (jnp.minimum(t, DENSE_CTX_TILES - 1), 0)),
                   pl.BlockSpec((DENSE_TM, D_MODEL), lambda t, c: (jnp.maximum(t - DENSE_CTX_TILES, 0), 0))],
        out_shape=[jax.ShapeDtypeStruct((n_half, D_MODEL), F32),
                   jax.ShapeDtypeStruct((N_TOK - n_half, D_MODEL), F32)],
        scratch_shapes=[pltpu.VMEM((D_MODEL, DENSE_TM), F32)],
        compiler_params=_cparams(("arbitrary", "arbitrary")),
        name="peer_dense",
    )(h2t, cnt, g1, r2, e2, u_bf, vt_bf, x1, mod, ln_g, ln_b)


def _grid_pos_embed():
    rows = LAT_LEN // GRID_W
    r, col = np.meshgrid(np.arange(rows), np.arange(GRID_W), indexing="ij")

    def sincos(pos, dim):
        omega = 1.0 / (10000.0 ** (np.arange(dim // 2, dtype=np.float64) / (dim // 2)))
        ang = pos.reshape(-1).astype(np.float64)[:, None] * omega[None, :]
        return np.concatenate([np.sin(ang), np.cos(ang)], axis=-1)

    pe = np.concatenate([sincos(r, D_MODEL // 2), sincos(col, D_MODEL // 2)], axis=-1)
    return jnp.asarray(pe, F32)


def _state_pack_t(state):
    packed = jnp.transpose(state, (0, 3, 1, 2)).reshape(N_LAT_SEQ, DV, D_K)
    return jnp.concatenate([jnp.zeros((1, DV, D_K), F32), packed], axis=0)


def _state_unpack_t(st):
    return jnp.transpose(st.reshape(N_CTX_SEQ, DV, HEADS, DK), (0, 2, 3, 1))[:, None]


def kernel(x_prompt, x_sample, c, state_fwd, state_bwd, c_ctx, w_ada, b_ada, w_in, pool_w, pool_scale,
           gk_w, gk_b, gla_norm_g, w_o, ln1_g, ln1_b, peer_wq, peer_keys, peer_u, peer_v, ln2_g, ln2_b):
    xp2 = x_prompt.reshape(N_CTX_SEQ * CTX_LEN, D_MODEL)
    xs2 = x_sample.reshape(N_LAT_SEQ * LAT_LEN, D_MODEL)
    pos = _grid_pos_embed()

    cond8 = jnp.zeros((8, D_MODEL), F32).at[0].set(c_ctx).at[1:3].set(c)
    mod = _mod_call(cond8, w_ada[0], b_ada[0][None, :])

    w = w_in[0]
    w_in_r = jnp.concatenate([w[:, :1536], w[:, 1568:], w[:, 1536:1568],
                              jnp.zeros((D_MODEL, P_COLS - 2080), F32)], axis=1).astype(BF16)
    gkw_bd = jnp.zeros((128, 2 * D_K), F32)
    gkw_bd = gkw_bd.at[0:GATE_RANK, 0:D_K].set(gk_w[0, 0]).at[GATE_RANK:2 * GATE_RANK, D_K:].set(gk_w[0, 1])
    gkb = gk_b[0].reshape(1, 2 * D_K)
    gkw2 = jnp.concatenate([gkw_bd, gkw_bd], axis=0).astype(BF16)
    xpool, q, k, v, gk, ogs = _proj_call(xp2, xs2, pos, mod, w_in_r, gkw2, gkb)

    o_f, st_f, wqt = _gla_fwd_call(q, k, v, gk, _state_pack_t(state_fwd[:, 0]), _gla_constants(False),
                                   peer_wq[0])
    x1, st_b = _gla_bwd_call(
        q, k, v, gk, _state_pack_t(state_bwd[:, 0]), _gla_constants(True), o_f, xpool, ogs,
        xp2, xs2, pos, mod, _pool_band(), pool_w[0].astype(BF16), pool_scale[0][None, :],
        gla_norm_g[0][None, :], w_o[0].astype(BF16), ln1_g[0][None, :], ln1_b[0][None, :])

    keys = peer_keys[0].reshape(2 * PEER_HEADS, N_KEYS, N_KEYS).astype(BF16)
    h2t, cnt, g1, r2, e2, u_bf, vt_bf = _route_call(x1, mod, wqt, keys, peer_u[0], peer_v[0])
    yp, ys = _dense_call(h2t, cnt, g1, r2, e2, u_bf, vt_bf, x1, mod, ln2_g[0][None, :], ln2_b[0][None, :])

    y_prompt = yp.reshape(N_CTX_SEQ, CTX_LEN, D_MODEL)
    y_sample = ys.reshape(N_LAT_SEQ, LAT_LEN, D_MODEL)
    return (y_prompt, y_sample, _state_unpack_t(st_f), _state_unpack_t(st_b))
```

```python
import math

import numpy as np
import jax
import jax.numpy as jnp
from jax import lax
from jax.experimental import pallas as pl
from jax.experimental.pallas import tpu as pltpu

F32 = jnp.float32
BF16 = jnp.bfloat16

D_MODEL = 1024
N_CTX_SEQ, CTX_LEN = 16, 256
N_LAT_SEQ, LAT_LEN = 2, 2048
GRID_W = 64
N_TOK = N_CTX_SEQ * CTX_LEN + N_LAT_SEQ * LAT_LEN
TILE = 256
N_TILES = N_TOK // TILE
CTX_TILES = N_CTX_SEQ * CTX_LEN // TILE
LAT_TILES = LAT_LEN // TILE

D_POOL = 512
POOL_GROUPS = 4
POOL_CH = 128
POOL_WINDOWS = (2, 4, 8, 16)
POOL_HALO = 8
HEADS = 4
DK = 64
DV = 128
D_K = HEADS * DK
D_V = HEADS * DV
GATE_RANK = 16
GATE_NORM = 16.0
CHUNK = 64
N_LEVELS = 7
P_COLS = 2048 + 128

PEER_HEADS = 8
N_KEYS = 128
PEER_TOPK = 16
N_EXPERTS = N_KEYS * N_KEYS
WQ_SLABS = PEER_HEADS * 2 * N_KEYS // 128
DENSE_TM = 512
DENSE_CH = 2048
DENSE_CTX_TILES = N_CTX_SEQ * CTX_LEN // DENSE_TM
DENSE_PIECES = 16
MOD_COLS = 1536
ALPHA = 2.0 ** 0.25
EPS = 1e-5
V7X_VMEM_BYTES = 64 * 1024 * 1024
VMEM_LIMIT = V7X_VMEM_BYTES * 7 // 8


def _cparams(sem):
    return pltpu.CompilerParams(dimension_semantics=sem, vmem_limit_bytes=VMEM_LIMIT)


def _silu(x):
    return x * jax.nn.sigmoid(x)


def _split_bf16(x, pieces):
    out = []
    for _ in range(pieces - 1):
        hi = x.astype(BF16)
        out.append(hi)
        x = x - hi.astype(F32)
    out.append(x.astype(BF16))
    return out


def _dot_exact(m01, x, pieces=3):
    c = x.shape[1]
    xs = jnp.concatenate(_split_bf16(x, pieces), axis=1)
    y = jnp.dot(m01, xs, preferred_element_type=F32)
    out = y[:, :c]
    for p in range(1, pieces):
        out = out + y[:, p * c:(p + 1) * c]
    return out


def _layer_norm(y, g, b):
    mu = jnp.mean(y, axis=-1, keepdims=True)
    yc = y - mu
    var = jnp.mean(yc * yc, axis=-1, keepdims=True)
    return yc * lax.rsqrt(var + EPS) * g + b


def _cond_row(tile):
    return (tile >= CTX_TILES).astype(jnp.int32) + (tile >= CTX_TILES + LAT_TILES).astype(jnp.int32)


def _mod_kernel(cond_ref, w_ref, b_ref, out_ref):
    s = _silu(cond_ref[...]).astype(BF16)
    out_ref[...] = jnp.dot(s, w_ref[...].astype(BF16), preferred_element_type=F32) + b_ref[...]


def _mod_call(cond8, w_ada, b_ada):
    n = w_ada.shape[1]
    bn = MOD_COLS
    return pl.pallas_call(
        _mod_kernel,
        grid=(n // bn,),
        in_specs=[pl.BlockSpec((8, D_MODEL), lambda j: (0, 0)),
                  pl.BlockSpec((D_MODEL, bn), lambda j: (0, j)),
                  pl.BlockSpec((1, bn), lambda j: (0, j))],
        out_specs=pl.BlockSpec((8, bn), lambda j: (0, j)),
        out_shape=jax.ShapeDtypeStruct((8, n), F32),
        compiler_params=_cparams(("arbitrary",)),
        name="mod",
    )(cond8, w_ada, b_ada)


def _load_x(i, xp_ref, xs_ref, pos_ref):
    return jnp.where(i < CTX_TILES, xp_ref[...], xs_ref[...] + pos_ref[...])


def _mod_rows(mod_ref, r, k):
    return mod_ref[pl.ds(r, 1), k * D_MODEL:(k + 1) * D_MODEL]


def _proj_kernel(xp_ref, xs_ref, pos_ref, mod_ref, w_ref, gkw_ref, gkb_ref,
                 oxp, oq, ok, ov, ogk, oog):
    i = pl.program_id(0)
    r = _cond_row(i)
    x = _load_x(i, xp_ref, xs_ref, pos_ref)
    h = x * (1.0 + _mod_rows(mod_ref, r, 1)) + _mod_rows(mod_ref, r, 0)
    p = jnp.dot(h.astype(BF16), w_ref[...], preferred_element_type=F32)
    oxp[...] = p[:, 0:512]
    oq[...] = p[:, 512:768] * (DK ** -0.5)
    ok[...] = p[:, 768:1024]
    ov[...] = p[:, 1024:1536]
    oog[...] = _silu(p[:, 1536:2048])
    glr = p[:, 2048:2176]
    g2 = jnp.concatenate(_split_bf16(glr, 2), axis=1)
    pre = jnp.dot(g2, gkw_ref[...], preferred_element_type=F32) + gkb_ref[...]
    ogk[...] = (jnp.minimum(pre, 0.0) - jnp.log(1.0 + jnp.exp(-jnp.abs(pre)))) * (1.0 / GATE_NORM)


def _x_specs():
    return [pl.BlockSpec((TILE, D_MODEL), lambda i: (jnp.minimum(i, CTX_TILES - 1), 0)),
            pl.BlockSpec((TILE, D_MODEL), lambda i: (jnp.maximum(i - CTX_TILES, 0), 0)),
            pl.BlockSpec((TILE, D_MODEL), lambda i: (jnp.maximum(i - CTX_TILES, 0) % LAT_TILES, 0))]


def _proj_call(xp2, xs2, pos, mod, w_in_r, gkw_bd, gkb):
    tok = lambda c: pl.BlockSpec((TILE, c), lambda i: (i, 0))
    full = lambda a: pl.BlockSpec(a.shape, lambda i: (0,) * a.ndim)
    outs = [512, 256, 256, 512, 512, 512]
    return pl.pallas_call(
        _proj_kernel,
        grid=(N_TILES,),
        in_specs=_x_specs() + [full(mod), full(w_in_r), full(gkw_bd), full(gkb)],
        out_specs=[tok(c) for c in outs],
        out_shape=[jax.ShapeDtypeStruct((N_TOK, c), F32) for c in outs],
        compiler_params=_cparams(("arbitrary",)),
        name="proj",
    )(xp2, xs2, pos, mod, w_in_r, gkw_bd, gkb)


def _gla_constants(rev):
    t = np.arange(TILE)
    same_chunk = (t[:, None] // CHUNK) == (t[None, :] // CHUNK)
    tri = same_chunk & ((t[None, :] >= t[:, None]) if rev else (t[None, :] <= t[:, None]))
    u = np.arange(CHUNK)
    x = u[:, None] ^ u[None, :]
    lvl = np.where(x > 0, np.floor(np.log2(np.maximum(x, 1))), N_LEVELS - 1).astype(np.int32)
    causal = (u[:, None] <= u[None, :]) if rev else (u[:, None] >= u[None, :])
    lvl = np.where(causal, lvl, -1).astype(np.int32)
    return jnp.asarray(tri, BF16), jnp.asarray(np.tile(lvl, (1, HEADS)), jnp.int32)


def _level_ref(b_scr, level, rev):
    half = 1 << level
    sub = lax.broadcasted_iota(jnp.int32, (8, D_K), 0)
    row = lambda r: jnp.broadcast_to(b_scr[r:r + 1, :], (8, D_K))
    pieces = []
    for m in range(TILE // 8):
        refs = [((8 * m + u) // (2 * half)) * (2 * half) + (half if rev else half - 1) for u in range(8)]
        piece = row(refs[0])
        for u in range(1, 8):
            if refs[u] != refs[u - 1]:
                piece = jnp.where(sub >= u, row(refs[u]), piece)
        pieces.append(piece)
    return jnp.concatenate(pieces, axis=0)


def _head_blocks(x, width):
    head = lax.broadcasted_iota(jnp.int32, x.shape, 1) // width
    return jnp.concatenate([jnp.where(head == h, x, jnp.zeros_like(x)) for h in range(HEADS)], axis=0)


def _gla_tile(q, k, v, g, tri, lmap, st_ref, b_scr, rev):
    b = _dot_exact(tri, g)
    b_scr[...] = b
    qe = (q * jnp.exp(b)).astype(BF16)
    vb = v.astype(BF16)

    cdim = (((1,), (1,)), ((), ()))
    st_head = lax.broadcasted_iota(jnp.int32, (DV, D_K), 1) // DK
    n_chunks = TILE // CHUNK
    rows = [slice(c * CHUNK, (c + 1) * CHUNK) for c in range(n_chunks)]
    b_ends, kvds = [], []
    for c in range(n_chunks):
        b_c = b[rows[c]]
        b_end = b_c[0:1, :] if rev else b_c[CHUNK - 1:CHUNK, :]
        kdec = (k[rows[c]] * jnp.exp(b_end - b_c)).astype(BF16)
        kv = lax.dot_general(vb[rows[c]], kdec, (((0,), (0,)), ((), ())), preferred_element_type=F32)
        kvd = kv[0:DV]
        for h in range(1, HEADS):
            kvd = jnp.where(st_head == h, kv[h * DV:(h + 1) * DV], kvd)
        b_ends.append(b_end)
        kvds.append(kvd)
    qds, kds = [], []
    for l in range(N_LEVELS - 1):
        d = b - _level_ref(b_scr, l, rev)
        qds.append((q * jnp.exp(jnp.minimum(d, 0.0))).astype(BF16))
        kds.append((k * jnp.exp(jnp.minimum(-d, 0.0))).astype(BF16))
    qds.append(q.astype(BF16))
    kds.append(k.astype(BF16))
    ps = []
    for c in range(n_chunks):
        p = jnp.zeros((CHUNK, HEADS * CHUNK), F32)
        for l in range(N_LEVELS):
            z = lax.dot_general(qds[l][rows[c]], _head_blocks(kds[l][rows[c]], DK), cdim,
                                preferred_element_type=F32)
            p = jnp.where(lmap == l, z, p)
        ps.append(p.astype(BF16))
    intra = [jnp.dot(ps[c], _head_blocks(vb[rows[c]], DV), preferred_element_type=F32) for c in range(n_chunks)]
    st = st_ref[...]
    outs = [None] * n_chunks
    for c in (reversed(range(n_chunks)) if rev else range(n_chunks)):
        o = lax.dot_general(qe[rows[c]], _head_blocks(st.astype(BF16), DK), cdim,
                            preferred_element_type=F32)
        outs[c] = o + intra[c]
        st = st * jnp.exp(b_ends[c]) + kvds[c]
    st_ref[...] = st
    return jnp.concatenate(outs, axis=0)


def _gla_fwd_kernel(q_ref, k_ref, v_ref, g_ref, st0_ref, tri_ref, lmap_ref, wq_ref,
                    of_ref, stout_ref, wqt_ref, st_scr, b_scr):
    i = pl.program_id(0)
    seq_start = (i <= CTX_TILES) | (i == CTX_TILES + LAT_TILES)

    @pl.when(seq_start)
    def _():
        st_scr[...] = st0_ref[0]

    wqt_ref[...] = wq_ref[...].T.astype(BF16)

    of_ref[...] = _gla_tile(q_ref[...], k_ref[...], v_ref[...], g_ref[...], tri_ref[...],
                            lmap_ref[...], st_scr, b_scr, rev=False)

    @pl.when(i < CTX_TILES)
    def _():
        stout_ref[0] = st_scr[...]


def _gla_fwd_call(q, k, v, gk, st0, consts, wq):
    tri, lmap = consts
    tok = lambda c: pl.BlockSpec((TILE, c), lambda i: (i, 0))
    full = lambda a: pl.BlockSpec(a.shape, lambda i: (0,) * a.ndim)
    slab = lambda i: jnp.minimum(i, WQ_SLABS - 1)
    return pl.pallas_call(
        _gla_fwd_kernel,
        grid=(N_TILES,),
        in_specs=[tok(D_K), tok(D_K), tok(D_V), pl.BlockSpec((TILE, D_K), lambda i: (i, 0)),
                  pl.BlockSpec((1, DV, D_K), lambda i: (_cond_row(i), 0, 0)),
                  full(tri), full(lmap),
                  pl.BlockSpec((D_MODEL, 128), lambda i: (0, slab(i)))],
        out_specs=[tok(D_V),
                   pl.BlockSpec((1, DV, D_K), lambda i: (jnp.minimum(i, CTX_TILES - 1), 0, 0)),
                   pl.BlockSpec((128, D_MODEL), lambda i: (slab(i), 0))],
        out_shape=[jax.ShapeDtypeStruct((N_TOK, D_V), F32),
                   jax.ShapeDtypeStruct((N_CTX_SEQ, DV, D_K), F32),
                   jax.ShapeDtypeStruct((WQ_SLABS * 128, D_MODEL), BF16)],
        scratch_shapes=[pltpu.VMEM((DV, D_K), F32), pltpu.VMEM((TILE, D_K), F32)],
        compiler_params=_cparams(("arbitrary",)),
        name="gla_fwd",
    )(q, k, v, gk, st0, tri, lmap, wq)


def _pool_band():
    t = np.arange(TILE)[:, None]
    e = np.arange(TILE + 2 * POOL_HALO)[None, :]
    bands = []
    for w in POOL_WINDOWS:
        lo = t + POOL_HALO - w // 2
        bands.append(((e >= lo) & (e < lo + w)).astype(np.float32))
    return jnp.asarray(np.stack(bands), BF16)


def _gla_bwd_kernel(q_ref, k_ref, v_ref, g_ref, st0_ref, tri_ref, lmap_ref,
                    of_ref, xpool_ref, prev_ref, next_ref, ogs_ref, xp_ref, xs_ref, pos_ref, mod_ref,
                    band_ref, poolw_ref, pscale_ref, gnorm_ref, wo_ref, ln_g_ref, ln_b_ref,
                    x1_ref, stout_ref, st_scr, b_scr):
    j = N_TILES - 1 - pl.program_id(0)
    lat_idx = jnp.maximum(j - CTX_TILES, 0) % LAT_TILES
    is_ctx = j < CTX_TILES
    seq_first = is_ctx | (lat_idx == 0)
    seq_last = is_ctx | (lat_idx == LAT_TILES - 1)

    @pl.when(seq_last)
    def _():
        st_scr[...] = st0_ref[0]

    xpool = xpool_ref[...]
    prev = jnp.where(seq_first, 0.0, prev_ref[...])
    nxt = jnp.where(seq_last, 0.0, next_ref[...])
    ext = jnp.concatenate([prev, xpool, nxt], axis=0)
    tpos = jnp.where(is_ctx, 0, lat_idx) * TILE + lax.broadcasted_iota(jnp.int32, (TILE, POOL_CH), 0)
    seq_len = jnp.where(is_ctx, CTX_LEN, LAT_LEN)
    group = [slice(gi * POOL_CH, (gi + 1) * POOL_CH) for gi in range(POOL_GROUPS)]
    wsums = [_dot_exact(band_ref[gi], ext[:, group[gi]], pieces=2)
             for gi in range(POOL_GROUPS)]
    dmeans = []
    for gi, w in enumerate(POOL_WINDOWS):
        lo = jnp.maximum(tpos - w // 2, 0)
        hi = jnp.minimum(tpos - w // 2 + w, seq_len)
        dmeans.append((wsums[gi] / (hi - lo).astype(F32) - xpool[:, group[gi]]).astype(BF16))
    ys = [jnp.dot(dmeans[gi], poolw_ref[gi], preferred_element_type=F32) for gi in range(POOL_GROUPS)]
    pool_parts = [ys[gi] * pscale_ref[:, group[gi]] for gi in range(POOL_GROUPS)]

    o_b = _gla_tile(q_ref[...], k_ref[...], v_ref[...], g_ref[...], tri_ref[...],
                    lmap_ref[...], st_scr, b_scr, rev=True)

    o = of_ref[...] + o_b
    ogs = ogs_ref[...]
    head = [slice(h * DV, (h + 1) * DV) for h in range(HEADS)]
    inv_rms = [lax.rsqrt(jnp.mean(o[:, sl] * o[:, sl], axis=-1, keepdims=True) + EPS) for sl in head]
    pool_parts += [o[:, sl] * inv_rms[h] * gnorm_ref[:, sl] * ogs[:, sl] for h, sl in enumerate(head)]
    mix_in = jnp.concatenate(pool_parts, axis=1).astype(BF16)
    mix = jnp.dot(mix_in, wo_ref[...], preferred_element_type=F32)

    r = _cond_row(j)
    x = _load_x(j, xp_ref, xs_ref, pos_ref)
    y = ALPHA * x + _mod_rows(mod_ref, r, 2) * mix
    x1_ref[...] = _layer_norm(y, ln_g_ref[...], ln_b_ref[...])

    @pl.when(is_ctx)
    def _():
        stout_ref[0] = st_scr[...]


def _gla_bwd_call(q, k, v, gk, st0, consts, o_f, xpool, ogs, xp2, xs2, pos, mod,
                  band, poolw, pscale, gnorm, wo, ln_g, ln_b):
    tri, lmap = consts
    rv = lambda i: N_TILES - 1 - i
    tok = lambda c: pl.BlockSpec((TILE, c), lambda i: (rv(i), 0))
    full = lambda a: pl.BlockSpec(a.shape, lambda i: (0,) * a.ndim)
    halo_blocks = TILE // POOL_HALO
    n_halo = N_TOK // POOL_HALO
    xspecs = [pl.BlockSpec((TILE, D_MODEL), lambda i: (jnp.minimum(rv(i), CTX_TILES - 1), 0)),
              pl.BlockSpec((TILE, D_MODEL), lambda i: (jnp.maximum(rv(i) - CTX_TILES, 0), 0)),
              pl.BlockSpec((TILE, D_MODEL), lambda i: (jnp.maximum(rv(i) - CTX_TILES, 0) % LAT_TILES, 0))]
    return pl.pallas_call(
        _gla_bwd_kernel,
        grid=(N_TILES,),
        in_specs=[tok(D_K), tok(D_K), tok(D_V), pl.BlockSpec((TILE, D_K), lambda i: (rv(i), 1)),
                  pl.BlockSpec((1, DV, D_K), lambda i: (_cond_row(rv(i)), 0, 0)),
                  full(tri), full(lmap),
                  tok(D_V), tok(D_POOL),
                  pl.BlockSpec((POOL_HALO, D_POOL), lambda i: (jnp.maximum(rv(i) * halo_blocks - 1, 0), 0)),
                  pl.BlockSpec((POOL_HALO, D_POOL),
                               lambda i: (jnp.minimum((rv(i) + 1) * halo_blocks, n_halo - 1), 0)),
                  tok(D_V)] + xspecs + [full(mod), full(band), full(poolw), full(pscale), full(gnorm),
                                        full(wo), full(ln_g), full(ln_b)],
        out_specs=[tok(D_MODEL),
                   pl.BlockSpec((1, DV, D_K), lambda i: (jnp.minimum(rv(i), CTX_TILES - 1), 0, 0))],
        out_shape=[jax.ShapeDtypeStruct((N_TOK, D_MODEL), F32),
                   jax.ShapeDtypeStruct((N_CTX_SEQ, DV, D_K), F32)],
        scratch_shapes=[pltpu.VMEM((DV, D_K), F32), pltpu.VMEM((TILE, D_K), F32)],
        compiler_params=_cparams(("arbitrary",)),
        name="gla_bwd_mix",
    )(q, k, v, gk, st0, tri, lmap, o_f, xpool, xpool, xpool, ogs, xp2, xs2, pos, mod,
      band, poolw, pscale, gnorm, wo, ln_g, ln_b)


def _top_values(x, n, with_rank=False, first_only=False):
    vals = []
    rank = jnp.full(x.shape, float(n), F32)
    row = lax.broadcasted_iota(jnp.int32, x.shape, 0) if first_only else None
    for j in range(n):
        m = jnp.max(x, axis=0, keepdims=True)
        vals.append(m)
        hit = x == m
        if first_only:
            hit = row == jnp.min(jnp.where(hit, row, x.shape[0]), axis=0, keepdims=True)
        if with_rank:
            rank = jnp.where(hit, float(j), rank)
        x = jnp.where(hit, -jnp.inf, x)
    return (vals, rank) if with_rank else vals


def _route_select(s1, s2, exact_ties):
    n_tok = s1.shape[1]
    sv1, rank1 = _top_values(s1, PEER_TOPK, with_rank=True, first_only=exact_ties)
    sv2, rank2 = _top_values(s2, PEER_TOPK, with_rank=True, first_only=exact_ties)
    a1 = jnp.concatenate(sv1, axis=0)
    a2 = jnp.concatenate(sv2, axis=0)
    row8 = lax.broadcasted_iota(jnp.int32, (8, n_tok), 0)
    cand = [a2 + sv1[0]]
    for a in range(1, 8):
        cand.append(jnp.where(row8 < PEER_TOPK // (a + 1), a2[0:8] + sv1[a], -jnp.inf))
    cand.append(a1[8:16] + sv2[0])
    cand_all = jnp.concatenate(cand, axis=0)
    if exact_ties:
        fv, crank = _top_values(cand_all, PEER_TOPK, with_rank=True, first_only=True)
        taken = (crank < float(PEER_TOPK)).astype(F32)
        blocks = [taken[0:16]] + [taken[8 + 8 * a:16 + 8 * a] for a in range(1, 8)]
        blocks += [taken[72 + a:73 + a] for a in range(8)]
    else:
        fv = _top_values(cand_all, PEER_TOPK)
        thr = fv[PEER_TOPK - 1]
        blocks = [(cand[a] >= thr).astype(F32) for a in range(8)]
        blocks += [(cand[8][a:a + 1] >= thr).astype(F32) for a in range(8)]
    denom = jnp.zeros_like(fv[0])
    for f in fv:
        denom = denom + jnp.exp(f - fv[0])
    rank1 = rank1.astype(BF16)
    rank2 = rank2.astype(BF16)
    cnt = jnp.zeros_like(rank1)
    n_sel = jnp.zeros((1, n_tok), F32)
    for a in range(PEER_TOPK):
        n_a = jnp.sum(blocks[a], axis=0, keepdims=True)
        n_sel = n_sel + n_a
        cnt = jnp.where(rank1 == float(a), n_a.astype(BF16), cnt)
    g1 = (jnp.exp(s1 - sv1[0]) / denom).astype(BF16)
    e2 = jnp.exp(s2 - sv2[0]).astype(BF16)
    if exact_ties:
        return cnt, g1, rank2, e2, jnp.zeros((1, n_tok), F32)
    took = lambda rk: jnp.sum((rk < float(PEER_TOPK)).astype(F32), axis=0, keepdims=True)
    k = float(PEER_TOPK)
    tied = ((took(rank1) != k) | (took(rank2) != k) | (n_sel != k)).astype(F32)
    return cnt, g1, rank2, e2, tied


def _route_kernel(x1_ref, mod_ref, wqt_ref, keys_ref, u_ref, v_ref,
                  h2t_ref, cnt_ref, g1_ref, r2_ref, e2_ref, ubf_ref, vtbf_ref, s_scr, tied_ref):
    i = pl.program_id(0)
    r = _cond_row(i)
    ubf_ref[...] = u_ref[...].astype(BF16)
    vtbf_ref[0] = v_ref[...].T.astype(BF16)
    h2 = x1_ref[...] * (1.0 + _mod_rows(mod_ref, r, 4)) + _mod_rows(mod_ref, r, 3)
    h2t = h2.T.astype(BF16)
    h2t_ref[...] = h2t
    qt = jnp.dot(wqt_ref[...], h2t, preferred_element_type=F32)

    def store(h, sel):
        cnt_ref[h], g1_ref[h], r2_ref[h], e2_ref[h] = sel[0], sel[1], sel[2], sel[3]

    for h in range(PEER_HEADS):
        s = []
        for p in range(2):
            row = (2 * h + p) * N_KEYS
            qhp = qt[row:row + N_KEYS, :].astype(BF16)
            s.append(jnp.dot(keys_ref[2 * h + p], qhp, preferred_element_type=F32))
            s_scr[2 * h + p] = s[p]
        sel = _route_select(s[0], s[1], exact_ties=False)
        store(h, sel)
        tied_ref[h] = jnp.max(sel[4])

    def redo(h, carry):
        @pl.when(tied_ref[h] > 0.0)
        def _():
            store(h, _route_select(s_scr[2 * h], s_scr[2 * h + 1], exact_ties=True))
        return carry
    lax.fori_loop(0, PEER_HEADS, redo, 0)


def _route_call(x1, mod, wqt, keys, u, v):
    full = lambda a: pl.BlockSpec(a.shape, lambda i: (0,) * a.ndim)
    hk = pl.BlockSpec((PEER_HEADS, N_KEYS, TILE), lambda i: (0, 0, i))
    per_key = lambda dt: jax.ShapeDtypeStruct((PEER_HEADS, N_KEYS, N_TOK), dt)
    slab = N_EXPERTS // N_TILES
    slabs_per_block = DENSE_CH // slab
    table = pl.BlockSpec((slab, D_MODEL), lambda i: (i, 0))
    return pl.pallas_call(
        _route_kernel,
        grid=(N_TILES,),
        in_specs=[pl.BlockSpec((TILE, D_MODEL), lambda i: (i, 0)), full(mod), full(wqt), full(keys),
                  table, table],
        out_specs=[pl.BlockSpec((D_MODEL, TILE), lambda i: (0, i)), hk, hk, hk, hk, table,
                   pl.BlockSpec((1, D_MODEL, slab), lambda i: (i // slabs_per_block, 0, i % slabs_per_block))],
        out_shape=[jax.ShapeDtypeStruct((D_MODEL, N_TOK), BF16),
                   per_key(BF16), per_key(BF16), per_key(BF16), per_key(BF16),
                   jax.ShapeDtypeStruct((N_EXPERTS, D_MODEL), BF16),
                   jax.ShapeDtypeStruct((N_EXPERTS // DENSE_CH, D_MODEL, DENSE_CH), BF16)],
        scratch_shapes=[pltpu.VMEM((2 * PEER_HEADS, N_KEYS, TILE), F32), pltpu.SMEM((PEER_HEADS,), F32)],
        compiler_params=_cparams(("arbitrary",)),
        name="peer_route",
    )(x1, mod, wqt, keys, u, v)


def _gelu_tanh(x):
    c1 = math.sqrt(2.0 / math.pi)
    c2 = c1 * 0.044715
    return x * (0.5 + 0.5 * jnp.tanh(x * (c1 + c2 * (x * x))))


def _dense_kernel(h2t_ref, cnt_ref, g1_ref, r2_ref, e2_ref, u_ref, vt_ref, x1_ref, mod_ref,
                  ln_g_ref, ln_b_ref, yp_ref, ys_ref, acc_ref):
    tt = pl.program_id(0)
    c = pl.program_id(1)
    n_sub = DENSE_CH // N_KEYS
    assert n_sub % 16 == 0, "count / gate rows are read as whole packed bf16 row groups"

    @pl.when(c == 0)
    def _():
        acc_ref[...] = jnp.zeros_like(acc_ref)

    i1_base = pl.multiple_of(c * n_sub, n_sub)
    cnt8 = [cnt_ref[h, pl.ds(i1_base, n_sub), :] for h in range(PEER_HEADS)]
    g18 = [g1_ref[h, pl.ds(i1_base, n_sub), :] for h in range(PEER_HEADS)]
    def gate(ci):
        wsum = jnp.zeros((N_KEYS, DENSE_TM), BF16)
        for h in range(PEER_HEADS):
            sel = r2_ref[h] < cnt8[h][ci:ci + 1, :]
            wsum = wsum + jnp.where(sel, e2_ref[h], jnp.zeros((), BF16)) * g18[h][ci:ci + 1, :]
        return wsum

    sub_per_piece = n_sub // DENSE_PIECES
    gates, a_pieces = [], []
    for pc in range(DENSE_PIECES):
        rows = slice(pc * sub_per_piece * N_KEYS, (pc + 1) * sub_per_piece * N_KEYS)
        lhs = u_ref[rows, :]
        if pc > 0:
            zero = gates[-1][0:16, 0:128] * jnp.zeros((), BF16)
            top = jnp.concatenate([lhs[0:16, 0:128] + zero, lhs[0:16, 128:]], axis=1)
            lhs = jnp.concatenate([top, lhs[16:]], axis=0)
        a_pieces.append(jnp.dot(lhs, h2t_ref[...], preferred_element_type=F32))
        gates.extend(gate(ci) for ci in range(pc * sub_per_piece, (pc + 1) * sub_per_piece))
    a = jnp.concatenate(a_pieces, axis=0)
    w_parts = [_gelu_tanh(a[ci * N_KEYS:(ci + 1) * N_KEYS, :].astype(BF16)) * gates[ci] for ci in range(n_sub)]
    zero = gates[-1][0:16, 0:128] * jnp.zeros((), BF16)
    w0 = w_parts[0]
    top = jnp.concatenate([w0[0:16, 0:128] + zero, w0[0:16, 128:]], axis=1)
    w_parts[0] = jnp.concatenate([top, w0[16:]], axis=0)
    w = jnp.concatenate(w_parts, axis=0)
    acc_ref[...] += jnp.dot(vt_ref[0], w, preferred_element_type=F32)

    @pl.when(c == pl.num_programs(1) - 1)
    def _():
        r = _cond_row(tt * (DENSE_TM // TILE))
        ffn = acc_ref[...].T
        y = _layer_norm(ALPHA * x1_ref[...] + _mod_rows(mod_ref, r, 5) * ffn, ln_g_ref[...], ln_b_ref[...])

        @pl.when(tt < DENSE_CTX_TILES)
        def _():
            yp_ref[...] = y

        @pl.when(tt >= DENSE_CTX_TILES)
        def _():
            ys_ref[...] = y


def _dense_call(h2t, cnt, g1, r2, e2, u_bf, vt_bf, x1, mod, ln_g, ln_b):
    full = lambda a: pl.BlockSpec(a.shape, lambda t, c: (0,) * a.ndim)
    hk = pl.BlockSpec((PEER_HEADS, N_KEYS, DENSE_TM), lambda t, c: (0, 0, t))
    n_half = N_CTX_SEQ * CTX_LEN
    return pl.pallas_call(
        _dense_kernel,
        grid=(N_TOK // DENSE_TM, N_EXPERTS // DENSE_CH),
        in_specs=[pl.BlockSpec((D_MODEL, DENSE_TM), lambda t, c: (0, t)),
                  hk, hk, hk, hk,
                  pl.BlockSpec((DENSE_CH, D_MODEL), lambda t, c: (c, 0)),
                  pl.BlockSpec((1, D_MODEL, DENSE_CH), lambda t, c: (c, 0, 0)),
                  pl.BlockSpec((DENSE_TM, D_MODEL), lambda t, c: (t, 0)),
                  full(mod), full(ln_g), full(ln_b)],
        out_specs=[pl.BlockSpec((DENSE_TM, D_MODEL), lambda t, c: (jnp.minimum(t, DENSE_CTX_TILES - 1), 0)),
                   pl.BlockSpec((DENSE_TM, D_MODEL), lambda t, c: (jnp.maximum(t - DENSE_CTX_TILES, 0), 0))],
        out_shape=[jax.ShapeDtypeStruct((n_half, D_MODEL), F32),
                   jax.ShapeDtypeStruct((N_TOK - n_half, D_MODEL), F32)],
        scratch_shapes=[pltpu.VMEM((D_MODEL, DENSE_TM), F32)],
        compiler_params=_cparams(("arbitrary", "arbitrary")),
        name="peer_dense",
    )(h2t, cnt, g1, r2, e2, u_bf, vt_bf, x1, mod, ln_g, ln_b)


def _grid_pos_embed():
    rows = LAT_LEN // GRID_W
    r, col = np.meshgrid(np.arange(rows), np.arange(GRID_W), indexing="ij")

    def sincos(pos, dim):
        omega = 1.0 / (10000.0 ** (np.arange(dim // 2, dtype=np.float64) / (dim // 2)))
        ang = pos.reshape(-1).astype(np.float64)[:, None] * omega[None, :]
        return np.concatenate([np.sin(ang), np.cos(ang)], axis=-1)

    pe = np.concatenate([sincos(r, D_MODEL // 2), sincos(col, D_MODEL // 2)], axis=-1)
    return jnp.asarray(pe, F32)


def _state_pack_t(state):
    packed = jnp.transpose(state, (0, 3, 1, 2)).reshape(N_LAT_SEQ, DV, D_K)
    return jnp.concatenate([jnp.zeros((1, DV, D_K), F32), packed], axis=0)


def _state_unpack_t(st):
    return jnp.transpose(st.reshape(N_CTX_SEQ, DV, HEADS, DK), (0, 2, 3, 1))[:, None]


def kernel(x_prompt, x_sample, c, state_fwd, state_bwd, c_ctx, w_ada, b_ada, w_in, pool_w, pool_scale,
           gk_w, gk_b, gla_norm_g, w_o, ln1_g, ln1_b, peer_wq, peer_keys, peer_u, peer_v, ln2_g, ln2_b):
    xp2 = x_prompt.reshape(N_CTX_SEQ * CTX_LEN, D_MODEL)
    xs2 = x_sample.reshape(N_LAT_SEQ * LAT_LEN, D_MODEL)
    pos = _grid_pos_embed()

    cond8 = jnp.zeros((8, D_MODEL), F32).at[0].set(c_ctx).at[1:3].set(c)
    mod = _mod_call(cond8, w_ada[0], b_ada[0][None, :])

    w = w_in[0]
    w_in_r = jnp.concatenate([w[:, :1536], w[:, 1568:], w[:, 1536:1568],
                              jnp.zeros((D_MODEL, P_COLS - 2080), F32)], axis=1).astype(BF16)
    gkw_bd = jnp.zeros((128, 2 * D_K), F32)
    gkw_bd = gkw_bd.at[0:GATE_RANK, 0:D_K].set(gk_w[0, 0]).at[GATE_RANK:2 * GATE_RANK, D_K:].set(gk_w[0, 1])
    gkb = gk_b[0].reshape(1, 2 * D_K)
    gkw2 = jnp.concatenate([gkw_bd, gkw_bd], axis=0).astype(BF16)
    xpool, q, k, v, gk, ogs = _proj_call(xp2, xs2, pos, mod, w_in_r, gkw2, gkb)

    o_f, st_f, wqt = _gla_fwd_call(q, k, v, gk, _state_pack_t(state_fwd[:, 0]), _gla_constants(False),
                                   peer_wq[0])
    x1, st_b = _gla_bwd_call(
        q, k, v, gk, _state_pack_t(state_bwd[:, 0]), _gla_constants(True), o_f, xpool, ogs,
        xp2, xs2, pos, mod, _pool_band(), pool_w[0].astype(BF16), pool_scale[0][None, :],
        gla_norm_g[0][None, :], w_o[0].astype(BF16), ln1_g[0][None, :], ln1_b[0][None, :])

    keys = peer_keys[0].reshape(2 * PEER_HEADS, N_KEYS, N_KEYS).astype(BF16)
    h2t, cnt, g1, r2, e2, u_bf, vt_bf = _route_call(x1, mod, wqt, keys, peer_u[0], peer_v[0])
    yp, ys = _dense_call(h2t, cnt, g1, r2, e2, u_bf, vt_bf, x1, mod, ln2_g[0][None, :], ln2_b[0][None, :])

    y_prompt = yp.reshape(N_CTX_SEQ, CTX_LEN, D_MODEL)
    y_sample = ys.reshape(N_LAT_SEQ, LAT_LEN, D_MODEL)
    return (y_prompt, y_sample, _state_unpack_t(st_f), _state_unpack_t(st_b))
```

```python
import math

import numpy as np
import jax
import jax.numpy as jnp
from jax import lax
from jax.experimental import pallas as pl
from jax.experimental.pallas import tpu as pltpu

F32 = jnp.float32
BF16 = jnp.bfloat16

D_MODEL = 1024
N_CTX_SEQ, CTX_LEN = 16, 256
N_LAT_SEQ, LAT_LEN = 2, 2048
GRID_W = 64
N_TOK = N_CTX_SEQ * CTX_LEN + N_LAT_SEQ * LAT_LEN
TILE = 256
N_TILES = N_TOK // TILE
CTX_TILES = N_CTX_SEQ * CTX_LEN // TILE
LAT_TILES = LAT_LEN // TILE

D_POOL = 512
POOL_GROUPS = 4
POOL_CH = 128
POOL_WINDOWS = (2, 4, 8, 16)
POOL_HALO = 8
HEADS = 4
DK = 64
DV = 128
D_K = HEADS * DK
D_V = HEADS * DV
GATE_RANK = 16
GATE_NORM = 16.0
CHUNK = 64
N_LEVELS = 7
P_COLS = 2048 + 128

PEER_HEADS = 8
N_KEYS = 128
PEER_TOPK = 16
N_EXPERTS = N_KEYS * N_KEYS
WQ_SLABS = PEER_HEADS * 2 * N_KEYS // 128
DENSE_TM = 512
DENSE_CH = 2048
DENSE_CTX_TILES = N_CTX_SEQ * CTX_LEN // DENSE_TM
DENSE_PIECES = 16
MOD_COLS = 1536
ALPHA = 2.0 ** 0.25
EPS = 1e-5
V7X_VMEM_BYTES = 64 * 1024 * 1024
VMEM_LIMIT = V7X_VMEM_BYTES * 7 // 8


def _cparams(sem):
    return pltpu.CompilerParams(dimension_semantics=sem, vmem_limit_bytes=VMEM_LIMIT)


def _silu(x):
    return x * jax.nn.sigmoid(x)


def _split_bf16(x, pieces):
    out = []
    for _ in range(pieces - 1):
        hi = x.astype(BF16)
        out.append(hi)
        x = x - hi.astype(F32)
    out.append(x.astype(BF16))
    return out


def _dot_exact(m01, x, pieces=3):
    c = x.shape[1]
    xs = jnp.concatenate(_split_bf16(x, pieces), axis=1)
    y = jnp.dot(m01, xs, preferred_element_type=F32)
    out = y[:, :c]
    for p in range(1, pieces):
        out = out + y[:, p * c:(p + 1) * c]
    return out


def _layer_norm(y, g, b):
    mu = jnp.mean(y, axis=-1, keepdims=True)
    yc = y - mu
    var = jnp.mean(yc * yc, axis=-1, keepdims=True)
    return yc * lax.rsqrt(var + EPS) * g + b


def _cond_row(tile):
    return (tile >= CTX_TILES).astype(jnp.int32) + (tile >= CTX_TILES + LAT_TILES).astype(jnp.int32)


def _mod_kernel(cond_ref, w_ref, b_ref, out_ref):
    s = _silu(cond_ref[...]).astype(BF16)
    out_ref[...] = jnp.dot(s, w_ref[...].astype(BF16), preferred_element_type=F32) + b_ref[...]


def _mod_call(cond8, w_ada, b_ada):
    n = w_ada.shape[1]
    bn = MOD_COLS
    return pl.pallas_call(
        _mod_kernel,
        grid=(n // bn,),
        in_specs=[pl.BlockSpec((8, D_MODEL), lambda j: (0, 0)),
                  pl.BlockSpec((D_MODEL, bn), lambda j: (0, j)),
                  pl.BlockSpec((1, bn), lambda j: (0, j))],
        out_specs=pl.BlockSpec((8, bn), lambda j: (0, j)),
        out_shape=jax.ShapeDtypeStruct((8, n), F32),
        compiler_params=_cparams(("arbitrary",)),
        name="mod",
    )(cond8, w_ada, b_ada)


def _load_x(i, xp_ref, xs_ref, pos_ref):
    return jnp.where(i < CTX_TILES, xp_ref[...], xs_ref[...] + pos_ref[...])


def _mod_rows(mod_ref, r, k):
    return mod_ref[pl.ds(r, 1), k * D_MODEL:(k + 1) * D_MODEL]


def _proj_kernel(xp_ref, xs_ref, pos_ref, mod_ref, w_ref, gkw_ref, gkb_ref,
                 oxp, oq, ok, ov, ogk, oog):
    i = pl.program_id(0)
    r = _cond_row(i)
    x = _load_x(i, xp_ref, xs_ref, pos_ref)
    h = x * (1.0 + _mod_rows(mod_ref, r, 1)) + _mod_rows(mod_ref, r, 0)
    p = jnp.dot(h.astype(BF16), w_ref[...], preferred_element_type=F32)
    oxp[...] = p[:, 0:512]
    oq[...] = p[:, 512:768] * (DK ** -0.5)
    ok[...] = p[:, 768:1024]
    ov[...] = p[:, 1024:1536]
    oog[...] = _silu(p[:, 1536:2048])
    glr = p[:, 2048:2176]
    g2 = jnp.concatenate(_split_bf16(glr, 2), axis=1)
    pre = jnp.dot(g2, gkw_ref[...], preferred_element_type=F32) + gkb_ref[...]
    ogk[...] = (jnp.minimum(pre, 0.0) - jnp.log(1.0 + jnp.exp(-jnp.abs(pre)))) * (1.0 / GATE_NORM)


def _x_specs():
    return [pl.BlockSpec((TILE, D_MODEL), lambda i: (jnp.minimum(i, CTX_TILES - 1), 0)),
            pl.BlockSpec((TILE, D_MODEL), lambda i: (jnp.maximum(i - CTX_TILES, 0), 0)),
            pl.BlockSpec((TILE, D_MODEL), lambda i: (jnp.maximum(i - CTX_TILES, 0) % LAT_TILES, 0))]


def _proj_call(xp2, xs2, pos, mod, w_in_r, gkw_bd, gkb):
    tok = lambda c: pl.BlockSpec((TILE, c), lambda i: (i, 0))
    full = lambda a: pl.BlockSpec(a.shape, lambda i: (0,) * a.ndim)
    outs = [512, 256, 256, 512, 512, 512]
    return pl.pallas_call(
        _proj_kernel,
        grid=(N_TILES,),
        in_specs=_x_specs() + [full(mod), full(w_in_r), full(gkw_bd), full(gkb)],
        out_specs=[tok(c) for c in outs],
        out_shape=[jax.ShapeDtypeStruct((N_TOK, c), F32) for c in outs],
        compiler_params=_cparams(("arbitrary",)),
        name="proj",
    )(xp2, xs2, pos, mod, w_in_r, gkw_bd, gkb)


def _gla_constants(rev):
    t = np.arange(TILE)
    same_chunk = (t[:, None] // CHUNK) == (t[None, :] // CHUNK)
    tri = same_chunk & ((t[None, :] >= t[:, None]) if rev else (t[None, :] <= t[:, None]))
    u = np.arange(CHUNK)
    x = u[:, None] ^ u[None, :]
    lvl = np.where(x > 0, np.floor(np.log2(np.maximum(x, 1))), N_LEVELS - 1).astype(np.int32)
    causal = (u[:, None] <= u[None, :]) if rev else (u[:, None] >= u[None, :])
    lvl = np.where(causal, lvl, -1).astype(np.int32)
    return jnp.asarray(tri, BF16), jnp.asarray(np.tile(lvl, (1, HEADS)), jnp.int32)


def _level_ref(b_scr, level, rev):
    half = 1 << level
    sub = lax.broadcasted_iota(jnp.int32, (8, D_K), 0)
    row = lambda r: jnp.broadcast_to(b_scr[r:r + 1, :], (8, D_K))
    pieces = []
    for m in range(b_scr.shape[0] // 8):
        refs = [((8 * m + u) // (2 * half)) * (2 * half) + (half if rev else half - 1) for u in range(8)]
        piece = row(refs[0])
        for u in range(1, 8):
            if refs[u] != refs[u - 1]:
                piece = jnp.where(sub >= u, row(refs[u]), piece)
        pieces.append(piece)
    return jnp.concatenate(pieces, axis=0)


def _head_blocks(x, width):
    head = lax.broadcasted_iota(jnp.int32, x.shape, 1) // width
    return jnp.concatenate([jnp.where(head == h, x, jnp.zeros_like(x)) for h in range(HEADS)], axis=0)


def _gla_tile(q, k, v, g, tri, lmap, st, b_scr, rev, restart=None):
    n_rows = q.shape[0]
    b = jnp.concatenate([_dot_exact(tri, g[t:t + TILE]) for t in range(0, n_rows, TILE)], axis=0)
    b_scr[...] = b
    qe = (q * jnp.exp(b)).astype(BF16)
    vb = v.astype(BF16)

    cdim = (((1,), (1,)), ((), ()))
    st_head = lax.broadcasted_iota(jnp.int32, (DV, D_K), 1) // DK
    n_chunks = n_rows // CHUNK
    rows = [slice(c * CHUNK, (c + 1) * CHUNK) for c in range(n_chunks)]
    b_ends, kvds = [], []
    for c in range(n_chunks):
        b_c = b[rows[c]]
        b_end = b_c[0:1, :] if rev else b_c[CHUNK - 1:CHUNK, :]
        kdec = (k[rows[c]] * jnp.exp(b_end - b_c)).astype(BF16)
        kv = lax.dot_general(vb[rows[c]], kdec, (((0,), (0,)), ((), ())), preferred_element_type=F32)
        kvd = kv[0:DV]
        for h in range(1, HEADS):
            kvd = jnp.where(st_head == h, kv[h * DV:(h + 1) * DV], kvd)
        b_ends.append(b_end)
        kvds.append(kvd)
    qds, kds = [], []
    for l in range(N_LEVELS - 1):
        d = b - _level_ref(b_scr, l, rev)
        qds.append((q * jnp.exp(jnp.minimum(d, 0.0))).astype(BF16))
        kds.append((k * jnp.exp(jnp.minimum(-d, 0.0))).astype(BF16))
    qds.append(q.astype(BF16))
    kds.append(k.astype(BF16))
    ps = []
    for c in range(n_chunks):
        p = jnp.zeros((CHUNK, HEADS * CHUNK), F32)
        for l in range(N_LEVELS):
            z = lax.dot_general(qds[l][rows[c]], _head_blocks(kds[l][rows[c]], DK), cdim,
                                preferred_element_type=F32)
            p = jnp.where(lmap == l, z, p)
        ps.append(p.astype(BF16))
    intra = [jnp.dot(ps[c], _head_blocks(vb[rows[c]], DV), preferred_element_type=F32) for c in range(n_chunks)]
    outs = [None] * n_chunks
    st_first = None
    for n, c in enumerate(reversed(range(n_chunks)) if rev else range(n_chunks)):
        if n > 0 and n % (TILE // CHUNK) == 0:
            st_first = st if st_first is None else st_first
            if restart is not None:
                st = jnp.where(restart[0], restart[1], st)
        o = lax.dot_general(qe[rows[c]], _head_blocks(st.astype(BF16), DK), cdim,
                            preferred_element_type=F32)
        outs[c] = o + intra[c]
        st = st * jnp.exp(b_ends[c]) + kvds[c]
    return jnp.concatenate(outs, axis=0), st, (st if st_first is None else st_first)


FWD_TILES = 2


def _gla_fwd_kernel(q_ref, k_ref, v_ref, g_ref, st0_ref, tri_ref, lmap_ref, wq_ref,
                    of_ref, stout_ref, wqt_ref, st_scr, b_scr):
    i = pl.program_id(0)
    first = FWD_TILES * i
    is_ctx = first < CTX_TILES
    seq_start = is_ctx | ((first - CTX_TILES) % LAT_TILES == 0)

    wqt_ref[...] = wq_ref[...].T.astype(BF16)

    @pl.when(i == 0)
    def _():
        st_scr[...] = jnp.zeros_like(st_scr)

    st_in = jnp.where(seq_start, st0_ref[0], st_scr[...])
    o, st_out, st_mid = _gla_tile(q_ref[...], k_ref[...], v_ref[...], g_ref[...], tri_ref[...], lmap_ref[...],
                                  st_in, b_scr, rev=False, restart=(is_ctx, st0_ref[0]))
    of_ref[...] = o
    st_scr[...] = st_out

    @pl.when(is_ctx)
    def _():
        stout_ref[0] = st_mid
        stout_ref[1] = st_out


def _gla_fwd_call(q, k, v, gk, st0, consts, wq):
    tri, lmap = consts
    rows = FWD_TILES * TILE
    steps = N_TILES // FWD_TILES
    assert steps == WQ_SLABS and CTX_TILES % FWD_TILES == 0 and LAT_TILES % FWD_TILES == 0
    tok = lambda c: pl.BlockSpec((rows, c), lambda i: (i, 0))
    full = lambda a: pl.BlockSpec(a.shape, lambda i: (0,) * a.ndim)
    return pl.pallas_call(
        _gla_fwd_kernel,
        grid=(steps,),
        in_specs=[tok(D_K), tok(D_K), tok(D_V), pl.BlockSpec((rows, D_K), lambda i: (i, 0)),
                  pl.BlockSpec((1, DV, D_K), lambda i: (_cond_row(FWD_TILES * i), 0, 0)),
                  full(tri), full(lmap),
                  pl.BlockSpec((D_MODEL, 128), lambda i: (0, i))],
        out_specs=[tok(D_V),
                   pl.BlockSpec((FWD_TILES, DV, D_K),
                                lambda i: (jnp.minimum(i, CTX_TILES // FWD_TILES - 1), 0, 0)),
                   pl.BlockSpec((128, D_MODEL), lambda i: (i, 0))],
        out_shape=[jax.ShapeDtypeStruct((N_TOK, D_V), F32),
                   jax.ShapeDtypeStruct((N_CTX_SEQ, DV, D_K), F32),
                   jax.ShapeDtypeStruct((WQ_SLABS * 128, D_MODEL), BF16)],
        scratch_shapes=[pltpu.VMEM((DV, D_K), F32), pltpu.VMEM((rows, D_K), F32)],
        compiler_params=_cparams(("arbitrary",)),
        name="gla_fwd",
    )(q, k, v, gk, st0, tri, lmap, wq)


def _pool_band():
    t = np.arange(TILE)[:, None]
    e = np.arange(TILE + 2 * POOL_HALO)[None, :]
    bands = []
    for w in POOL_WINDOWS:
        lo = t + POOL_HALO - w // 2
        bands.append(((e >= lo) & (e < lo + w)).astype(np.float32))
    return jnp.asarray(np.stack(bands), BF16)


def _gla_bwd_kernel(q_ref, k_ref, v_ref, g_ref, st0_ref, tri_ref, lmap_ref,
                    of_ref, xpool_ref, prev_ref, next_ref, ogs_ref, xp_ref, xs_ref, pos_ref, mod_ref,
                    band_ref, poolw_ref, pscale_ref, gnorm_ref, wo_ref, ln_g_ref, ln_b_ref,
                    x1_ref, stout_ref, st_scr, b_scr):
    j = N_TILES - 1 - pl.program_id(0)
    lat_idx = jnp.maximum(j - CTX_TILES, 0) % LAT_TILES
    is_ctx = j < CTX_TILES
    seq_first = is_ctx | (lat_idx == 0)
    seq_last = is_ctx | (lat_idx == LAT_TILES - 1)

    @pl.when(seq_last)
    def _():
        st_scr[...] = st0_ref[0]

    xpool = xpool_ref[...]
    prev = jnp.where(seq_first, 0.0, prev_ref[...])
    nxt = jnp.where(seq_last, 0.0, next_ref[...])
    ext = jnp.concatenate([prev, xpool, nxt], axis=0)
    tpos = jnp.where(is_ctx, 0, lat_idx) * TILE + lax.broadcasted_iota(jnp.int32, (TILE, POOL_CH), 0)
    seq_len = jnp.where(is_ctx, CTX_LEN, LAT_LEN)
    group = [slice(gi * POOL_CH, (gi + 1) * POOL_CH) for gi in range(POOL_GROUPS)]
    wsums = [_dot_exact(band_ref[gi], ext[:, group[gi]], pieces=2)
             for gi in range(POOL_GROUPS)]
    dmeans = []
    for gi, w in enumerate(POOL_WINDOWS):
        lo = jnp.maximum(tpos - w // 2, 0)
        hi = jnp.minimum(tpos - w // 2 + w, seq_len)
        dmeans.append((wsums[gi] / (hi - lo).astype(F32) - xpool[:, group[gi]]).astype(BF16))
    ys = [jnp.dot(dmeans[gi], poolw_ref[gi], preferred_element_type=F32) for gi in range(POOL_GROUPS)]
    pool_parts = [ys[gi] * pscale_ref[:, group[gi]] for gi in range(POOL_GROUPS)]

    o_b, st_new, _ = _gla_tile(q_ref[...], k_ref[...], v_ref[...], g_ref[...], tri_ref[...],
                               lmap_ref[...], st_scr[...], b_scr, rev=True)
    st_scr[...] = st_new

    o = of_ref[...] + o_b
    ogs = ogs_ref[...]
    head = [slice(h * DV, (h + 1) * DV) for h in range(HEADS)]
    inv_rms = [lax.rsqrt(jnp.mean(o[:, sl] * o[:, sl], axis=-1, keepdims=True) + EPS) for sl in head]
    pool_parts += [o[:, sl] * inv_rms[h] * gnorm_ref[:, sl] * ogs[:, sl] for h, sl in enumerate(head)]
    mix_in = jnp.concatenate(pool_parts, axis=1).astype(BF16)
    mix = jnp.dot(mix_in, wo_ref[...], preferred_element_type=F32)

    r = _cond_row(j)
    x = _load_x(j, xp_ref, xs_ref, pos_ref)
    y = ALPHA * x + _mod_rows(mod_ref, r, 2) * mix
    x1_ref[...] = _layer_norm(y, ln_g_ref[...], ln_b_ref[...])

    @pl.when(is_ctx)
    def _():
        stout_ref[0] = st_scr[...]


def _gla_bwd_call(q, k, v, gk, st0, consts, o_f, xpool, ogs, xp2, xs2, pos, mod,
                  band, poolw, pscale, gnorm, wo, ln_g, ln_b):
    tri, lmap = consts
    rv = lambda i: N_TILES - 1 - i
    tok = lambda c: pl.BlockSpec((TILE, c), lambda i: (rv(i), 0))
    full = lambda a: pl.BlockSpec(a.shape, lambda i: (0,) * a.ndim)
    halo_blocks = TILE // POOL_HALO
    n_halo = N_TOK // POOL_HALO
    xspecs = [pl.BlockSpec((TILE, D_MODEL), lambda i: (jnp.minimum(rv(i), CTX_TILES - 1), 0)),
              pl.BlockSpec((TILE, D_MODEL), lambda i: (jnp.maximum(rv(i) - CTX_TILES, 0), 0)),
              pl.BlockSpec((TILE, D_MODEL), lambda i: (jnp.maximum(rv(i) - CTX_TILES, 0) % LAT_TILES, 0))]
    return pl.pallas_call(
        _gla_bwd_kernel,
        grid=(N_TILES,),
        in_specs=[tok(D_K), tok(D_K), tok(D_V), pl.BlockSpec((TILE, D_K), lambda i: (rv(i), 1)),
                  pl.BlockSpec((1, DV, D_K), lambda i: (_cond_row(rv(i)), 0, 0)),
                  full(tri), full(lmap),
                  tok(D_V), tok(D_POOL),
                  pl.BlockSpec((POOL_HALO, D_POOL), lambda i: (jnp.maximum(rv(i) * halo_blocks - 1, 0), 0)),
                  pl.BlockSpec((POOL_HALO, D_POOL),
                               lambda i: (jnp.minimum((rv(i) + 1) * halo_blocks, n_halo - 1), 0)),
                  tok(D_V)] + xspecs + [full(mod), full(band), full(poolw), full(pscale), full(gnorm),
                                        full(wo), full(ln_g), full(ln_b)],
        out_specs=[tok(D_MODEL),
                   pl.BlockSpec((1, DV, D_K), lambda i: (jnp.minimum(rv(i), CTX_TILES - 1), 0, 0))],
        out_shape=[jax.ShapeDtypeStruct((N_TOK, D_MODEL), F32),
                   jax.ShapeDtypeStruct((N_CTX_SEQ, DV, D_K), F32)],
        scratch_shapes=[pltpu.VMEM((DV, D_K), F32), pltpu.VMEM((TILE, D_K), F32)],
        compiler_params=_cparams(("arbitrary",)),
        name="gla_bwd_mix",
    )(q, k, v, gk, st0, tri, lmap, o_f, xpool, xpool, xpool, ogs, xp2, xs2, pos, mod,
      band, poolw, pscale, gnorm, wo, ln_g, ln_b)


def _top_values(x, n, with_rank=False, first_only=False):
    vals = []
    rank = jnp.full(x.shape, float(n), F32)
    row = lax.broadcasted_iota(jnp.int32, x.shape, 0) if first_only else None
    for j in range(n):
        m = jnp.max(x, axis=0, keepdims=True)
        vals.append(m)
        hit = x == m
        if first_only:
            hit = row == jnp.min(jnp.where(hit, row, x.shape[0]), axis=0, keepdims=True)
        if with_rank:
            rank = jnp.where(hit, float(j), rank)
        x = jnp.where(hit, -jnp.inf, x)
    return (vals, rank) if with_rank else vals


def _route_select(s1, s2, exact_ties):
    n_tok = s1.shape[1]
    sv1, rank1 = _top_values(s1, PEER_TOPK, with_rank=True, first_only=exact_ties)
    sv2, rank2 = _top_values(s2, PEER_TOPK, with_rank=True, first_only=exact_ties)
    a1 = jnp.concatenate(sv1, axis=0)
    a2 = jnp.concatenate(sv2, axis=0)
    row8 = lax.broadcasted_iota(jnp.int32, (8, n_tok), 0)
    cand = [a2 + sv1[0]]
    for a in range(1, 8):
        cand.append(jnp.where(row8 < PEER_TOPK // (a + 1), a2[0:8] + sv1[a], -jnp.inf))
    cand.append(a1[8:16] + sv2[0])
    cand_all = jnp.concatenate(cand, axis=0)
    if exact_ties:
        fv, crank = _top_values(cand_all, PEER_TOPK, with_rank=True, first_only=True)
        taken = (crank < float(PEER_TOPK)).astype(F32)
        blocks = [taken[0:16]] + [taken[8 + 8 * a:16 + 8 * a] for a in range(1, 8)]
        blocks += [taken[72 + a:73 + a] for a in range(8)]
    else:
        fv = _top_values(cand_all, PEER_TOPK)
        thr = fv[PEER_TOPK - 1]
        blocks = [(cand[a] >= thr).astype(F32) for a in range(8)]
        blocks += [(cand[8][a:a + 1] >= thr).astype(F32) for a in range(8)]
    denom = jnp.zeros_like(fv[0])
    for f in fv:
        denom = denom + jnp.exp(f - fv[0])
    rank1 = rank1.astype(BF16)
    rank2 = rank2.astype(BF16)
    cnt = jnp.zeros_like(rank1)
    n_sel = jnp.zeros((1, n_tok), F32)
    for a in range(PEER_TOPK):
        n_a = jnp.sum(blocks[a], axis=0, keepdims=True)
        n_sel = n_sel + n_a
        cnt = jnp.where(rank1 == float(a), n_a.astype(BF16), cnt)
    g1 = (jnp.exp(s1 - sv1[0]) / denom).astype(BF16)
    e2 = jnp.exp(s2 - sv2[0]).astype(BF16)
    if exact_ties:
        return cnt, g1, rank2, e2, jnp.zeros((1, n_tok), F32)
    took = lambda rk: jnp.sum((rk < float(PEER_TOPK)).astype(F32), axis=0, keepdims=True)
    k = float(PEER_TOPK)
    tied = ((took(rank1) != k) | (took(rank2) != k) | (n_sel != k)).astype(F32)
    return cnt, g1, rank2, e2, tied


def _route_kernel(x1_ref, mod_ref, wqt_ref, keys_ref, u_ref, v_ref,
                  h2t_ref, cnt_ref, g1_ref, r2_ref, e2_ref, ubf_ref, vtbf_ref, s_scr, tied_ref):
    i = pl.program_id(0)
    r = _cond_row(i)
    ubf_ref[...] = u_ref[...].astype(BF16)
    vtbf_ref[0] = v_ref[...].T.astype(BF16)
    h2 = x1_ref[...] * (1.0 + _mod_rows(mod_ref, r, 4)) + _mod_rows(mod_ref, r, 3)
    h2t = h2.T.astype(BF16)
    h2t_ref[...] = h2t
    qt = jnp.dot(wqt_ref[...], h2t, preferred_element_type=F32)

    def store(h, sel):
        cnt_ref[h], g1_ref[h], r2_ref[h], e2_ref[h] = sel[0], sel[1], sel[2], sel[3]

    for h in range(PEER_HEADS):
        s = []
        for p in range(2):
            row = (2 * h + p) * N_KEYS
            qhp = qt[row:row + N_KEYS, :].astype(BF16)
            s.append(jnp.dot(keys_ref[2 * h + p], qhp, preferred_element_type=F32))
            s_scr[2 * h + p] = s[p]
        sel = _route_select(s[0], s[1], exact_ties=False)
        store(h, sel)
        tied_ref[h] = jnp.max(sel[4])

    def redo(h, carry):
        @pl.when(tied_ref[h] > 0.0)
        def _():
            store(h, _route_select(s_scr[2 * h], s_scr[2 * h + 1], exact_ties=True))
        return carry
    lax.fori_loop(0, PEER_HEADS, redo, 0)


def _route_call(x1, mod, wqt, keys, u, v):
    full = lambda a: pl.BlockSpec(a.shape, lambda i: (0,) * a.ndim)
    hk = pl.BlockSpec((PEER_HEADS, N_KEYS, TILE), lambda i: (0, 0, i))
    per_key = lambda dt: jax.ShapeDtypeStruct((PEER_HEADS, N_KEYS, N_TOK), dt)
    slab = N_EXPERTS // N_TILES
    slabs_per_block = DENSE_CH // slab
    table = pl.BlockSpec((slab, D_MODEL), lambda i: (i, 0))
    return pl.pallas_call(
        _route_kernel,
        grid=(N_TILES,),
        in_specs=[pl.BlockSpec((TILE, D_MODEL), lambda i: (i, 0)), full(mod), full(wqt), full(keys),
                  table, table],
        out_specs=[pl.BlockSpec((D_MODEL, TILE), lambda i: (0, i)), hk, hk, hk, hk, table,
                   pl.BlockSpec((1, D_MODEL, slab), lambda i: (i // slabs_per_block, 0, i % slabs_per_block))],
        out_shape=[jax.ShapeDtypeStruct((D_MODEL, N_TOK), BF16),
                   per_key(BF16), per_key(BF16), per_key(BF16), per_key(BF16),
                   jax.ShapeDtypeStruct((N_EXPERTS, D_MODEL), BF16),
                   jax.ShapeDtypeStruct((N_EXPERTS // DENSE_CH, D_MODEL, DENSE_CH), BF16)],
        scratch_shapes=[pltpu.VMEM((2 * PEER_HEADS, N_KEYS, TILE), F32), pltpu.SMEM((PEER_HEADS,), F32)],
        compiler_params=_cparams(("arbitrary",)),
        name="peer_route",
    )(x1, mod, wqt, keys, u, v)


def _gelu_tanh(x):
    c1 = math.sqrt(2.0 / math.pi)
    c2 = c1 * 0.044715
    return x * (0.5 + 0.5 * jnp.tanh(x * (c1 + c2 * (x * x))))


def _dense_kernel(h2t_ref, cnt_ref, g1_ref, r2_ref, e2_ref, u_ref, vt_ref, x1_ref, mod_ref,
                  ln_g_ref, ln_b_ref, yp_ref, ys_ref, acc_ref):
    tt = pl.program_id(0)
    c = pl.program_id(1)
    n_sub = DENSE_CH // N_KEYS
    assert n_sub % 16 == 0, "count / gate rows are read as whole packed bf16 row groups"

    @pl.when(c == 0)
    def _():
        acc_ref[...] = jnp.zeros_like(acc_ref)

    i1_base = pl.multiple_of(c * n_sub, n_sub)
    cnt8 = [cnt_ref[h, pl.ds(i1_base, n_sub), :] for h in range(PEER_HEADS)]
    g18 = [g1_ref[h, pl.ds(i1_base, n_sub), :] for h in range(PEER_HEADS)]
    def gate(ci):
        wsum = jnp.zeros((N_KEYS, DENSE_TM), BF16)
        for h in range(PEER_HEADS):
            sel = r2_ref[h] < cnt8[h][ci:ci + 1, :]
            wsum = wsum + jnp.where(sel, e2_ref[h], jnp.zeros((), BF16)) * g18[h][ci:ci + 1, :]
        return wsum

    sub_per_piece = n_sub // DENSE_PIECES
    gates, a_pieces = [], []
    for pc in range(DENSE_PIECES):
        rows = slice(pc * sub_per_piece * N_KEYS, (pc + 1) * sub_per_piece * N_KEYS)
        lhs = u_ref[rows, :]
        if pc > 0:
            zero = gates[-1][0:16, 0:128] * jnp.zeros((), BF16)
            top = jnp.concatenate([lhs[0:16, 0:128] + zero, lhs[0:16, 128:]], axis=1)
            lhs = jnp.concatenate([top, lhs[16:]], axis=0)
        a_pieces.append(jnp.dot(lhs, h2t_ref[...], preferred_element_type=F32))
        gates.extend(gate(ci) for ci in range(pc * sub_per_piece, (pc + 1) * sub_per_piece))
    a = jnp.concatenate(a_pieces, axis=0)
    w_parts = [_gelu_tanh(a[ci * N_KEYS:(ci + 1) * N_KEYS, :].astype(BF16)) * gates[ci] for ci in range(n_sub)]
    zero = gates[-1][0:16, 0:128] * jnp.zeros((), BF16)
    w0 = w_parts[0]
    top = jnp.concatenate([w0[0:16, 0:128] + zero, w0[0:16, 128:]], axis=1)
    w_parts[0] = jnp.concatenate([top, w0[16:]], axis=0)
    w = jnp.concatenate(w_parts, axis=0)
    acc_ref[...] += jnp.dot(vt_ref[0], w, preferred_element_type=F32)

    @pl.when(c == pl.num_programs(1) - 1)
    def _():
        r = _cond_row(tt * (DENSE_TM // TILE))
        ffn = acc_ref[...].T
        y = _layer_norm(ALPHA * x1_ref[...] + _mod_rows(mod_ref, r, 5) * ffn, ln_g_ref[...], ln_b_ref[...])

        @pl.when(tt < DENSE_CTX_TILES)
        def _():
            yp_ref[...] = y

        @pl.when(tt >= DENSE_CTX_TILES)
        def _():
            ys_ref[...] = y


def _dense_call(h2t, cnt, g1, r2, e2, u_bf, vt_bf, x1, mod, ln_g, ln_b):
    full = lambda a: pl.BlockSpec(a.shape, lambda t, c: (0,) * a.ndim)
    hk = pl.BlockSpec((PEER_HEADS, N_KEYS, DENSE_TM), lambda t, c: (0, 0, t))
    n_half = N_CTX_SEQ * CTX_LEN
    return pl.pallas_call(
        _dense_kernel,
        grid=(N_TOK // DENSE_TM, N_EXPERTS // DENSE_CH),
        in_specs=[pl.BlockSpec((D_MODEL, DENSE_TM), lambda t, c: (0, t)),
                  hk, hk, hk, hk,
                  pl.BlockSpec((DENSE_CH, D_MODEL), lambda t, c: (c, 0)),
                  pl.BlockSpec((1, D_MODEL, DENSE_CH), lambda t, c: (c, 0, 0)),
                  pl.BlockSpec((DENSE_TM, D_MODEL), lambda t, c: (t, 0)),
                  full(mod), full(ln_g), full(ln_b)],
        out_specs=[pl.BlockSpec((DENSE_TM, D_MODEL), lambda t, c: (jnp.minimum(t, DENSE_CTX_TILES - 1), 0)),
                   pl.BlockSpec((DENSE_TM, D_MODEL), lambda t, c: (jnp.maximum(t - DENSE_CTX_TILES, 0), 0))],
        out_shape=[jax.ShapeDtypeStruct((n_half, D_MODEL), F32),
                   jax.ShapeDtypeStruct((N_TOK - n_half, D_MODEL), F32)],
        scratch_shapes=[pltpu.VMEM((D_MODEL, DENSE_TM), F32)],
        compiler_params=_cparams(("arbitrary", "arbitrary")),
        name="peer_dense",
    )(h2t, cnt, g1, r2, e2, u_bf, vt_bf, x1, mod, ln_g, ln_b)


def _grid_pos_embed():
    rows = LAT_LEN // GRID_W
    r, col = np.meshgrid(np.arange(rows), np.arange(GRID_W), indexing="ij")

    def sincos(pos, dim):
        omega = 1.0 / (10000.0 ** (np.arange(dim // 2, dtype=np.float64) / (dim // 2)))
        ang = pos.reshape(-1).astype(np.float64)[:, None] * omega[None, :]
        return np.concatenate([np.sin(ang), np.cos(ang)], axis=-1)

    pe = np.concatenate([sincos(r, D_MODEL // 2), sincos(col, D_MODEL // 2)], axis=-1)
    return jnp.asarray(pe, F32)


def _state_pack_t(state):
    packed = jnp.transpose(state, (0, 3, 1, 2)).reshape(N_LAT_SEQ, DV, D_K)
    return jnp.concatenate([jnp.zeros((1, DV, D_K), F32), packed], axis=0)


def _state_unpack_t(st):
    return jnp.transpose(st.reshape(N_CTX_SEQ, DV, HEADS, DK), (0, 2, 3, 1))[:, None]


def kernel(x_prompt, x_sample, c, state_fwd, state_bwd, c_ctx, w_ada, b_ada, w_in, pool_w, pool_scale,
           gk_w, gk_b, gla_norm_g, w_o, ln1_g, ln1_b, peer_wq, peer_keys, peer_u, peer_v, ln2_g, ln2_b):
    xp2 = x_prompt.reshape(N_CTX_SEQ * CTX_LEN, D_MODEL)
    xs2 = x_sample.reshape(N_LAT_SEQ * LAT_LEN, D_MODEL)
    pos = _grid_pos_embed()

    cond8 = jnp.zeros((8, D_MODEL), F32).at[0].set(c_ctx).at[1:3].set(c)
    mod = _mod_call(cond8, w_ada[0], b_ada[0][None, :])

    w = w_in[0]
    w_in_r = jnp.concatenate([w[:, :1536], w[:, 1568:], w[:, 1536:1568],
                              jnp.zeros((D_MODEL, P_COLS - 2080), F32)], axis=1).astype(BF16)
    gkw_bd = jnp.zeros((128, 2 * D_K), F32)
    gkw_bd = gkw_bd.at[0:GATE_RANK, 0:D_K].set(gk_w[0, 0]).at[GATE_RANK:2 * GATE_RANK, D_K:].set(gk_w[0, 1])
    gkb = gk_b[0].reshape(1, 2 * D_K)
    gkw2 = jnp.concatenate([gkw_bd, gkw_bd], axis=0).astype(BF16)
    xpool, q, k, v, gk, ogs = _proj_call(xp2, xs2, pos, mod, w_in_r, gkw2, gkb)

    o_f, st_f, wqt = _gla_fwd_call(q, k, v, gk, _state_pack_t(state_fwd[:, 0]), _gla_constants(False),
                                   peer_wq[0])
    x1, st_b = _gla_bwd_call(
        q, k, v, gk, _state_pack_t(state_bwd[:, 0]), _gla_constants(True), o_f, xpool, ogs,
        xp2, xs2, pos, mod, _pool_band(), pool_w[0].astype(BF16), pool_scale[0][None, :],
        gla_norm_g[0][None, :], w_o[0].astype(BF16), ln1_g[0][None, :], ln1_b[0][None, :])

    keys = peer_keys[0].reshape(2 * PEER_HEADS, N_KEYS, N_KEYS).astype(BF16)
    h2t, cnt, g1, r2, e2, u_bf, vt_bf = _route_call(x1, mod, wqt, keys, peer_u[0], peer_v[0])
    yp, ys = _dense_call(h2t, cnt, g1, r2, e2, u_bf, vt_bf, x1, mod, ln2_g[0][None, :], ln2_b[0][None, :])

    y_prompt = yp.reshape(N_CTX_SEQ, CTX_LEN, D_MODEL)
    y_sample = ys.reshape(N_LAT_SEQ, LAT_LEN, D_MODEL)
    return (y_prompt, y_sample, _state_unpack_t(st_f), _state_unpack_t(st_b))
```

```python
import math

import numpy as np
import jax
import jax.numpy as jnp
from jax import lax
from jax.experimental import pallas as pl
from jax.experimental.pallas import tpu as pltpu

F32 = jnp.float32
BF16 = jnp.bfloat16

D_MODEL = 1024
N_CTX_SEQ, CTX_LEN = 16, 256
N_LAT_SEQ, LAT_LEN = 2, 2048
GRID_W = 64
N_TOK = N_CTX_SEQ * CTX_LEN + N_LAT_SEQ * LAT_LEN
TILE = 256
N_TILES = N_TOK // TILE
CTX_TILES = N_CTX_SEQ * CTX_LEN // TILE
LAT_TILES = LAT_LEN // TILE

D_POOL = 512
POOL_GROUPS = 4
POOL_CH = 128
POOL_WINDOWS = (2, 4, 8, 16)
POOL_HALO = 8
HEADS = 4
DK = 64
DV = 128
D_K = HEADS * DK
D_V = HEADS * DV
GATE_RANK = 16
GATE_NORM = 16.0
CHUNK = 64
N_LEVELS = 7
P_COLS = 2048 + 128

PEER_HEADS = 8
N_KEYS = 128
PEER_TOPK = 16
N_EXPERTS = N_KEYS * N_KEYS
WQ_SLABS = PEER_HEADS * 2 * N_KEYS // 128
DENSE_TM = 512
DENSE_CH = 2048
DENSE_CTX_TILES = N_CTX_SEQ * CTX_LEN // DENSE_TM
DENSE_PIECES = 16
MOD_COLS = 1536
ALPHA = 2.0 ** 0.25
EPS = 1e-5
V7X_VMEM_BYTES = 64 * 1024 * 1024
VMEM_LIMIT = V7X_VMEM_BYTES * 7 // 8


def _cparams(sem):
    return pltpu.CompilerParams(dimension_semantics=sem, vmem_limit_bytes=VMEM_LIMIT)


def _silu(x):
    return x * jax.nn.sigmoid(x)


def _split_bf16(x, pieces):
    out = []
    for _ in range(pieces - 1):
        hi = x.astype(BF16)
        out.append(hi)
        x = x - hi.astype(F32)
    out.append(x.astype(BF16))
    return out


def _dot_exact(m01, x, pieces=3):
    c = x.shape[1]
    xs = jnp.concatenate(_split_bf16(x, pieces), axis=1)
    y = jnp.dot(m01, xs, preferred_element_type=F32)
    out = y[:, :c]
    for p in range(1, pieces):
        out = out + y[:, p * c:(p + 1) * c]
    return out


def _layer_norm(y, g, b):
    mu = jnp.mean(y, axis=-1, keepdims=True)
    yc = y - mu
    var = jnp.mean(yc * yc, axis=-1, keepdims=True)
    return yc * lax.rsqrt(var + EPS) * g + b


def _cond_row(tile):
    return (tile >= CTX_TILES).astype(jnp.int32) + (tile >= CTX_TILES + LAT_TILES).astype(jnp.int32)


def _mod_kernel(cond_ref, w_ref, b_ref, out_ref):
    s = _silu(cond_ref[...]).astype(BF16)
    out_ref[...] = jnp.dot(s, w_ref[...].astype(BF16), preferred_element_type=F32) + b_ref[...]


def _mod_call(cond8, w_ada, b_ada):
    n = w_ada.shape[1]
    bn = MOD_COLS
    return pl.pallas_call(
        _mod_kernel,
        grid=(n // bn,),
        in_specs=[pl.BlockSpec((8, D_MODEL), lambda j: (0, 0)),
                  pl.BlockSpec((D_MODEL, bn), lambda j: (0, j)),
                  pl.BlockSpec((1, bn), lambda j: (0, j))],
        out_specs=pl.BlockSpec((8, bn), lambda j: (0, j)),
        out_shape=jax.ShapeDtypeStruct((8, n), F32),
        compiler_params=_cparams(("arbitrary",)),
        name="mod",
    )(cond8, w_ada, b_ada)


def _load_x(i, xp_ref, xs_ref, pos_ref):
    return jnp.where(i < CTX_TILES, xp_ref[...], xs_ref[...] + pos_ref[...])


def _mod_rows(mod_ref, r, k):
    return mod_ref[pl.ds(r, 1), k * D_MODEL:(k + 1) * D_MODEL]


def _proj_kernel(xp_ref, xs_ref, pos_ref, mod_ref, w_ref, gkw_ref, gkb_ref,
                 oxp, oq, ok, ov, ogk, oog):
    i = pl.program_id(0)
    r = _cond_row(i)
    x = _load_x(i, xp_ref, xs_ref, pos_ref)
    h = x * (1.0 + _mod_rows(mod_ref, r, 1)) + _mod_rows(mod_ref, r, 0)
    p = jnp.dot(h.astype(BF16), w_ref[...], preferred_element_type=F32)
    oxp[...] = p[:, 0:512]
    oq[...] = p[:, 512:768] * (DK ** -0.5)
    ok[...] = p[:, 768:1024]
    ov[...] = p[:, 1024:1536]
    oog[...] = _silu(p[:, 1536:2048])
    glr = p[:, 2048:2176]
    g2 = jnp.concatenate(_split_bf16(glr, 2), axis=1)
    pre = jnp.dot(g2, gkw_ref[...], preferred_element_type=F32) + gkb_ref[...]
    ogk[...] = (jnp.minimum(pre, 0.0) - jnp.log(1.0 + jnp.exp(-jnp.abs(pre)))) * (1.0 / GATE_NORM)


def _x_specs():
    return [pl.BlockSpec((TILE, D_MODEL), lambda i: (jnp.minimum(i, CTX_TILES - 1), 0)),
            pl.BlockSpec((TILE, D_MODEL), lambda i: (jnp.maximum(i - CTX_TILES, 0), 0)),
            pl.BlockSpec((TILE, D_MODEL), lambda i: (jnp.maximum(i - CTX_TILES, 0) % LAT_TILES, 0))]


def _proj_call(xp2, xs2, pos, mod, w_in_r, gkw_bd, gkb):
    tok = lambda c: pl.BlockSpec((TILE, c), lambda i: (i, 0))
    full = lambda a: pl.BlockSpec(a.shape, lambda i: (0,) * a.ndim)
    outs = [512, 256, 256, 512, 512, 512]
    return pl.pallas_call(
        _proj_kernel,
        grid=(N_TILES,),
        in_specs=_x_specs() + [full(mod), full(w_in_r), full(gkw_bd), full(gkb)],
        out_specs=[tok(c) for c in outs],
        out_shape=[jax.ShapeDtypeStruct((N_TOK, c), F32) for c in outs],
        compiler_params=_cparams(("arbitrary",)),
        name="proj",
    )(xp2, xs2, pos, mod, w_in_r, gkw_bd, gkb)


def _gla_constants(rev):
    t = np.arange(TILE)
    same_chunk = (t[:, None] // CHUNK) == (t[None, :] // CHUNK)
    tri = same_chunk & ((t[None, :] >= t[:, None]) if rev else (t[None, :] <= t[:, None]))
    u = np.arange(CHUNK)
    x = u[:, None] ^ u[None, :]
    lvl = np.where(x > 0, np.floor(np.log2(np.maximum(x, 1))), N_LEVELS - 1).astype(np.int32)
    causal = (u[:, None] <= u[None, :]) if rev else (u[:, None] >= u[None, :])
    lvl = np.where(causal, lvl, -1).astype(np.int32)
    return jnp.asarray(tri, BF16), jnp.asarray(np.tile(lvl, (1, HEADS)), jnp.int32)


def _level_ref(b_scr, level, rev):
    half = 1 << level
    sub = lax.broadcasted_iota(jnp.int32, (8, D_K), 0)
    row = lambda r: jnp.broadcast_to(b_scr[r:r + 1, :], (8, D_K))
    pieces = []
    for m in range(b_scr.shape[0] // 8):
        refs = [((8 * m + u) // (2 * half)) * (2 * half) + (half if rev else half - 1) for u in range(8)]
        piece = row(refs[0])
        for u in range(1, 8):
            if refs[u] != refs[u - 1]:
                piece = jnp.where(sub >= u, row(refs[u]), piece)
        pieces.append(piece)
    return jnp.concatenate(pieces, axis=0)


def _head_blocks(x, width):
    head = lax.broadcasted_iota(jnp.int32, x.shape, 1) // width
    return jnp.concatenate([jnp.where(head == h, x, jnp.zeros_like(x)) for h in range(HEADS)], axis=0)


def _gla_tile(q, k, v, g, tri, lmap, st, b_scr, rev, restart=None):
    n_rows = q.shape[0]
    b = jnp.concatenate([_dot_exact(tri, g[t:t + TILE]) for t in range(0, n_rows, TILE)], axis=0)
    b_scr[...] = b
    qe = (q * jnp.exp(b)).astype(BF16)
    vb = v.astype(BF16)

    cdim = (((1,), (1,)), ((), ()))
    st_head = lax.broadcasted_iota(jnp.int32, (DV, D_K), 1) // DK
    n_chunks = n_rows // CHUNK
    rows = [slice(c * CHUNK, (c + 1) * CHUNK) for c in range(n_chunks)]
    b_ends, kvds = [], []
    for c in range(n_chunks):
        b_c = b[rows[c]]
        b_end = b_c[0:1, :] if rev else b_c[CHUNK - 1:CHUNK, :]
        kdec = (k[rows[c]] * jnp.exp(b_end - b_c)).astype(BF16)
        kv = lax.dot_general(vb[rows[c]], kdec, (((0,), (0,)), ((), ())), preferred_element_type=F32)
        kvd = kv[0:DV]
        for h in range(1, HEADS):
            kvd = jnp.where(st_head == h, kv[h * DV:(h + 1) * DV], kvd)
        b_ends.append(b_end)
        kvds.append(kvd)
    qds, kds = [], []
    for l in range(N_LEVELS - 1):
        d = b - _level_ref(b_scr, l, rev)
        qds.append((q * jnp.exp(jnp.minimum(d, 0.0))).astype(BF16))
        kds.append((k * jnp.exp(jnp.minimum(-d, 0.0))).astype(BF16))
    qds.append(q.astype(BF16))
    kds.append(k.astype(BF16))
    ps = []
    for c in range(n_chunks):
        p = jnp.zeros((CHUNK, HEADS * CHUNK), F32)
        for l in range(N_LEVELS):
            z = lax.dot_general(qds[l][rows[c]], _head_blocks(kds[l][rows[c]], DK), cdim,
                                preferred_element_type=F32)
            p = jnp.where(lmap == l, z, p)
        ps.append(p.astype(BF16))
    intra = [jnp.dot(ps[c], _head_blocks(vb[rows[c]], DV), preferred_element_type=F32) for c in range(n_chunks)]
    outs = [None] * n_chunks
    tile_states = []
    for n, c in enumerate(reversed(range(n_chunks)) if rev else range(n_chunks)):
        if n > 0 and n % (TILE // CHUNK) == 0:
            tile_states.append(st)
            if restart is not None:
                st = jnp.where(restart[0], restart[1], st)
        o = lax.dot_general(qe[rows[c]], _head_blocks(st.astype(BF16), DK), cdim,
                            preferred_element_type=F32)
        outs[c] = o + intra[c]
        st = st * jnp.exp(b_ends[c]) + kvds[c]
    tile_states.append(st)
    return jnp.concatenate(outs, axis=0), tile_states


FWD_TILES = 4


def _gla_fwd_kernel(q_ref, k_ref, v_ref, g_ref, st0_ref, tri_ref, lmap_ref, wq_ref,
                    of_ref, stout_ref, wqt_ref, st_scr, b_scr):
    i = pl.program_id(0)
    first = FWD_TILES * i
    is_ctx = first < CTX_TILES
    seq_start = is_ctx | ((first - CTX_TILES) % LAT_TILES == 0)

    wqt_ref[...] = wq_ref[...].T.astype(BF16)

    @pl.when(i == 0)
    def _():
        st_scr[...] = jnp.zeros_like(st_scr)

    st_in = jnp.where(seq_start, st0_ref[0], st_scr[...])
    o, states = _gla_tile(q_ref[...], k_ref[...], v_ref[...], g_ref[...], tri_ref[...], lmap_ref[...],
                          st_in, b_scr, rev=False, restart=(is_ctx, st0_ref[0]))
    of_ref[...] = o
    st_scr[...] = states[-1]

    @pl.when(is_ctx)
    def _():
        for t in range(FWD_TILES):
            stout_ref[t] = states[t]


def _gla_fwd_call(q, k, v, gk, st0, consts, wq):
    tri, lmap = consts
    rows = FWD_TILES * TILE
    steps = N_TILES // FWD_TILES
    assert WQ_SLABS % steps == 0 and CTX_TILES % FWD_TILES == 0 and LAT_TILES % FWD_TILES == 0
    wq_cols = 128 * WQ_SLABS // steps
    tok = lambda c: pl.BlockSpec((rows, c), lambda i: (i, 0))
    full = lambda a: pl.BlockSpec(a.shape, lambda i: (0,) * a.ndim)
    return pl.pallas_call(
        _gla_fwd_kernel,
        grid=(steps,),
        in_specs=[tok(D_K), tok(D_K), tok(D_V), pl.BlockSpec((rows, D_K), lambda i: (i, 0)),
                  pl.BlockSpec((1, DV, D_K), lambda i: (_cond_row(FWD_TILES * i), 0, 0)),
                  full(tri), full(lmap),
                  pl.BlockSpec((D_MODEL, wq_cols), lambda i: (0, i))],
        out_specs=[tok(D_V),
                   pl.BlockSpec((FWD_TILES, DV, D_K),
                                lambda i: (jnp.minimum(i, CTX_TILES // FWD_TILES - 1), 0, 0)),
                   pl.BlockSpec((wq_cols, D_MODEL), lambda i: (i, 0))],
        out_shape=[jax.ShapeDtypeStruct((N_TOK, D_V), F32),
                   jax.ShapeDtypeStruct((N_CTX_SEQ, DV, D_K), F32),
                   jax.ShapeDtypeStruct((WQ_SLABS * 128, D_MODEL), BF16)],
        scratch_shapes=[pltpu.VMEM((DV, D_K), F32), pltpu.VMEM((rows, D_K), F32)],
        compiler_params=_cparams(("arbitrary",)),
        name="gla_fwd",
    )(q, k, v, gk, st0, tri, lmap, wq)


def _pool_band():
    t = np.arange(TILE)[:, None]
    e = np.arange(TILE + 2 * POOL_HALO)[None, :]
    bands = []
    for w in POOL_WINDOWS:
        lo = t + POOL_HALO - w // 2
        bands.append(((e >= lo) & (e < lo + w)).astype(np.float32))
    return jnp.asarray(np.stack(bands), BF16)


def _gla_bwd_kernel(q_ref, k_ref, v_ref, g_ref, st0_ref, tri_ref, lmap_ref,
                    of_ref, xpool_ref, prev_ref, next_ref, ogs_ref, xp_ref, xs_ref, pos_ref, mod_ref,
                    band_ref, poolw_ref, pscale_ref, gnorm_ref, wo_ref, ln_g_ref, ln_b_ref,
                    x1_ref, stout_ref, st_scr, b_scr):
    j = N_TILES - 1 - pl.program_id(0)
    lat_idx = jnp.maximum(j - CTX_TILES, 0) % LAT_TILES
    is_ctx = j < CTX_TILES
    seq_first = is_ctx | (lat_idx == 0)
    seq_last = is_ctx | (lat_idx == LAT_TILES - 1)

    @pl.when(seq_last)
    def _():
        st_scr[...] = st0_ref[0]

    xpool = xpool_ref[...]
    prev = jnp.where(seq_first, 0.0, prev_ref[...])
    nxt = jnp.where(seq_last, 0.0, next_ref[...])
    ext = jnp.concatenate([prev, xpool, nxt], axis=0)
    tpos = jnp.where(is_ctx, 0, lat_idx) * TILE + lax.broadcasted_iota(jnp.int32, (TILE, POOL_CH), 0)
    seq_len = jnp.where(is_ctx, CTX_LEN, LAT_LEN)
    group = [slice(gi * POOL_CH, (gi + 1) * POOL_CH) for gi in range(POOL_GROUPS)]
    wsums = [_dot_exact(band_ref[gi], ext[:, group[gi]], pieces=2)
             for gi in range(POOL_GROUPS)]
    dmeans = []
    for gi, w in enumerate(POOL_WINDOWS):
        lo = jnp.maximum(tpos - w // 2, 0)
        hi = jnp.minimum(tpos - w // 2 + w, seq_len)
        dmeans.append((wsums[gi] / (hi - lo).astype(F32) - xpool[:, group[gi]]).astype(BF16))
    ys = [jnp.dot(dmeans[gi], poolw_ref[gi], preferred_element_type=F32) for gi in range(POOL_GROUPS)]
    pool_parts = [ys[gi] * pscale_ref[:, group[gi]] for gi in range(POOL_GROUPS)]

    o_b, states = _gla_tile(q_ref[...], k_ref[...], v_ref[...], g_ref[...], tri_ref[...],
                            lmap_ref[...], st_scr[...], b_scr, rev=True)
    st_scr[...] = states[-1]

    o = of_ref[...] + o_b
    ogs = ogs_ref[...]
    head = [slice(h * DV, (h + 1) * DV) for h in range(HEADS)]
    inv_rms = [lax.rsqrt(jnp.mean(o[:, sl] * o[:, sl], axis=-1, keepdims=True) + EPS) for sl in head]
    pool_parts += [o[:, sl] * inv_rms[h] * gnorm_ref[:, sl] * ogs[:, sl] for h, sl in enumerate(head)]
    mix_in = jnp.concatenate(pool_parts, axis=1).astype(BF16)
    mix = jnp.dot(mix_in, wo_ref[...], preferred_element_type=F32)

    r = _cond_row(j)
    x = _load_x(j, xp_ref, xs_ref, pos_ref)
    y = ALPHA * x + _mod_rows(mod_ref, r, 2) * mix
    x1_ref[...] = _layer_norm(y, ln_g_ref[...], ln_b_ref[...])

    @pl.when(is_ctx)
    def _():
        stout_ref[0] = st_scr[...]


def _gla_bwd_call(q, k, v, gk, st0, consts, o_f, xpool, ogs, xp2, xs2, pos, mod,
                  band, poolw, pscale, gnorm, wo, ln_g, ln_b):
    tri, lmap = consts
    rv = lambda i: N_TILES - 1 - i
    tok = lambda c: pl.BlockSpec((TILE, c), lambda i: (rv(i), 0))
    full = lambda a: pl.BlockSpec(a.shape, lambda i: (0,) * a.ndim)
    halo_blocks = TILE // POOL_HALO
    n_halo = N_TOK // POOL_HALO
    xspecs = [pl.BlockSpec((TILE, D_MODEL), lambda i: (jnp.minimum(rv(i), CTX_TILES - 1), 0)),
              pl.BlockSpec((TILE, D_MODEL), lambda i: (jnp.maximum(rv(i) - CTX_TILES, 0), 0)),
              pl.BlockSpec((TILE, D_MODEL), lambda i: (jnp.maximum(rv(i) - CTX_TILES, 0) % LAT_TILES, 0))]
    return pl.pallas_call(
        _gla_bwd_kernel,
        grid=(N_TILES,),
        in_specs=[tok(D_K), tok(D_K), tok(D_V), pl.BlockSpec((TILE, D_K), lambda i: (rv(i), 1)),
                  pl.BlockSpec((1, DV, D_K), lambda i: (_cond_row(rv(i)), 0, 0)),
                  full(tri), full(lmap),
                  tok(D_V), tok(D_POOL),
                  pl.BlockSpec((POOL_HALO, D_POOL), lambda i: (jnp.maximum(rv(i) * halo_blocks - 1, 0), 0)),
                  pl.BlockSpec((POOL_HALO, D_POOL),
                               lambda i: (jnp.minimum((rv(i) + 1) * halo_blocks, n_halo - 1), 0)),
                  tok(D_V)] + xspecs + [full(mod), full(band), full(poolw), full(pscale), full(gnorm),
                                        full(wo), full(ln_g), full(ln_b)],
        out_specs=[tok(D_MODEL),
                   pl.BlockSpec((1, DV, D_K), lambda i: (jnp.minimum(rv(i), CTX_TILES - 1), 0, 0))],
        out_shape=[jax.ShapeDtypeStruct((N_TOK, D_MODEL), F32),
                   jax.ShapeDtypeStruct((N_CTX_SEQ, DV, D_K), F32)],
        scratch_shapes=[pltpu.VMEM((DV, D_K), F32), pltpu.VMEM((TILE, D_K), F32)],
        compiler_params=_cparams(("arbitrary",)),
        name="gla_bwd_mix",
    )(q, k, v, gk, st0, tri, lmap, o_f, xpool, xpool, xpool, ogs, xp2, xs2, pos, mod,
      band, poolw, pscale, gnorm, wo, ln_g, ln_b)


def _top_values(x, n, with_rank=False, first_only=False):
    vals = []
    rank = jnp.full(x.shape, float(n), F32)
    row = lax.broadcasted_iota(jnp.int32, x.shape, 0) if first_only else None
    for j in range(n):
        m = jnp.max(x, axis=0, keepdims=True)
        vals.append(m)
        hit = x == m
        if first_only:
            hit = row == jnp.min(jnp.where(hit, row, x.shape[0]), axis=0, keepdims=True)
        if with_rank:
            rank = jnp.where(hit, float(j), rank)
        x = jnp.where(hit, -jnp.inf, x)
    return (vals, rank) if with_rank else vals


def _route_select(s1, s2, exact_ties):
    n_tok = s1.shape[1]
    sv1, rank1 = _top_values(s1, PEER_TOPK, with_rank=True, first_only=exact_ties)
    sv2, rank2 = _top_values(s2, PEER_TOPK, with_rank=True, first_only=exact_ties)
    a1 = jnp.concatenate(sv1, axis=0)
    a2 = jnp.concatenate(sv2, axis=0)
    row8 = lax.broadcasted_iota(jnp.int32, (8, n_tok), 0)
    cand = [a2 + sv1[0]]
    for a in range(1, 8):
        cand.append(jnp.where(row8 < PEER_TOPK // (a + 1), a2[0:8] + sv1[a], -jnp.inf))
    cand.append(a1[8:16] + sv2[0])
    cand_all = jnp.concatenate(cand, axis=0)
    if exact_ties:
        fv, crank = _top_values(cand_all, PEER_TOPK, with_rank=True, first_only=True)
        taken = (crank < float(PEER_TOPK)).astype(F32)
        blocks = [taken[0:16]] + [taken[8 + 8 * a:16 + 8 * a] for a in range(1, 8)]
        blocks += [taken[72 + a:73 + a] for a in range(8)]
    else:
        fv = _top_values(cand_all, PEER_TOPK)
        thr = fv[PEER_TOPK - 1]
        blocks = [(cand[a] >= thr).astype(F32) for a in range(8)]
        blocks += [(cand[8][a:a + 1] >= thr).astype(F32) for a in range(8)]
    denom = jnp.zeros_like(fv[0])
    for f in fv:
        denom = denom + jnp.exp(f - fv[0])
    rank1 = rank1.astype(BF16)
    rank2 = rank2.astype(BF16)
    cnt = jnp.zeros_like(rank1)
    n_sel = jnp.zeros((1, n_tok), F32)
    for a in range(PEER_TOPK):
        n_a = jnp.sum(blocks[a], axis=0, keepdims=True)
        n_sel = n_sel + n_a
        cnt = jnp.where(rank1 == float(a), n_a.astype(BF16), cnt)
    g1 = (jnp.exp(s1 - sv1[0]) / denom).astype(BF16)
    e2 = jnp.exp(s2 - sv2[0]).astype(BF16)
    if exact_ties:
        return cnt, g1, rank2, e2, jnp.zeros((1, n_tok), F32)
    took = lambda rk: jnp.sum((rk < float(PEER_TOPK)).astype(F32), axis=0, keepdims=True)
    k = float(PEER_TOPK)
    tied = ((took(rank1) != k) | (took(rank2) != k) | (n_sel != k)).astype(F32)
    return cnt, g1, rank2, e2, tied


def _route_kernel(x1_ref, mod_ref, wqt_ref, keys_ref, u_ref, v_ref,
                  h2t_ref, cnt_ref, g1_ref, r2_ref, e2_ref, ubf_ref, vtbf_ref, s_scr, tied_ref):
    i = pl.program_id(0)
    r = _cond_row(i)
    ubf_ref[...] = u_ref[...].astype(BF16)
    vtbf_ref[0] = v_ref[...].T.astype(BF16)
    h2 = x1_ref[...] * (1.0 + _mod_rows(mod_ref, r, 4)) + _mod_rows(mod_ref, r, 3)
    h2t = h2.T.astype(BF16)
    h2t_ref[...] = h2t
    qt = jnp.dot(wqt_ref[...], h2t, preferred_element_type=F32)

    def store(h, sel):
        cnt_ref[h], g1_ref[h], r2_ref[h], e2_ref[h] = sel[0], sel[1], sel[2], sel[3]

    for h in range(PEER_HEADS):
        s = []
        for p in range(2):
            row = (2 * h + p) * N_KEYS
            qhp = qt[row:row + N_KEYS, :].astype(BF16)
            s.append(jnp.dot(keys_ref[2 * h + p], qhp, preferred_element_type=F32))
            s_scr[2 * h + p] = s[p]
        sel = _route_select(s[0], s[1], exact_ties=False)
        store(h, sel)
        tied_ref[h] = jnp.max(sel[4])

    def redo(h, carry):
        @pl.when(tied_ref[h] > 0.0)
        def _():
            store(h, _route_select(s_scr[2 * h], s_scr[2 * h + 1], exact_ties=True))
        return carry
    lax.fori_loop(0, PEER_HEADS, redo, 0)


def _route_call(x1, mod, wqt, keys, u, v):
    full = lambda a: pl.BlockSpec(a.shape, lambda i: (0,) * a.ndim)
    hk = pl.BlockSpec((PEER_HEADS, N_KEYS, TILE), lambda i: (0, 0, i))
    per_key = lambda dt: jax.ShapeDtypeStruct((PEER_HEADS, N_KEYS, N_TOK), dt)
    slab = N_EXPERTS // N_TILES
    slabs_per_block = DENSE_CH // slab
    table = pl.BlockSpec((slab, D_MODEL), lambda i: (i, 0))
    return pl.pallas_call(
        _route_kernel,
        grid=(N_TILES,),
        in_specs=[pl.BlockSpec((TILE, D_MODEL), lambda i: (i, 0)), full(mod), full(wqt), full(keys),
                  table, table],
        out_specs=[pl.BlockSpec((D_MODEL, TILE), lambda i: (0, i)), hk, hk, hk, hk, table,
                   pl.BlockSpec((1, D_MODEL, slab), lambda i: (i // slabs_per_block, 0, i % slabs_per_block))],
        out_shape=[jax.ShapeDtypeStruct((D_MODEL, N_TOK), BF16),
                   per_key(BF16), per_key(BF16), per_key(BF16), per_key(BF16),
                   jax.ShapeDtypeStruct((N_EXPERTS, D_MODEL), BF16),
                   jax.ShapeDtypeStruct((N_EXPERTS // DENSE_CH, D_MODEL, DENSE_CH), BF16)],
        scratch_shapes=[pltpu.VMEM((2 * PEER_HEADS, N_KEYS, TILE), F32), pltpu.SMEM((PEER_HEADS,), F32)],
        compiler_params=_cparams(("arbitrary",)),
        name="peer_route",
    )(x1, mod, wqt, keys, u, v)


def _gelu_tanh(x):
    c1 = math.sqrt(2.0 / math.pi)
    c2 = c1 * 0.044715
    return x * (0.5 + 0.5 * jnp.tanh(x * (c1 + c2 * (x * x))))


def _dense_kernel(h2t_ref, cnt_ref, g1_ref, r2_ref, e2_ref, u_ref, vt_ref, x1_ref, mod_ref,
                  ln_g_ref, ln_b_ref, yp_ref, ys_ref, acc_ref):
    tt = pl.program_id(0)
    c = pl.program_id(1)
    n_sub = DENSE_CH // N_KEYS
    assert n_sub % 16 == 0, "count / gate rows are read as whole packed bf16 row groups"

    @pl.when(c == 0)
    def _():
        acc_ref[...] = jnp.zeros_like(acc_ref)

    i1_base = pl.multiple_of(c * n_sub, n_sub)
    cnt8 = [cnt_ref[h, pl.ds(i1_base, n_sub), :] for h in range(PEER_HEADS)]
    g18 = [g1_ref[h, pl.ds(i1_base, n_sub), :] for h in range(PEER_HEADS)]
    def gate(ci):
        wsum = jnp.zeros((N_KEYS, DENSE_TM), BF16)
        for h in range(PEER_HEADS):
            sel = r2_ref[h] < cnt8[h][ci:ci + 1, :]
            wsum = wsum + jnp.where(sel, e2_ref[h], jnp.zeros((), BF16)) * g18[h][ci:ci + 1, :]
        return wsum

    sub_per_piece = n_sub // DENSE_PIECES
    gates, a_pieces = [], []
    for pc in range(DENSE_PIECES):
        rows = slice(pc * sub_per_piece * N_KEYS, (pc + 1) * sub_per_piece * N_KEYS)
        lhs = u_ref[rows, :]
        if pc > 0:
            zero = gates[-1][0:16, 0:128] * jnp.zeros((), BF16)
            top = jnp.concatenate([lhs[0:16, 0:128] + zero, lhs[0:16, 128:]], axis=1)
            lhs = jnp.concatenate([top, lhs[16:]], axis=0)
        a_pieces.append(jnp.dot(lhs, h2t_ref[...], preferred_element_type=F32))
        gates.extend(gate(ci) for ci in range(pc * sub_per_piece, (pc + 1) * sub_per_piece))
    a = jnp.concatenate(a_pieces, axis=0)
    w_parts = [_gelu_tanh(a[ci * N_KEYS:(ci + 1) * N_KEYS, :].astype(BF16)) * gates[ci] for ci in range(n_sub)]
    zero = gates[-1][0:16, 0:128] * jnp.zeros((), BF16)
    w0 = w_parts[0]
    top = jnp.concatenate([w0[0:16, 0:128] + zero, w0[0:16, 128:]], axis=1)
    w_parts[0] = jnp.concatenate([top, w0[16:]], axis=0)
    w = jnp.concatenate(w_parts, axis=0)
    acc_ref[...] += jnp.dot(vt_ref[0], w, preferred_element_type=F32)

    @pl.when(c == pl.num_programs(1) - 1)
    def _():
        r = _cond_row(tt * (DENSE_TM // TILE))
        ffn = acc_ref[...].T
        y = _layer_norm(ALPHA * x1_ref[...] + _mod_rows(mod_ref, r, 5) * ffn, ln_g_ref[...], ln_b_ref[...])

        @pl.when(tt < DENSE_CTX_TILES)
        def _():
            yp_ref[...] = y

        @pl.when(tt >= DENSE_CTX_TILES)
        def _():
            ys_ref[...] = y


def _dense_call(h2t, cnt, g1, r2, e2, u_bf, vt_bf, x1, mod, ln_g, ln_b):
    full = lambda a: pl.BlockSpec(a.shape, lambda t, c: (0,) * a.ndim)
    hk = pl.BlockSpec((PEER_HEADS, N_KEYS, DENSE_TM), lambda t, c: (0, 0, t))
    n_half = N_CTX_SEQ * CTX_LEN
    return pl.pallas_call(
        _dense_kernel,
        grid=(N_TOK // DENSE_TM, N_EXPERTS // DENSE_CH),
        in_specs=[pl.BlockSpec((D_MODEL, DENSE_TM), lambda t, c: (0, t)),
                  hk, hk, hk, hk,
                  pl.BlockSpec((DENSE_CH, D_MODEL), lambda t, c: (c, 0)),
                  pl.BlockSpec((1, D_MODEL, DENSE_CH), lambda t, c: (c, 0, 0)),
                  pl.BlockSpec((DENSE_TM, D_MODEL), lambda t, c: (t, 0)),
                  full(mod), full(ln_g), full(ln_b)],
        out_specs=[pl.BlockSpec((DENSE_TM, D_MODEL), lambda t, c: (jnp.minimum(t, DENSE_CTX_TILES - 1), 0)),
                   pl.BlockSpec((DENSE_TM, D_MODEL), lambda t, c: (jnp.maximum(t - DENSE_CTX_TILES, 0), 0))],
        out_shape=[jax.ShapeDtypeStruct((n_half, D_MODEL), F32),
                   jax.ShapeDtypeStruct((N_TOK - n_half, D_MODEL), F32)],
        scratch_shapes=[pltpu.VMEM((D_MODEL, DENSE_TM), F32)],
        compiler_params=_cparams(("arbitrary", "arbitrary")),
        name="peer_dense",
    )(h2t, cnt, g1, r2, e2, u_bf, vt_bf, x1, mod, ln_g, ln_b)


def _grid_pos_embed():
    rows = LAT_LEN // GRID_W
    r, col = np.meshgrid(np.arange(rows), np.arange(GRID_W), indexing="ij")

    def sincos(pos, dim):
        omega = 1.0 / (10000.0 ** (np.arange(dim // 2, dtype=np.float64) / (dim // 2)))
        ang = pos.reshape(-1).astype(np.float64)[:, None] * omega[None, :]
        return np.concatenate([np.sin(ang), np.cos(ang)], axis=-1)

    pe = np.concatenate([sincos(r, D_MODEL // 2), sincos(col, D_MODEL // 2)], axis=-1)
    return jnp.asarray(pe, F32)


def _state_pack_t(state):
    packed = jnp.transpose(state, (0, 3, 1, 2)).reshape(N_LAT_SEQ, DV, D_K)
    return jnp.concatenate([jnp.zeros((1, DV, D_K), F32), packed], axis=0)


def _state_unpack_t(st):
    return jnp.transpose(st.reshape(N_CTX_SEQ, DV, HEADS, DK), (0, 2, 3, 1))[:, None]


def kernel(x_prompt, x_sample, c, state_fwd, state_bwd, c_ctx, w_ada, b_ada, w_in, pool_w, pool_scale,
           gk_w, gk_b, gla_norm_g, w_o, ln1_g, ln1_b, peer_wq, peer_keys, peer_u, peer_v, ln2_g, ln2_b):
    xp2 = x_prompt.reshape(N_CTX_SEQ * CTX_LEN, D_MODEL)
    xs2 = x_sample.reshape(N_LAT_SEQ * LAT_LEN, D_MODEL)
    pos = _grid_pos_embed()

    cond8 = jnp.zeros((8, D_MODEL), F32).at[0].set(c_ctx).at[1:3].set(c)
    mod = _mod_call(cond8, w_ada[0], b_ada[0][None, :])

    w = w_in[0]
    w_in_r = jnp.concatenate([w[:, :1536], w[:, 1568:], w[:, 1536:1568],
                              jnp.zeros((D_MODEL, P_COLS - 2080), F32)], axis=1).astype(BF16)
    gkw_bd = jnp.zeros((128, 2 * D_K), F32)
    gkw_bd = gkw_bd.at[0:GATE_RANK, 0:D_K].set(gk_w[0, 0]).at[GATE_RANK:2 * GATE_RANK, D_K:].set(gk_w[0, 1])
    gkb = gk_b[0].reshape(1, 2 * D_K)
    gkw2 = jnp.concatenate([gkw_bd, gkw_bd], axis=0).astype(BF16)
    xpool, q, k, v, gk, ogs = _proj_call(xp2, xs2, pos, mod, w_in_r, gkw2, gkb)

    o_f, st_f, wqt = _gla_fwd_call(q, k, v, gk, _state_pack_t(state_fwd[:, 0]), _gla_constants(False),
                                   peer_wq[0])
    x1, st_b = _gla_bwd_call(
        q, k, v, gk, _state_pack_t(state_bwd[:, 0]), _gla_constants(True), o_f, xpool, ogs,
        xp2, xs2, pos, mod, _pool_band(), pool_w[0].astype(BF16), pool_scale[0][None, :],
        gla_norm_g[0][None, :], w_o[0].astype(BF16), ln1_g[0][None, :], ln1_b[0][None, :])

    keys = peer_keys[0].reshape(2 * PEER_HEADS, N_KEYS, N_KEYS).astype(BF16)
    h2t, cnt, g1, r2, e2, u_bf, vt_bf = _route_call(x1, mod, wqt, keys, peer_u[0], peer_v[0])
    yp, ys = _dense_call(h2t, cnt, g1, r2, e2, u_bf, vt_bf, x1, mod, ln2_g[0][None, :], ln2_b[0][None, :])

    y_prompt = yp.reshape(N_CTX_SEQ, CTX_LEN, D_MODEL)
    y_sample = ys.reshape(N_LAT_SEQ, LAT_LEN, D_MODEL)
    return (y_prompt, y_sample, _state_unpack_t(st_f), _state_unpack_t(st_b))
```
